```python
import jax, jax.numpy as jnp
from jax import lax
import numpy as np

D_MODEL = 1024
BATCH = 4
SEQ = 8192
DEPTH = 1

GRID_W = 64
NA_HEADS = 8
NA_HEAD_DIM = 64
NA_ROWS = 8
NA_COLS = 16
SWA_Q_HEADS = 8
SWA_KV_HEADS = 2
SWA_HEAD_DIM = 64
SWA_WINDOW = 128
SWA_BLOCK = 128
ROPE_THETA = 10000.0
N_EXPERTS = 32
TOP_K = 4
D_FF_EXPERT = 1024
SWIGLU_LIMIT = 7.0
SWIGLU_ALPHA = 1.702
EXPERT_BLOCK = 256
RMS_EPS = 1e-5

NA_WIDTH = NA_HEADS * NA_HEAD_DIM
SWA_Q_WIDTH = SWA_Q_HEADS * SWA_HEAD_DIM
SWA_KV_WIDTH = SWA_KV_HEADS * SWA_HEAD_DIM
IN_SIZES = (NA_WIDTH, NA_WIDTH, NA_WIDTH, SWA_Q_WIDTH, SWA_KV_WIDTH, SWA_KV_WIDTH, D_MODEL, D_MODEL)
D_IN = sum(IN_SIZES)
IN_SPLITS = tuple(int(v) for v in np.cumsum(IN_SIZES)[:-1])

kernel_name = 'hybrid_natten_swa_moe_encoder'


def rmsnorm(x, g):
    xf = x.astype(jnp.float32)
    y = xf * lax.rsqrt(jnp.mean(xf * xf, axis=-1, keepdims=True) + RMS_EPS)
    return (y * g.astype(jnp.float32)).astype(x.dtype)


def rope(x, pos):
    half = x.shape[-1] // 2
    inv_freq = ROPE_THETA ** (-jnp.arange(half, dtype=jnp.float32) / half)
    ang = pos.astype(jnp.float32)[:, None] * inv_freq[None, :]
    cos = jnp.cos(ang)[None, :, None, :]
    sin = jnp.sin(ang)[None, :, None, :]
    xf = x.astype(jnp.float32)
    x1, x2 = xf[..., :half], xf[..., half:]
    return jnp.concatenate([x1 * cos - x2 * sin, x2 * cos + x1 * sin], axis=-1).astype(x.dtype)


def neighborhood_attention(q, k, v, rpb):
    b, s, h, dh = q.shape
    rows = s // GRID_W
    kr = min(NA_ROWS, rows)
    scale = dh ** -0.5
    qg = q.reshape(b, rows, GRID_W, h, dh)
    kg = k.reshape(b, rows, GRID_W, h, dh)
    vg = v.reshape(b, rows, GRID_W, h, dh)
    col = np.arange(GRID_W)
    cstart = np.clip(col - NA_COLS // 2, 0, GRID_W - NA_COLS)
    cidx = cstart[:, None] + np.arange(NA_COLS)[None, :]
    rpb_c = rpb[:, :, cidx - col[:, None] + NA_COLS - 1]

    def one_row(r):
        rs = jnp.clip(r - kr // 2, 0, rows - kr)
        kw = lax.dynamic_slice_in_dim(kg, rs, kr, axis=1)[:, :, cidx]
        vw = lax.dynamic_slice_in_dim(vg, rs, kr, axis=1)[:, :, cidx]
        qr = lax.dynamic_index_in_dim(qg, r, axis=1, keepdims=False)
        sc = jnp.einsum('bchd,bicjhd->bhcij', qr, kw).astype(jnp.float32) * scale
        dr = rs + jnp.arange(kr) - r + NA_ROWS - 1
        bias = jnp.take(rpb_c, dr, axis=1).transpose(0, 2, 1, 3)
        sc = sc + bias[None].astype(jnp.float32)
        p = jax.nn.softmax(sc.reshape(b, h, GRID_W, kr * NA_COLS), axis=-1)
        p = p.reshape(b, h, GRID_W, kr, NA_COLS).astype(v.dtype)
        return jnp.einsum('bhcij,bicjhd->bchd', p, vw)

    out = lax.map(one_row, jnp.arange(rows))
    return out.transpose(1, 0, 2, 3, 4).reshape(b, s, h * dh)


def sliding_window_gqa(q, k, v, sinks):
    b, s, hq, dh = q.shape
    hkv = k.shape[2]
    grp = hq // hkv
    nb = s // SWA_BLOCK
    scale = dh ** -0.5
    pad = ((0, 0), (SWA_BLOCK, SWA_BLOCK), (0, 0), (0, 0))
    kp = jnp.pad(k, pad).reshape(b, nb + 2, SWA_BLOCK, hkv, dh)
    vp = jnp.pad(v, pad).reshape(b, nb + 2, SWA_BLOCK, hkv, dh)
    kband = jnp.concatenate([kp[:, :-2], kp[:, 1:-1], kp[:, 2:]], axis=2)
    vband = jnp.concatenate([vp[:, :-2], vp[:, 1:-1], vp[:, 2:]], axis=2)
    qb = q.reshape(b, nb, SWA_BLOCK, hkv, grp, dh)
    sc = jnp.einsum('bnqkgd,bnjkd->bkgnqj', qb, kband).astype(jnp.float32) * scale
    qpos = jnp.arange(SWA_BLOCK)[:, None]
    koff = jnp.arange(3 * SWA_BLOCK)[None, :] - SWA_BLOCK
    rel_ok = jnp.abs(koff - qpos) <= SWA_WINDOW
    kabs = jnp.arange(nb)[:, None] * SWA_BLOCK + koff
    in_seq = (kabs >= 0) & (kabs < s)
    mask = rel_ok[None, :, :] & in_seq[:, None, :]
    sc = jnp.where(mask[None, None, None], sc, -jnp.inf)
    sink = sinks.astype(jnp.float32).reshape(1, hkv, grp, 1, 1, 1)
    m = jnp.maximum(jnp.max(sc, axis=-1, keepdims=True), sink)
    e = jnp.exp(sc - m)
    p = (e / (jnp.sum(e, axis=-1, keepdims=True) + jnp.exp(sink - m))).astype(v.dtype)
    o = jnp.einsum('bkgnqj,bnjkd->bnqkgd', p, vband)
    return o.reshape(b, s, hq * dh)


def moe_clamped_swiglu(h, w_router, b_router, w1, b1, w2, b2):
    b, s, d = h.shape
    t = b * s
    xt = h.reshape(t, d)
    logits = (xt @ w_router + b_router).astype(jnp.float32)
    top_v, top_i = lax.top_k(logits, TOP_K)
    gates = jax.nn.softmax(top_v, axis=-1)
    n_assign = t * TOP_K
    e_id = top_i.reshape(n_assign).astype(jnp.int32)
    tok = jnp.arange(n_assign, dtype=jnp.int32) // TOP_K
    wts = gates.reshape(n_assign)
    order = jnp.argsort(e_id, stable=True)
    e_sorted = e_id[order]
    counts = jnp.bincount(e_id, length=N_EXPERTS)
    start = jnp.cumsum(counts) - counts
    padded = (counts + EXPERT_BLOCK - 1) // EXPERT_BLOCK * EXPERT_BLOCK
    pend = jnp.cumsum(padded)
    pstart = pend - padded
    dest = pstart[e_sorted] + jnp.arange(n_assign, dtype=jnp.int32) - start[e_sorted]
    n_rows = n_assign + N_EXPERTS * EXPERT_BLOCK
    n_blk = n_rows // EXPERT_BLOCK
    row_tok = jnp.full((n_rows,), t, jnp.int32).at[dest].set(tok[order])
    row_w = jnp.zeros((n_rows,), jnp.float32).at[dest].set(wts[order])
    blk_exp = jnp.minimum(jnp.searchsorted(pend, jnp.arange(n_blk) * EXPERT_BLOCK, side='right'), N_EXPERTS - 1)
    x_ext = jnp.concatenate([xt, jnp.zeros((1, d), xt.dtype)], axis=0)

    def expert_block(args):
        rows, ex = args
        xb = x_ext[rows]
        gu = xb @ w1[ex] + b1[ex]
        gate = jnp.minimum(gu[:, :D_FF_EXPERT], SWIGLU_LIMIT)
        up = jnp.clip(gu[:, D_FF_EXPERT:], -SWIGLU_LIMIT, SWIGLU_LIMIT)
        glu = gate * jax.nn.sigmoid(gate * SWIGLU_ALPHA)
        return ((up + 1.0) * glu) @ w2[ex] + b2[ex]

    yb = lax.map(expert_block, (row_tok.reshape(n_blk, EXPERT_BLOCK), blk_exp))
    contrib = yb.reshape(n_rows, d).astype(jnp.float32) * row_w[:, None]
    y = jax.ops.segment_sum(contrib, row_tok, num_segments=t + 1)[:t]
    return y.reshape(b, s, d).astype(h.dtype)


def setup_inputs(seed: int = 0) -> dict:
    key = jax.random.key(seed)
    ks = jax.random.split(key, 18)
    f32 = jnp.float32

    def nrm(k, shape, scale):
        return jax.random.normal(k, shape, f32) * scale

    return {
        'x': nrm(ks[0], (BATCH, SEQ, D_MODEL), 1.0),
        'norm1_g': 1.0 + nrm(ks[1], (DEPTH, D_MODEL), 0.02),
        'w_in': nrm(ks[2], (DEPTH, D_MODEL, D_IN), D_MODEL ** -0.5),
        'b_in': nrm(ks[3], (DEPTH, D_IN), 0.02),
        'na_rpb': nrm(ks[4], (DEPTH, NA_HEADS, 2 * NA_ROWS - 1, 2 * NA_COLS - 1), 0.5),
        'swa_sinks': nrm(ks[5], (DEPTH, SWA_Q_HEADS), 1.0),
        'w_up_a': nrm(ks[6], (DEPTH, NA_WIDTH, D_MODEL), NA_WIDTH ** -0.5),
        'w_up_b': nrm(ks[7], (DEPTH, SWA_Q_WIDTH, D_MODEL), SWA_Q_WIDTH ** -0.5),
        'w_out': nrm(ks[8], (DEPTH, D_MODEL, D_MODEL), D_MODEL ** -0.5),
        'norm2_g': 1.0 + nrm(ks[9], (DEPTH, D_MODEL), 0.02),
        'w_router': nrm(ks[10], (DEPTH, D_MODEL, N_EXPERTS), D_MODEL ** -0.5),
        'b_router': nrm(ks[11], (DEPTH, N_EXPERTS), 0.01),
        'w1': nrm(ks[12], (DEPTH, N_EXPERTS, D_MODEL, 2 * D_FF_EXPERT), D_MODEL ** -0.5),
        'b1': nrm(ks[13], (DEPTH, N_EXPERTS, 2 * D_FF_EXPERT), 0.02),
        'w2': nrm(ks[14], (DEPTH, N_EXPERTS, D_FF_EXPERT, D_MODEL), D_FF_EXPERT ** -0.5),
        'b2': nrm(ks[15], (DEPTH, N_EXPERTS, D_MODEL), 0.02),
        'final_g': 1.0 + nrm(ks[16], (D_MODEL,), 0.02),
    }


def reference(x, norm1_g, w_in, b_in, na_rpb, swa_sinks, w_up_a, w_up_b, w_out,
              norm2_g, w_router, b_router, w1, b1, w2, b2, final_g):
    b, s, _ = x.shape
    pos = jnp.arange(s, dtype=jnp.int32)
    for l in range(DEPTH):
        h = rmsnorm(x, norm1_g[l])
        proj = h @ w_in[l] + b_in[l]
        qa, ka, va, qb, kb, vb, ga, gb = jnp.split(proj, IN_SPLITS, axis=-1)
        oa = neighborhood_attention(
            qa.reshape(b, s, NA_HEADS, NA_HEAD_DIM),
            ka.reshape(b, s, NA_HEADS, NA_HEAD_DIM),
            va.reshape(b, s, NA_HEADS, NA_HEAD_DIM),
            na_rpb[l])
        ob = sliding_window_gqa(
            rope(qb.reshape(b, s, SWA_Q_HEADS, SWA_HEAD_DIM), pos),
            rope(kb.reshape(b, s, SWA_KV_HEADS, SWA_HEAD_DIM), pos),
            vb.reshape(b, s, SWA_KV_HEADS, SWA_HEAD_DIM),
            swa_sinks[l])
        merged = jax.nn.sigmoid(ga) * (oa @ w_up_a[l]) + jax.nn.sigmoid(gb) * (ob @ w_up_b[l])
        x = x + merged @ w_out[l]
        h2 = rmsnorm(x, norm2_g[l])
        x = x + moe_clamped_swiglu(h2, w_router[l], b_router[l], w1[l], b1[l], w2[l], b2[l])
    return rmsnorm(x, final_g)
```

```python
import functools

import jax
import jax.numpy as jnp
import numpy as np
from jax import lax
from jax.experimental import pallas as pl
from jax.experimental.pallas import tpu as pltpu

D_MODEL = 1024
GRID_W = 64
NA_HEADS = 8
NA_HEAD_DIM = 64
NA_ROWS = 8
NA_COLS = 16
SWA_Q_HEADS = 8
SWA_KV_HEADS = 2
SWA_HEAD_DIM = 64
SWA_WINDOW = 128
SWA_BLOCK = 128
ROPE_THETA = 10000.0
N_EXPERTS = 32
TOP_K = 4
D_FF = 1024
SWIGLU_LIMIT = 7.0
SWIGLU_ALPHA = 1.702
EXPERT_BLOCK = 256
RMS_EPS = 1e-5

NA_WIDTH = NA_HEADS * NA_HEAD_DIM
SWA_Q_WIDTH = SWA_Q_HEADS * SWA_HEAD_DIM
SWA_KV_WIDTH = SWA_KV_HEADS * SWA_HEAD_DIM
LANES = 128
SUBLANES = 8
NEG_BIG = -1e30

C_QA, C_KA, C_VA = 0, NA_WIDTH, 2 * NA_WIDTH
C_QB = 3 * NA_WIDTH
C_KB = C_QB + SWA_Q_WIDTH
C_VB = C_KB + SWA_KV_WIDTH
C_GA = C_VB + SWA_KV_WIDTH
C_GB = C_GA + D_MODEL
D_IN = C_GB + D_MODEL

TM_PROJ = 512
TM_MIX = 512
TM_OUT = 512
VMEM_LIMIT = 56 * 1024 * 1024

_BF16 = jnp.bfloat16
_F32 = jnp.float32


def _const_spec(shape):
    nd = len(shape)
    return pl.BlockSpec(shape, lambda *_: (0,) * nd, pipeline_mode=pl.Buffered(1))


def _rope_slab(y, cos, sin_signed):
    lane = lax.broadcasted_iota(jnp.int32, y.shape, 1)
    first_half = (lane & (SWA_HEAD_DIM - 1)) < (SWA_HEAD_DIM // 2)
    rot = jnp.where(first_half, pltpu.roll(y, LANES - SWA_HEAD_DIM // 2, axis=1),
                    pltpu.roll(y, SWA_HEAD_DIM // 2, axis=1))
    return y * cos + rot * sin_signed


def _inproj_kernel(x_ref, g_ref, w_ref, b_ref, cos_ref, sin_ref,
                   qkva_ref, qb_ref, kvb_ref, gate_ref):
    x = x_ref[...]
    var = jnp.mean(x * x, axis=-1, keepdims=True)
    h = (x * lax.rsqrt(var + RMS_EPS) * g_ref[...]).astype(_BF16)

    def proj(c0, c1):
        return jnp.dot(h, w_ref[:, c0:c1], preferred_element_type=_F32) + b_ref[:, c0:c1]

    scale = NA_HEAD_DIM ** -0.5
    qkva_ref[:, C_QA:C_KA] = (proj(C_QA, C_KA) * scale).astype(_BF16)
    qkva_ref[:, C_KA:C_VA] = proj(C_KA, C_VA).astype(_BF16)
    qkva_ref[:, C_VA:C_QB] = proj(C_VA, C_QB).astype(_BF16)

    cos = cos_ref[...]
    sin = sin_ref[...]
    qb = proj(C_QB, C_KB)
    for s in range(SWA_Q_WIDTH // LANES):
        slab = _rope_slab(qb[:, s * LANES:(s + 1) * LANES], cos, sin)
        qb_ref[:, s * LANES:(s + 1) * LANES] = (slab * (SWA_HEAD_DIM ** -0.5)).astype(_BF16)
    kvb = proj(C_KB, C_GA)
    kvb_ref[:, 0:LANES] = _rope_slab(kvb[:, 0:LANES], cos, sin).astype(_BF16)
    kvb_ref[:, LANES:2 * LANES] = kvb[:, LANES:2 * LANES].astype(_BF16)

    for c0 in range(C_GA, D_IN, 512):
        gate_ref[:, c0 - C_GA:c0 - C_GA + 512] = jax.nn.sigmoid(proj(c0, c0 + 512)).astype(_BF16)


def _inproj(x2, g1, w_in, b_in, cos_t, sin_t, seq):
    t = x2.shape[0]
    tm = TM_PROJ
    nseq = seq // tm
    row = lambda i: (i, 0)
    return pl.pallas_call(
        _inproj_kernel,
        grid=(t // tm,),
        in_specs=[
            pl.BlockSpec((tm, D_MODEL), row),
            _const_spec((1, D_MODEL)),
            _const_spec((D_MODEL, D_IN)),
            _const_spec((1, D_IN)),
            pl.BlockSpec((tm, LANES), lambda i: (i % nseq, 0)),
            pl.BlockSpec((tm, LANES), lambda i: (i % nseq, 0)),
        ],
        out_specs=[
            pl.BlockSpec((tm, 3 * NA_WIDTH), row),
            pl.BlockSpec((tm, SWA_Q_WIDTH), row),
            pl.BlockSpec((tm, 2 * SWA_KV_WIDTH), row),
            pl.BlockSpec((tm, 2 * D_MODEL), row),
        ],
        out_shape=[
            jax.ShapeDtypeStruct((t, 3 * NA_WIDTH), _BF16),
            jax.ShapeDtypeStruct((t, SWA_Q_WIDTH), _BF16),
            jax.ShapeDtypeStruct((t, 2 * SWA_KV_WIDTH), _BF16),
            jax.ShapeDtypeStruct((t, 2 * D_MODEL), _BF16),
        ],
        compiler_params=pltpu.CompilerParams(
            dimension_semantics=("parallel",), vmem_limit_bytes=VMEM_LIMIT),
        name="inproj",
    )(x2, g1, w_in, b_in, cos_t, sin_t)


_NT_DIMS = (((1,), (1,)), ((), ()))


def _mixer_kernel(sink_ref, x_ref, q_ref, kp_ref, kc_ref, kn_ref, vp_ref, vc_ref, vn_ref,
                  qb_ref, kvp_ref, kvc_ref, kvn_ref, gate_ref, tbl_ref,
                  wua_ref, wub_ref, wo_ref, g2_ref, wr_ref, br_ref, tri_ref,
                  x1_ref, h2_ref, ids_ref, wts_ref, rank_ref, cnt_ref,
                  kcat, vcat, kvcat, oa_s, ob_s, *, rows, n_swa_blocks):
    j = pl.program_id(1)
    rows_per_tile = TM_MIX // GRID_W
    halo = (NA_ROWS // 2) * GRID_W
    band = NA_ROWS * GRID_W

    kcat[0:halo] = kp_ref[...]
    kcat[halo:halo + TM_MIX] = kc_ref[...]
    kcat[halo + TM_MIX:] = kn_ref[...]
    vcat[0:halo] = vp_ref[...]
    vcat[halo:halo + TM_MIX] = vc_ref[...]
    vcat[halo + TM_MIX:] = vn_ref[...]

    lane_q = lax.broadcasted_iota(jnp.int32, (GRID_W, LANES), 1)
    low_q = lane_q < NA_HEAD_DIM
    lane_o = lax.broadcasted_iota(jnp.int32, (GRID_W, LANES), 1) < NA_HEAD_DIM

    def na_row(i, carry):
        r = j * rows_per_tile + i
        rs = jnp.clip(r - NA_ROWS // 2, 0, rows - NA_ROWS)
        d = r - rs
        start = pl.multiple_of((rs - (j * rows_per_tile - NA_ROWS // 2)) * GRID_W, GRID_W)
        q0 = pl.multiple_of(i * GRID_W, GRID_W)
        for p in range(NA_HEADS // 2):
            cols = slice(p * LANES, (p + 1) * LANES)
            qpair = q_ref[pl.ds(q0, GRID_W), cols]
            zero = jnp.zeros_like(qpair)
            qs = jnp.concatenate([jnp.where(low_q, qpair, zero), jnp.where(low_q, zero, qpair)], axis=0)
            kb = kcat[pl.ds(start, band), cols]
            vb = vcat[pl.ds(start, band), cols]
            s = lax.dot_general(qs, kb, _NT_DIMS, preferred_element_type=_F32) + tbl_ref[d, p]
            m = jnp.max(s, axis=-1, keepdims=True)
            e = jnp.exp(s - m)
            l = jnp.sum(e, axis=-1, keepdims=True)
            o = jnp.dot(e.astype(_BF16), vb, preferred_element_type=_F32) * (1.0 / l)
            oa_s[pl.ds(q0, GRID_W), cols] = jnp.where(lane_o, o[:GRID_W], o[GRID_W:]).astype(_BF16)
        return carry

    lax.fori_loop(0, rows_per_tile, na_row, 0)

    kvcat[0:SWA_BLOCK] = kvp_ref[...]
    kvcat[SWA_BLOCK:SWA_BLOCK + TM_MIX] = kvc_ref[...]
    kvcat[SWA_BLOCK + TM_MIX:] = kvn_ref[...]

    n_slabs = SWA_Q_WIDTH // LANES
    stack = n_slabs * SWA_BLOCK
    lane_s = lax.broadcasted_iota(jnp.int32, (SWA_BLOCK, LANES), 1) < SWA_HEAD_DIM
    qpos = lax.broadcasted_iota(jnp.int32, (stack, 3 * SWA_BLOCK), 0) & (SWA_BLOCK - 1)
    kcol = lax.broadcasted_iota(jnp.int32, (stack, 3 * SWA_BLOCK), 1)
    rel_ok = jnp.abs(kcol - SWA_BLOCK - qpos) <= SWA_WINDOW
    rowblk = lax.broadcasted_iota(jnp.int32, (stack, 1), 0) // SWA_BLOCK

    def swa_block(n, carry):
        nb = j * (TM_MIX // SWA_BLOCK) + n
        t0 = pl.multiple_of(n * SWA_BLOCK, SWA_BLOCK)
        ok = rel_ok & ((kcol >= SWA_BLOCK) | (nb > 0)) & ((kcol < 2 * SWA_BLOCK) | (nb < n_swa_blocks - 1))
        kband = kvcat[pl.ds(t0, 3 * SWA_BLOCK), 0:LANES]
        vband = kvcat[pl.ds(t0, 3 * SWA_BLOCK), LANES:2 * LANES]
        outs = []
        for g in range(SWA_KV_HEADS):
            parts = []
            for s_ in range(n_slabs):
                slab = qb_ref[pl.ds(t0, SWA_BLOCK), s_ * LANES:(s_ + 1) * LANES]
                zero = jnp.zeros_like(slab)
                parts.append(jnp.where(lane_s, slab, zero) if g == 0 else jnp.where(lane_s, zero, slab))
            qg = jnp.concatenate(parts, axis=0)
            sink = jnp.zeros((stack, 1), _F32)
            for s_ in range(n_slabs):
                sink = jnp.where(rowblk == s_, sink_ref[SWA_KV_HEADS * s_ + g], sink)
            s = lax.dot_general(qg, kband, _NT_DIMS, preferred_element_type=_F32)
            s = jnp.where(ok, s, NEG_BIG)
            m = jnp.maximum(jnp.max(s, axis=-1, keepdims=True), sink)
            e = jnp.exp(s - m)
            den = jnp.sum(e, axis=-1, keepdims=True) + jnp.exp(sink - m)
            outs.append(jnp.dot(e.astype(_BF16), vband, preferred_element_type=_F32) * (1.0 / den))
        for s_ in range(n_slabs):
            rs_ = slice(s_ * SWA_BLOCK, (s_ + 1) * SWA_BLOCK)
            ob_s[pl.ds(t0, SWA_BLOCK), s_ * LANES:(s_ + 1) * LANES] = jnp.where(
                lane_s, outs[0][rs_], outs[1][rs_]).astype(_BF16)
        return carry

    lax.fori_loop(0, TM_MIX // SWA_BLOCK, swa_block, 0)

    ua = jnp.dot(oa_s[...], wua_ref[...], preferred_element_type=_F32)
    ub = jnp.dot(ob_s[...], wub_ref[...], preferred_element_type=_F32)
    merged = (gate_ref[:, 0:D_MODEL].astype(_F32) * ua
              + gate_ref[:, D_MODEL:].astype(_F32) * ub).astype(_BF16)
    x1 = x_ref[...] + jnp.dot(merged, wo_ref[...], preferred_element_type=_F32)
    x1_ref[...] = x1

    var = jnp.mean(x1 * x1, axis=-1, keepdims=True)
    h2 = x1 * lax.rsqrt(var + RMS_EPS) * g2_ref[...]
    h2_ref[...] = h2
    logits = lax.dot_general(wr_ref[...], h2, _NT_DIMS, preferred_element_type=_F32,
                             precision=lax.Precision.HIGHEST) + br_ref[...]
    eidx = lax.broadcasted_iota(jnp.int32, logits.shape, 0)
    vals, idxs, hots = [], [], []
    for _ in range(TOP_K):
        m = jnp.max(logits, axis=0, keepdims=True)
        idx = jnp.min(jnp.where(logits == m, eidx, N_EXPERTS), axis=0, keepdims=True)
        hot = eidx == idx
        logits = jnp.where(hot, -jnp.inf, logits)
        vals.append(m)
        idxs.append(idx)
        hots.append(hot)
    es = [jnp.exp(v - vals[0]) for v in vals]
    inv = 1.0 / (es[0] + es[1] + es[2] + es[3])
    sel = jnp.zeros(logits.shape, _F32)
    for hot in hots:
        sel = sel + jnp.where(hot, 1.0, 0.0)
    prefix = jnp.dot(sel.astype(_BF16), tri_ref[...], preferred_element_type=_F32)
    ranks = [jnp.sum(jnp.where(hot, prefix, 0.0), axis=0, keepdims=True).astype(jnp.int32) for hot in hots]
    pad_i = jnp.zeros((8 - TOP_K, TM_MIX), jnp.int32)
    ids_ref[...] = jnp.concatenate(idxs + [pad_i], axis=0)
    rank_ref[...] = jnp.concatenate(ranks + [pad_i], axis=0)
    wts_ref[...] = jnp.concatenate([e * inv for e in es] + [jnp.zeros((8 - TOP_K, TM_MIX), _F32)], axis=0)
    cnt_ref[...] = jnp.broadcast_to(jnp.sum(sel, axis=1, keepdims=True), (N_EXPERTS, LANES))


def _mixer(sinks_perm, x2, qkva, qb, kvb, gates, tbl, wua, wub, wo, g2, wr_t, br, tri, batch, seq):
    t = x2.shape[0]
    tm = TM_MIX
    nj = seq // tm
    rows = seq // GRID_W
    hb = tm // ((NA_ROWS // 2) * GRID_W)
    sb = tm // SWA_BLOCK
    n_halo = seq // ((NA_ROWS // 2) * GRID_W)
    n_swa = seq // SWA_BLOCK
    halo = (NA_ROWS // 2) * GRID_W

    tile = lambda b, j, *_: (b * nj + j, 0)

    def na_spec(col, which):
        if which == 0:
            return pl.BlockSpec((tm, NA_WIDTH), lambda b, j, *_: (b * nj + j, col))
        if which < 0:
            return pl.BlockSpec((halo, NA_WIDTH),
                                lambda b, j, *_: (b * n_halo + jnp.maximum(j * hb - 1, 0), col))
        return pl.BlockSpec((halo, NA_WIDTH),
                            lambda b, j, *_: (b * n_halo + jnp.minimum(j * hb + hb, n_halo - 1), col))

    kv_prev = pl.BlockSpec((SWA_BLOCK, 2 * SWA_KV_WIDTH),
                           lambda b, j, *_: (b * n_swa + jnp.maximum(j * sb - 1, 0), 0))
    kv_next = pl.BlockSpec((SWA_BLOCK, 2 * SWA_KV_WIDTH),
                           lambda b, j, *_: (b * n_swa + jnp.minimum(j * sb + sb, n_swa - 1), 0))

    def cspec(shape):
        nd = len(shape)
        return pl.BlockSpec(shape, lambda *_: (0,) * nd, pipeline_mode=pl.Buffered(1))

    grid_spec = pltpu.PrefetchScalarGridSpec(
        num_scalar_prefetch=1,
        grid=(batch, nj),
        in_specs=[
            pl.BlockSpec((tm, D_MODEL), tile),
            na_spec(0, 0),
            na_spec(1, -1), na_spec(1, 0), na_spec(1, 1),
            na_spec(2, -1), na_spec(2, 0), na_spec(2, 1),
            pl.BlockSpec((tm, SWA_Q_WIDTH), tile),
            kv_prev, pl.BlockSpec((tm, 2 * SWA_KV_WIDTH), tile), kv_next,
            pl.BlockSpec((tm, 2 * D_MODEL), tile),
            cspec(tbl.shape),
            cspec(wua.shape), cspec(wub.shape), cspec(wo.shape),
            cspec(g2.shape), cspec(wr_t.shape), cspec(br.shape), cspec(tri.shape),
        ],
        out_specs=[
            pl.BlockSpec((tm, D_MODEL), tile),
            pl.BlockSpec((tm, D_MODEL), tile),
            pl.BlockSpec((8, tm), lambda b, j, *_: (0, b * nj + j)),
            pl.BlockSpec((8, tm), lambda b, j, *_: (0, b * nj + j)),
            pl.BlockSpec((8, tm), lambda b, j, *_: (0, b * nj + j)),
            pl.BlockSpec((N_EXPERTS, LANES), tile),
        ],
        scratch_shapes=[
            pltpu.VMEM((tm + 2 * halo, NA_WIDTH), _BF16),
            pltpu.VMEM((tm + 2 * halo, NA_WIDTH), _BF16),
            pltpu.VMEM((tm + 2 * SWA_BLOCK, 2 * SWA_KV_WIDTH), _BF16),
            pltpu.VMEM((tm, NA_WIDTH), _BF16),
            pltpu.VMEM((tm, SWA_Q_WIDTH), _BF16),
        ],
    )
    return pl.pallas_call(
        functools.partial(_mixer_kernel, rows=rows, n_swa_blocks=n_swa),
        grid_spec=grid_spec,
        out_shape=[
            jax.ShapeDtypeStruct((t, D_MODEL), _F32),
            jax.ShapeDtypeStruct((t, D_MODEL), _F32),
            jax.ShapeDtypeStruct((8, t), jnp.int32),
            jax.ShapeDtypeStruct((8, t), _F32),
            jax.ShapeDtypeStruct((8, t), jnp.int32),
            jax.ShapeDtypeStruct((t // tm * N_EXPERTS, LANES), _F32),
        ],
        compiler_params=pltpu.CompilerParams(
            dimension_semantics=("parallel", "parallel"), vmem_limit_bytes=VMEM_LIMIT),
        name="mixer",
    )(sinks_perm, x2, qkva, qkva, qkva, qkva, qkva, qkva, qkva, qb, kvb, kvb, kvb, gates, tbl,
      wua, wub, wo, g2, wr_t, br, tri)


def _expert_kernel(bexp_ref, nvalid_ref, nact_ref, rowa_hbm, h2_hbm, w1_ref, b1_ref, w2_ref, b2_ref, y_hbm,
                   idx_s, xbuf, ybuf, isem, gsem, ssem, *, n_tokens):
    i = pl.program_id(0)
    nact = nact_ref[0]
    eb = EXPERT_BLOCK

    def idx_copy(blk):
        return pltpu.make_async_copy(rowa_hbm.at[blk], idx_s.at[blk % 3], isem.at[blk % 3])

    def issue_gather(blk):
        islot = blk % 3
        xslot = blk % 2

        def body(r, c):
            tok = jnp.maximum(idx_s[islot, r], 0) >> 2
            pltpu.make_async_copy(h2_hbm.at[pl.ds(tok, 1), :], xbuf.at[xslot, pl.ds(r, 1), :],
                                  gsem.at[xslot]).start()
            return c

        lax.fori_loop(0, eb, body, 0)

    def wait_gather(blk):
        xslot = blk % 2
        pltpu.make_async_copy(h2_hbm.at[pl.ds(0, eb), :], xbuf.at[xslot], gsem.at[xslot]).wait()

    def issue_scatter(blk):
        islot = blk % 3
        yslot = blk % 2

        def body(r, c):
            a = idx_s[islot, r]
            dst = (a & (TOP_K - 1)) * n_tokens + (a >> 2)
            pltpu.make_async_copy(ybuf.at[yslot, pl.ds(r, 1), :], y_hbm.at[pl.ds(dst, 1), :],
                                  ssem.at[yslot]).start()
            return c

        lax.fori_loop(0, nvalid_ref[blk], body, 0)

    def wait_scatter(blk):
        yslot = blk % 2
        n = nvalid_ref[blk]
        n_tiled = pl.multiple_of((n // SUBLANES) * SUBLANES, SUBLANES)

        @pl.when(n_tiled > 0)
        def _():
            pltpu.make_async_copy(ybuf.at[yslot, pl.ds(0, n_tiled), :], y_hbm.at[pl.ds(0, n_tiled), :],
                                  ssem.at[yslot]).wait()

        def one_row(r, c):
            pltpu.make_async_copy(ybuf.at[yslot, pl.ds(0, 1), :], y_hbm.at[pl.ds(0, 1), :],
                                  ssem.at[yslot]).wait()
            return c

        lax.fori_loop(0, n - n_tiled, one_row, 0)

    @pl.when(i == 0)
    def _():
        idx_copy(0).start()
        idx_copy(0).wait()
        issue_gather(0)

        @pl.when(nact > 1)
        def _():
            idx_copy(1).start()

    @pl.when(i < nact)
    def _():
        @pl.when(i + 2 < nact)
        def _():
            idx_copy(i + 2).start()

        @pl.when(i + 1 < nact)
        def _():
            idx_copy(i + 1).wait()
            issue_gather(i + 1)

        wait_gather(i)
        slot = i % 2
        x = xbuf[slot].astype(_BF16)
        gu = jnp.dot(x, w1_ref[...], preferred_element_type=_F32) + b1_ref[...]
        gate = jnp.minimum(gu[:, :D_FF], SWIGLU_LIMIT)
        up = jnp.clip(gu[:, D_FF:], -SWIGLU_LIMIT, SWIGLU_LIMIT)
        act = ((up + 1.0) * (gate * jax.nn.sigmoid(gate * SWIGLU_ALPHA))).astype(_BF16)
        ybuf[slot] = jnp.dot(act, w2_ref[...], preferred_element_type=_F32) + b2_ref[...]
        issue_scatter(i)

        @pl.when(i > 0)
        def _():
            wait_scatter(i - 1)

        @pl.when(i == nact - 1)
        def _():
            wait_scatter(i)


def _experts(blk_exp, n_valid, n_active, row_a, h2, w1, b1, w2, b2, n_tokens):
    n_blk = row_a.shape[0]
    eb = EXPERT_BLOCK
    grid_spec = pltpu.PrefetchScalarGridSpec(
        num_scalar_prefetch=3,
        grid=(n_blk,),
        in_specs=[
            pl.BlockSpec(memory_space=pl.ANY),
            pl.BlockSpec(memory_space=pl.ANY),
            pl.BlockSpec((None, D_MODEL, 2 * D_FF), lambda i, be, nv, na: (be[i], 0, 0)),
            pl.BlockSpec((None, 1, 2 * D_FF), lambda i, be, nv, na: (be[i], 0, 0)),
            pl.BlockSpec((None, D_FF, D_MODEL), lambda i, be, nv, na: (be[i], 0, 0)),
            pl.BlockSpec((None, 1, D_MODEL), lambda i, be, nv, na: (be[i], 0, 0)),
        ],
        out_specs=pl.BlockSpec(memory_space=pl.ANY),
        scratch_shapes=[
            pltpu.SMEM((3, eb), jnp.int32),
            pltpu.VMEM((2, eb, D_MODEL), _F32),
            pltpu.VMEM((2, eb, D_MODEL), _F32),
            pltpu.SemaphoreType.DMA((3,)),
            pltpu.SemaphoreType.DMA((2,)),
            pltpu.SemaphoreType.DMA((2,)),
        ],
    )
    return pl.pallas_call(
        functools.partial(_expert_kernel, n_tokens=n_tokens),
        grid_spec=grid_spec,
        out_shape=jax.ShapeDtypeStruct((TOP_K * n_tokens, D_MODEL), _F32),
        compiler_params=pltpu.CompilerParams(
            dimension_semantics=("arbitrary",), vmem_limit_bytes=VMEM_LIMIT),
        name="experts",
    )(blk_exp, n_valid, n_active, row_a, h2, w1, b1, w2, b2)


def _combine_kernel(x1_ref, y0_ref, y1_ref, y2_ref, y3_ref, wts_ref, g_ref, o_ref):
    wt = wts_ref[...].T
    y = x1_ref[...]
    for k, y_ref in enumerate((y0_ref, y1_ref, y2_ref, y3_ref)):
        y = y + wt[:, k:k + 1] * y_ref[...]
    var = jnp.mean(y * y, axis=-1, keepdims=True)
    o_ref[...] = y * lax.rsqrt(var + RMS_EPS) * g_ref[...]


def _combine(x1, y, wts, gf):
    t = x1.shape[0]
    tm = TM_OUT
    nt = t // tm
    yspec = lambda k: pl.BlockSpec((tm, D_MODEL), lambda i: (k * nt + i, 0))
    return pl.pallas_call(
        _combine_kernel,
        grid=(nt,),
        in_specs=[
            pl.BlockSpec((tm, D_MODEL), lambda i: (i, 0)),
            yspec(0), yspec(1), yspec(2), yspec(3),
            pl.BlockSpec((8, tm), lambda i: (0, i)),
            _const_spec((1, D_MODEL)),
        ],
        out_specs=pl.BlockSpec((tm, D_MODEL), lambda i: (i, 0)),
        out_shape=jax.ShapeDtypeStruct((t, D_MODEL), _F32),
        compiler_params=pltpu.CompilerParams(
            dimension_semantics=("parallel",), vmem_limit_bytes=VMEM_LIMIT),
        name="combine",
    )(x1, y, y, y, y, wts, gf)


def _rope_tables(seq):
    half = SWA_HEAD_DIM // 2
    inv_freq = ROPE_THETA ** (-jnp.arange(half, dtype=_F32) / half)
    ang = jnp.arange(seq, dtype=_F32)[:, None] * inv_freq[None, :]
    cos, sin = jnp.cos(ang), jnp.sin(ang)
    cos_t = jnp.tile(jnp.concatenate([cos, cos], axis=1), (1, LANES // SWA_HEAD_DIM))
    sin_t = jnp.tile(jnp.concatenate([-sin, sin], axis=1), (1, LANES // SWA_HEAD_DIM))
    return cos_t, sin_t


def _na_bias_table(rpb):
    col = np.arange(GRID_W)
    cstart = np.clip(col - NA_COLS // 2, 0, GRID_W - NA_COLS)
    kc = np.arange(GRID_W)
    valid = (kc[None, :] >= cstart[:, None]) & (kc[None, :] < cstart[:, None] + NA_COLS)
    off = np.clip(kc[None, :] - col[:, None] + NA_COLS - 1, 0, 2 * NA_COLS - 2)
    ext = jnp.where(valid[None, None], rpb[:, :, off], NEG_BIG)
    u = np.arange(NA_ROWS)[None, :] - np.arange(NA_ROWS)[:, None] + NA_ROWS - 1
    tbl = ext[:, u]
    tbl = tbl.transpose(1, 0, 3, 2, 4).reshape(NA_ROWS, NA_HEADS // 2, 2 * GRID_W, NA_ROWS * GRID_W)
    return tbl.astype(_F32)


def kernel(x, norm1_g, w_in, b_in, na_rpb, swa_sinks, w_up_a, w_up_b, w_out, norm2_g, w_router,
           b_router, w1, b1, w2, b2, final_g):
    batch, seq, d = x.shape
    depth = w_in.shape[0]
    t = batch * seq
    assert depth == 1, "the final norm is fused into the single layer's combine step"
    assert d == D_MODEL and seq % TM_MIX == 0 and seq // GRID_W >= 2 * NA_ROWS and t % TM_OUT == 0

    group = SWA_Q_HEADS // SWA_KV_HEADS
    head_order = np.arange(SWA_Q_HEADS).reshape(SWA_KV_HEADS, group).T.reshape(-1)
    qb_cols = (head_order[:, None] * SWA_HEAD_DIM + np.arange(SWA_HEAD_DIM)[None, :]).reshape(-1)
    col_perm = np.arange(D_IN)
    col_perm[C_QB:C_KB] = C_QB + qb_cols
    cos_t, sin_t = _rope_tables(seq)
    tri = (np.arange(TM_MIX)[:, None] < np.arange(TM_MIX)[None, :]).astype(np.float32)
    tri = jnp.asarray(tri, _BF16)

    n_assign = t * TOP_K
    n_rows = n_assign + N_EXPERTS * EXPERT_BLOCK
    n_blk = n_rows // EXPERT_BLOCK
    n_tiles = t // TM_MIX

    x2 = x.reshape(t, d)
    for l in range(depth):
        w_in_l = w_in[l][:, col_perm].astype(_BF16)
        b_in_l = b_in[l][col_perm].reshape(1, D_IN)
        qkva, qb, kvb, gates = _inproj(x2, norm1_g[l].reshape(1, d), w_in_l, b_in_l, cos_t, sin_t, seq)

        x1, h2, ids, wts, rank, cnt = _mixer(
            swa_sinks[l][head_order].astype(_F32), x2, qkva, qb, kvb, gates, _na_bias_table(na_rpb[l]),
            w_up_a[l].astype(_BF16), w_up_b[l][qb_cols].astype(_BF16), w_out[l].astype(_BF16),
            norm2_g[l].reshape(1, d), w_router[l].T, b_router[l].reshape(N_EXPERTS, 1), tri, batch, seq)

        cnt = cnt.reshape(n_tiles, N_EXPERTS, LANES)[:, :, 0].astype(jnp.int32)
        counts = jnp.sum(cnt, axis=0)
        padded = (counts + EXPERT_BLOCK - 1) // EXPERT_BLOCK * EXPERT_BLOCK
        pend = jnp.cumsum(padded)
        base = (pend - padded)[None, :] + jnp.cumsum(cnt, axis=0) - cnt
        tile_of = jnp.arange(t, dtype=jnp.int32) // TM_MIX
        dest = base.reshape(-1)[tile_of[None, :] * N_EXPERTS + ids[:TOP_K]] + rank[:TOP_K]
        assign = jnp.arange(t, dtype=jnp.int32)[None, :] * TOP_K + jnp.arange(TOP_K, dtype=jnp.int32)[:, None]
        row_a = jnp.full((n_rows,), -1, jnp.int32).at[dest.reshape(-1)].set(assign.reshape(-1))
        blk_exp = jnp.minimum(
            jnp.searchsorted(pend, jnp.arange(n_blk, dtype=jnp.int32) * EXPERT_BLOCK, side="right"),
            N_EXPERTS - 1).astype(jnp.int32)
        n_active = (pend[-1:] // EXPERT_BLOCK).astype(jnp.int32)
        blk_start = jnp.arange(n_blk, dtype=jnp.int32) * EXPERT_BLOCK
        n_valid = jnp.clip((pend - padded + counts)[blk_exp] - blk_start, 0, EXPERT_BLOCK).astype(jnp.int32)

        y = _experts(blk_exp, n_valid, n_active, row_a.reshape(n_blk, EXPERT_BLOCK), h2,
                     w1[l].astype(_BF16), b1[l].reshape(N_EXPERTS, 1, 2 * D_FF),
                     w2[l].astype(_BF16), b2[l].reshape(N_EXPERTS, 1, D_MODEL), t)
        x2 = _combine(x1, y, wts, final_g.reshape(1, d))
    return x2.reshape(batch, seq, d)
```

```python
import functools

import jax
import jax.numpy as jnp
import numpy as np
from jax import lax
from jax.experimental import pallas as pl
from jax.experimental.pallas import tpu as pltpu

D_MODEL = 1024
GRID_W = 64
NA_HEADS = 8
NA_HEAD_DIM = 64
NA_ROWS = 8
NA_COLS = 16
SWA_Q_HEADS = 8
SWA_KV_HEADS = 2
SWA_HEAD_DIM = 64
SWA_WINDOW = 128
SWA_BLOCK = 128
ROPE_THETA = 10000.0
N_EXPERTS = 32
TOP_K = 4
D_FF = 1024
SWIGLU_LIMIT = 7.0
SWIGLU_ALPHA = 1.702
EXPERT_BLOCK = 256
RMS_EPS = 1e-5

NA_WIDTH = NA_HEADS * NA_HEAD_DIM
SWA_Q_WIDTH = SWA_Q_HEADS * SWA_HEAD_DIM
SWA_KV_WIDTH = SWA_KV_HEADS * SWA_HEAD_DIM
LANES = 128
SUBLANES = 8
NEG_BIG = -1e30

C_QA, C_KA, C_VA = 0, NA_WIDTH, 2 * NA_WIDTH
C_QB = 3 * NA_WIDTH
C_KB = C_QB + SWA_Q_WIDTH
C_VB = C_KB + SWA_KV_WIDTH
C_GA = C_VB + SWA_KV_WIDTH
C_GB = C_GA + D_MODEL
D_IN = C_GB + D_MODEL

TM_PROJ = 512
TM_MIX = 512
TM_OUT = 512
FF_CHUNK = 256
VMEM_LIMIT = 56 * 1024 * 1024

_BF16 = jnp.bfloat16
_F32 = jnp.float32


def _const_spec(shape):
    nd = len(shape)
    return pl.BlockSpec(shape, lambda *_: (0,) * nd, pipeline_mode=pl.Buffered(1))


def _rope_slab(y, cos, sin_signed):
    lane = lax.broadcasted_iota(jnp.int32, y.shape, 1)
    first_half = (lane & (SWA_HEAD_DIM - 1)) < (SWA_HEAD_DIM // 2)
    rot = jnp.where(first_half, pltpu.roll(y, LANES - SWA_HEAD_DIM // 2, axis=1),
                    pltpu.roll(y, SWA_HEAD_DIM // 2, axis=1))
    return y * cos + rot * sin_signed


def _inproj_kernel(x_ref, g_ref, w_ref, b_ref, cos_ref, sin_ref,
                   qkva_ref, qb_ref, kvb_ref, gate_ref):
    x = x_ref[...]
    var = jnp.mean(x * x, axis=-1, keepdims=True)
    h = (x * lax.rsqrt(var + RMS_EPS) * g_ref[...]).astype(_BF16)

    def proj(c0, c1):
        return jnp.dot(h, w_ref[:, c0:c1], preferred_element_type=_F32) + b_ref[:, c0:c1]

    scale = NA_HEAD_DIM ** -0.5
    qkva_ref[:, C_QA:C_KA] = (proj(C_QA, C_KA) * scale).astype(_BF16)
    qkva_ref[:, C_KA:C_VA] = proj(C_KA, C_VA).astype(_BF16)
    qkva_ref[:, C_VA:C_QB] = proj(C_VA, C_QB).astype(_BF16)

    cos = cos_ref[...]
    sin = sin_ref[...]
    qb = proj(C_QB, C_KB)
    for s in range(SWA_Q_WIDTH // LANES):
        slab = _rope_slab(qb[:, s * LANES:(s + 1) * LANES], cos, sin)
        qb_ref[:, s * LANES:(s + 1) * LANES] = (slab * (SWA_HEAD_DIM ** -0.5)).astype(_BF16)
    kvb = proj(C_KB, C_GA)
    kvb_ref[:, 0:LANES] = _rope_slab(kvb[:, 0:LANES], cos, sin).astype(_BF16)
    kvb_ref[:, LANES:2 * LANES] = kvb[:, LANES:2 * LANES].astype(_BF16)

    for c0 in range(C_GA, D_IN, 512):
        gate_ref[:, c0 - C_GA:c0 - C_GA + 512] = jax.nn.sigmoid(proj(c0, c0 + 512)).astype(_BF16)


def _inproj(x2, g1, w_in, b_in, cos_t, sin_t, seq):
    t = x2.shape[0]
    tm = TM_PROJ
    nseq = seq // tm
    row = lambda i: (i, 0)
    return pl.pallas_call(
        _inproj_kernel,
        grid=(t // tm,),
        in_specs=[
            pl.BlockSpec((tm, D_MODEL), row),
            _const_spec((1, D_MODEL)),
            _const_spec((D_MODEL, D_IN)),
            _const_spec((1, D_IN)),
            pl.BlockSpec((tm, LANES), lambda i: (i % nseq, 0)),
            pl.BlockSpec((tm, LANES), lambda i: (i % nseq, 0)),
        ],
        out_specs=[
            pl.BlockSpec((tm, 3 * NA_WIDTH), row),
            pl.BlockSpec((tm, SWA_Q_WIDTH), row),
            pl.BlockSpec((tm, 2 * SWA_KV_WIDTH), row),
            pl.BlockSpec((tm, 2 * D_MODEL), row),
        ],
        out_shape=[
            jax.ShapeDtypeStruct((t, 3 * NA_WIDTH), _BF16),
            jax.ShapeDtypeStruct((t, SWA_Q_WIDTH), _BF16),
            jax.ShapeDtypeStruct((t, 2 * SWA_KV_WIDTH), _BF16),
            jax.ShapeDtypeStruct((t, 2 * D_MODEL), _BF16),
        ],
        compiler_params=pltpu.CompilerParams(
            dimension_semantics=("parallel",), vmem_limit_bytes=VMEM_LIMIT),
        name="inproj",
    )(x2, g1, w_in, b_in, cos_t, sin_t)


_NT_DIMS = (((1,), (1,)), ((), ()))


def _mixer_kernel(sink_ref, x_ref, q_ref, kp_ref, kc_ref, kn_ref, vp_ref, vc_ref, vn_ref,
                  qb_ref, kvp_ref, kvc_ref, kvn_ref, gate_ref, tbl_ref,
                  wua_ref, wub_ref, wo_ref, g2_ref, wr_ref, br_ref, tri_ref,
                  x1_ref, h2_ref, ids_ref, wts_ref, rank_ref, cnt_ref,
                  kcat, vcat, kvcat, oa_s, ob_s, *, rows, n_swa_blocks):
    j = pl.program_id(1)
    rows_per_tile = TM_MIX // GRID_W
    halo = (NA_ROWS // 2) * GRID_W
    band = NA_ROWS * GRID_W

    kcat[0:halo] = kp_ref[...]
    kcat[halo:halo + TM_MIX] = kc_ref[...]
    kcat[halo + TM_MIX:] = kn_ref[...]
    vcat[0:halo] = vp_ref[...]
    vcat[halo:halo + TM_MIX] = vc_ref[...]
    vcat[halo + TM_MIX:] = vn_ref[...]

    lane_q = lax.broadcasted_iota(jnp.int32, (GRID_W, LANES), 1)
    low_q = lane_q < NA_HEAD_DIM
    lane_o = lax.broadcasted_iota(jnp.int32, (GRID_W, LANES), 1) < NA_HEAD_DIM

    def na_row(i, carry):
        r = j * rows_per_tile + i
        rs = jnp.clip(r - NA_ROWS // 2, 0, rows - NA_ROWS)
        d = r - rs
        start = pl.multiple_of((rs - (j * rows_per_tile - NA_ROWS // 2)) * GRID_W, GRID_W)
        q0 = pl.multiple_of(i * GRID_W, GRID_W)
        for p in range(NA_HEADS // 2):
            cols = slice(p * LANES, (p + 1) * LANES)
            qpair = q_ref[pl.ds(q0, GRID_W), cols]
            zero = jnp.zeros_like(qpair)
            qs = jnp.concatenate([jnp.where(low_q, qpair, zero), jnp.where(low_q, zero, qpair)], axis=0)
            kb = kcat[pl.ds(start, band), cols]
            vb = vcat[pl.ds(start, band), cols]
            s = lax.dot_general(qs, kb, _NT_DIMS, preferred_element_type=_F32) + tbl_ref[d, p]
            m = jnp.max(s, axis=-1, keepdims=True)
            e = jnp.exp(s - m)
            l = jnp.sum(e, axis=-1, keepdims=True)
            o = jnp.dot(e.astype(_BF16), vb, preferred_element_type=_F32) * (1.0 / l)
            oa_s[pl.ds(q0, GRID_W), cols] = jnp.where(lane_o, o[:GRID_W], o[GRID_W:]).astype(_BF16)
        return carry

    lax.fori_loop(0, rows_per_tile, na_row, 0)

    kvcat[0:SWA_BLOCK] = kvp_ref[...]
    kvcat[SWA_BLOCK:SWA_BLOCK + TM_MIX] = kvc_ref[...]
    kvcat[SWA_BLOCK + TM_MIX:] = kvn_ref[...]

    n_slabs = SWA_Q_WIDTH // LANES
    stack = n_slabs * SWA_BLOCK
    lane_s = lax.broadcasted_iota(jnp.int32, (SWA_BLOCK, LANES), 1) < SWA_HEAD_DIM
    qpos = lax.broadcasted_iota(jnp.int32, (stack, 3 * SWA_BLOCK), 0) & (SWA_BLOCK - 1)
    kcol = lax.broadcasted_iota(jnp.int32, (stack, 3 * SWA_BLOCK), 1)
    rel_ok = jnp.abs(kcol - SWA_BLOCK - qpos) <= SWA_WINDOW
    rowblk = lax.broadcasted_iota(jnp.int32, (stack, 1), 0) // SWA_BLOCK

    def swa_block(n, carry):
        nb = j * (TM_MIX // SWA_BLOCK) + n
        t0 = pl.multiple_of(n * SWA_BLOCK, SWA_BLOCK)
        ok = rel_ok & ((kcol >= SWA_BLOCK) | (nb > 0)) & ((kcol < 2 * SWA_BLOCK) | (nb < n_swa_blocks - 1))
        kband = kvcat[pl.ds(t0, 3 * SWA_BLOCK), 0:LANES]
        vband = kvcat[pl.ds(t0, 3 * SWA_BLOCK), LANES:2 * LANES]
        outs = []
        for g in range(SWA_KV_HEADS):
            parts = []
            for s_ in range(n_slabs):
                slab = qb_ref[pl.ds(t0, SWA_BLOCK), s_ * LANES:(s_ + 1) * LANES]
                zero = jnp.zeros_like(slab)
                parts.append(jnp.where(lane_s, slab, zero) if g == 0 else jnp.where(lane_s, zero, slab))
            qg = jnp.concatenate(parts, axis=0)
            sink = jnp.zeros((stack, 1), _F32)
            for s_ in range(n_slabs):
                sink = jnp.where(rowblk == s_, sink_ref[SWA_KV_HEADS * s_ + g], sink)
            s = lax.dot_general(qg, kband, _NT_DIMS, preferred_element_type=_F32)
            s = jnp.where(ok, s, NEG_BIG)
            m = jnp.maximum(jnp.max(s, axis=-1, keepdims=True), sink)
            e = jnp.exp(s - m)
            den = jnp.sum(e, axis=-1, keepdims=True) + jnp.exp(sink - m)
            outs.append(jnp.dot(e.astype(_BF16), vband, preferred_element_type=_F32) * (1.0 / den))
        for s_ in range(n_slabs):
            rs_ = slice(s_ * SWA_BLOCK, (s_ + 1) * SWA_BLOCK)
            ob_s[pl.ds(t0, SWA_BLOCK), s_ * LANES:(s_ + 1) * LANES] = jnp.where(
                lane_s, outs[0][rs_], outs[1][rs_]).astype(_BF16)
        return carry

    lax.fori_loop(0, TM_MIX // SWA_BLOCK, swa_block, 0)

    ua = jnp.dot(oa_s[...], wua_ref[...], preferred_element_type=_F32)
    ub = jnp.dot(ob_s[...], wub_ref[...], preferred_element_type=_F32)
    merged = (gate_ref[:, 0:D_MODEL].astype(_F32) * ua
              + gate_ref[:, D_MODEL:].astype(_F32) * ub).astype(_BF16)
    x1 = x_ref[...] + jnp.dot(merged, wo_ref[...], preferred_element_type=_F32)
    x1_ref[...] = x1

    var = jnp.mean(x1 * x1, axis=-1, keepdims=True)
    h2 = x1 * lax.rsqrt(var + RMS_EPS) * g2_ref[...]
    h2_ref[...] = h2
    logits = lax.dot_general(wr_ref[...], h2, _NT_DIMS, preferred_element_type=_F32,
                             precision=lax.Precision.HIGHEST) + br_ref[...]
    eidx = lax.broadcasted_iota(jnp.int32, logits.shape, 0)
    vals, idxs, hots = [], [], []
    for _ in range(TOP_K):
        m = jnp.max(logits, axis=0, keepdims=True)
        idx = jnp.min(jnp.where(logits == m, eidx, N_EXPERTS), axis=0, keepdims=True)
        hot = eidx == idx
        logits = jnp.where(hot, -jnp.inf, logits)
        vals.append(m)
        idxs.append(idx)
        hots.append(hot)
    es = [jnp.exp(v - vals[0]) for v in vals]
    inv = 1.0 / (es[0] + es[1] + es[2] + es[3])
    sel = jnp.zeros(logits.shape, _F32)
    for hot in hots:
        sel = sel + jnp.where(hot, 1.0, 0.0)
    prefix = jnp.dot(sel.astype(_BF16), tri_ref[...], preferred_element_type=_F32)
    ranks = [jnp.sum(jnp.where(hot, prefix, 0.0), axis=0, keepdims=True).astype(jnp.int32) for hot in hots]
    pad_i = jnp.zeros((8 - TOP_K, TM_MIX), jnp.int32)
    ids_ref[...] = jnp.concatenate(idxs + [pad_i], axis=0)
    rank_ref[...] = jnp.concatenate(ranks + [pad_i], axis=0)
    wts_ref[...] = jnp.concatenate([e * inv for e in es] + [jnp.zeros((8 - TOP_K, TM_MIX), _F32)], axis=0)
    cnt_ref[...] = jnp.broadcast_to(jnp.sum(sel, axis=1, keepdims=True), (N_EXPERTS, LANES))


def _mixer(sinks_perm, x2, qkva, qb, kvb, gates, tbl, wua, wub, wo, g2, wr_t, br, tri, batch, seq):
    t = x2.shape[0]
    tm = TM_MIX
    nj = seq // tm
    rows = seq // GRID_W
    hb = tm // ((NA_ROWS // 2) * GRID_W)
    sb = tm // SWA_BLOCK
    n_halo = seq // ((NA_ROWS // 2) * GRID_W)
    n_swa = seq // SWA_BLOCK
    halo = (NA_ROWS // 2) * GRID_W

    tile = lambda b, j, *_: (b * nj + j, 0)

    def na_spec(col, which):
        if which == 0:
            return pl.BlockSpec((tm, NA_WIDTH), lambda b, j, *_: (b * nj + j, col))
        if which < 0:
            return pl.BlockSpec((halo, NA_WIDTH),
                                lambda b, j, *_: (b * n_halo + jnp.maximum(j * hb - 1, 0), col))
        return pl.BlockSpec((halo, NA_WIDTH),
                            lambda b, j, *_: (b * n_halo + jnp.minimum(j * hb + hb, n_halo - 1), col))

    kv_prev = pl.BlockSpec((SWA_BLOCK, 2 * SWA_KV_WIDTH),
                           lambda b, j, *_: (b * n_swa + jnp.maximum(j * sb - 1, 0), 0))
    kv_next = pl.BlockSpec((SWA_BLOCK, 2 * SWA_KV_WIDTH),
                           lambda b, j, *_: (b * n_swa + jnp.minimum(j * sb + sb, n_swa - 1), 0))

    def cspec(shape):
        nd = len(shape)
        return pl.BlockSpec(shape, lambda *_: (0,) * nd, pipeline_mode=pl.Buffered(1))

    grid_spec = pltpu.PrefetchScalarGridSpec(
        num_scalar_prefetch=1,
        grid=(batch, nj),
        in_specs=[
            pl.BlockSpec((tm, D_MODEL), tile),
            na_spec(0, 0),
            na_spec(1, -1), na_spec(1, 0), na_spec(1, 1),
            na_spec(2, -1), na_spec(2, 0), na_spec(2, 1),
            pl.BlockSpec((tm, SWA_Q_WIDTH), tile),
            kv_prev, pl.BlockSpec((tm, 2 * SWA_KV_WIDTH), tile), kv_next,
            pl.BlockSpec((tm, 2 * D_MODEL), tile),
            cspec(tbl.shape),
            cspec(wua.shape), cspec(wub.shape), cspec(wo.shape),
            cspec(g2.shape), cspec(wr_t.shape), cspec(br.shape), cspec(tri.shape),
        ],
        out_specs=[
            pl.BlockSpec((tm, D_MODEL), tile),
            pl.BlockSpec((tm, D_MODEL), tile),
            pl.BlockSpec((8, tm), lambda b, j, *_: (0, b * nj + j)),
            pl.BlockSpec((8, tm), lambda b, j, *_: (0, b * nj + j)),
            pl.BlockSpec((8, tm), lambda b, j, *_: (0, b * nj + j)),
            pl.BlockSpec((N_EXPERTS, LANES), tile),
        ],
        scratch_shapes=[
            pltpu.VMEM((tm + 2 * halo, NA_WIDTH), _BF16),
            pltpu.VMEM((tm + 2 * halo, NA_WIDTH), _BF16),
            pltpu.VMEM((tm + 2 * SWA_BLOCK, 2 * SWA_KV_WIDTH), _BF16),
            pltpu.VMEM((tm, NA_WIDTH), _BF16),
            pltpu.VMEM((tm, SWA_Q_WIDTH), _BF16),
        ],
    )
    return pl.pallas_call(
        functools.partial(_mixer_kernel, rows=rows, n_swa_blocks=n_swa),
        grid_spec=grid_spec,
        out_shape=[
            jax.ShapeDtypeStruct((t, D_MODEL), _F32),
            jax.ShapeDtypeStruct((t, D_MODEL), _F32),
            jax.ShapeDtypeStruct((8, t), jnp.int32),
            jax.ShapeDtypeStruct((8, t), _F32),
            jax.ShapeDtypeStruct((8, t), jnp.int32),
            jax.ShapeDtypeStruct((t // tm * N_EXPERTS, LANES), _F32),
        ],
        compiler_params=pltpu.CompilerParams(
            dimension_semantics=("parallel", "parallel"), vmem_limit_bytes=VMEM_LIMIT),
        name="mixer",
    )(sinks_perm, x2, qkva, qkva, qkva, qkva, qkva, qkva, qkva, qb, kvb, kvb, kvb, gates, tbl,
      wua, wub, wo, g2, wr_t, br, tri)


def _expert_kernel(bexp_ref, nact_ref, gidx_hbm, sidx_hbm, h2_hbm, w1_ref, b1_ref, w2_ref, b2_ref,
                   y_hbm, gs0, gs1, ss0, ss1, xbuf0, xbuf1, ybuf0, ybuf1, isem, gsem, ssem,
                   *, n_blocks, n_real_rows):
    i = pl.program_id(0)
    nact = nact_ref[0]
    eb = EXPERT_BLOCK
    gs, ss, xbuf, ybuf = (gs0, gs1), (ss0, ss1), (xbuf0, xbuf1), (ybuf0, ybuf1)
    rows = pl.ds(0, eb)

    def gidx_copy(blk, slot):
        return pltpu.make_async_copy(gidx_hbm.at[blk], gs[slot], isem.at[0, slot])

    def sidx_copy(entry, slot):
        return pltpu.make_async_copy(sidx_hbm.at[entry], ss[slot], isem.at[1, slot])

    def gather_row(slot, r):
        pltpu.make_async_copy(h2_hbm.at[pl.ds(gs[slot][r], 1), :], xbuf[slot].at[pl.ds(r, 1), :],
                              gsem.at[slot]).start()

    def wait_gather(slot):
        pltpu.make_async_copy(h2_hbm.at[rows, :], xbuf[slot].at[rows, :], gsem.at[slot]).wait()

    def scatter_row(slot, r):
        pltpu.make_async_copy(ybuf[slot].at[pl.ds(r, 1), :], y_hbm.at[pl.ds(ss[slot][r], 1), :],
                              ssem.at[slot]).start()

    def wait_scatter(slot):
        pltpu.make_async_copy(ybuf[slot].at[rows, :], y_hbm.at[rows, :], ssem.at[slot]).wait()

    def rolled(fn, slot):
        def body(r, c):
            fn(slot, r)
            return c
        lax.fori_loop(0, eb, body, 0)

    @pl.when(i == 0)
    def _():
        gidx_copy(0, 0).start()
        gidx_copy(0, 0).wait()
        rolled(gather_row, 0)
        gidx_copy(jnp.minimum(1, n_blocks - 1), 1).start()
        sidx_copy(0, 1).start()
        ybuf1[...] = jnp.zeros(ybuf1.shape, _F32)
        init = pltpu.make_async_copy(ybuf1.at[rows, :], y_hbm.at[pl.ds(n_real_rows, eb), :], ssem.at[0])
        init.start()
        init.wait()

    def step(cur):
        nxt = 1 - cur
        gidx_copy(jnp.minimum(i + 2, n_blocks - 1), cur).start()
        sidx_copy(i + 1, cur).start()
        gidx_copy(0, nxt).wait()
        sidx_copy(0, nxt).wait()
        wait_gather(cur)

        @pl.when(i >= 1)
        def _():
            wait_scatter(cur)

        x = xbuf[cur][rows, :].astype(_BF16)
        n_chunks = D_FF // FF_CHUNK
        rows_per_chunk = eb // n_chunks
        spare = pl.ds(pl.multiple_of(eb + jnp.minimum(i, 0) * SUBLANES, SUBLANES), SUBLANES)
        y = None
        for jc in range(n_chunks):
            for r in range(jc * rows_per_chunk, (jc + 1) * rows_per_chunk):
                gather_row(nxt, r)
                scatter_row(nxt, r)
            xbuf[nxt][spare, 0:FF_CHUNK] = jnp.zeros((SUBLANES, FF_CHUNK), _F32)
            ybuf[nxt][spare, 0:FF_CHUNK] = jnp.zeros((SUBLANES, FF_CHUNK), _F32)
            anchor = (xbuf[nxt][spare, 0:FF_CHUNK] + ybuf[nxt][spare, 0:FF_CHUNK])[0:1]
            c0 = jc * FF_CHUNK
            g = (jnp.dot(x, w1_ref[:, c0:c0 + FF_CHUNK], preferred_element_type=_F32)
                 + (b1_ref[:, c0:c0 + FF_CHUNK] + anchor))
            u = (jnp.dot(x, w1_ref[:, D_FF + c0:D_FF + c0 + FF_CHUNK], preferred_element_type=_F32)
                 + b1_ref[:, D_FF + c0:D_FF + c0 + FF_CHUNK])
            gate = jnp.minimum(g, SWIGLU_LIMIT)
            up = jnp.clip(u, -SWIGLU_LIMIT, SWIGLU_LIMIT)
            act = ((up + 1.0) * (gate * jax.nn.sigmoid(gate * SWIGLU_ALPHA))).astype(_BF16)
            part = jnp.dot(act, w2_ref[c0:c0 + FF_CHUNK, :], preferred_element_type=_F32)
            y = part + b2_ref[...] if y is None else y + part
        ybuf[cur][rows, :] = y

        @pl.when(i == nact - 1)
        def _():
            sidx_copy(0, cur).wait()
            rolled(scatter_row, cur)
            wait_scatter(nxt)
            wait_scatter(cur)
            wait_gather(nxt)
            gidx_copy(0, cur).wait()

    for parity in range(2):
        @pl.when(jnp.logical_and(i < nact, i % 2 == parity))
        def _():
            step(parity)


def _experts(blk_exp, n_active, gidx, sidx, h2, w1, b1, w2, b2, n_tokens):
    n_blk = gidx.shape[0]
    eb = EXPERT_BLOCK
    n_real = TOP_K * n_tokens
    wmap = lambda i, be, na: (be[i], 0, 0)
    grid_spec = pltpu.PrefetchScalarGridSpec(
        num_scalar_prefetch=2,
        grid=(n_blk,),
        in_specs=[
            pl.BlockSpec(memory_space=pl.ANY),
            pl.BlockSpec(memory_space=pl.ANY),
            pl.BlockSpec(memory_space=pl.ANY),
            pl.BlockSpec((None, D_MODEL, 2 * D_FF), wmap),
            pl.BlockSpec((None, 1, 2 * D_FF), wmap),
            pl.BlockSpec((None, D_FF, D_MODEL), wmap),
            pl.BlockSpec((None, 1, D_MODEL), wmap),
        ],
        out_specs=pl.BlockSpec(memory_space=pl.ANY),
        scratch_shapes=[
            pltpu.SMEM((eb,), jnp.int32), pltpu.SMEM((eb,), jnp.int32),
            pltpu.SMEM((eb,), jnp.int32), pltpu.SMEM((eb,), jnp.int32),
            pltpu.VMEM((eb + SUBLANES, D_MODEL), _F32),
            pltpu.VMEM((eb + SUBLANES, D_MODEL), _F32),
            pltpu.VMEM((eb + SUBLANES, D_MODEL), _F32),
            pltpu.VMEM((eb + SUBLANES, D_MODEL), _F32),
            pltpu.SemaphoreType.DMA((2, 2)),
            pltpu.SemaphoreType.DMA((2,)),
            pltpu.SemaphoreType.DMA((2,)),
        ],
    )
    return pl.pallas_call(
        functools.partial(_expert_kernel, n_blocks=n_blk, n_real_rows=n_real),
        grid_spec=grid_spec,
        out_shape=jax.ShapeDtypeStruct((n_real + 2 * eb, D_MODEL), _F32),
        compiler_params=pltpu.CompilerParams(
            dimension_semantics=("arbitrary",), vmem_limit_bytes=VMEM_LIMIT),
        name="experts",
    )(blk_exp, n_active, gidx, sidx, h2, w1, b1, w2, b2)


def _combine_kernel(x1_ref, y0_ref, y1_ref, y2_ref, y3_ref, wts_ref, g_ref, o_ref):
    wt = wts_ref[...].T
    y = x1_ref[...]
    for k, y_ref in enumerate((y0_ref, y1_ref, y2_ref, y3_ref)):
        y = y + wt[:, k:k + 1] * y_ref[...]
    var = jnp.mean(y * y, axis=-1, keepdims=True)
    o_ref[...] = y * lax.rsqrt(var + RMS_EPS) * g_ref[...]


def _combine(x1, y, wts, gf):
    t = x1.shape[0]
    tm = TM_OUT
    nt = t // tm
    yspec = lambda k: pl.BlockSpec((tm, D_MODEL), lambda i: (k * nt + i, 0))
    return pl.pallas_call(
        _combine_kernel,
        grid=(nt,),
        in_specs=[
            pl.BlockSpec((tm, D_MODEL), lambda i: (i, 0)),
            yspec(0), yspec(1), yspec(2), yspec(3),
            pl.BlockSpec((8, tm), lambda i: (0, i)),
            _const_spec((1, D_MODEL)),
        ],
        out_specs=pl.BlockSpec((tm, D_MODEL), lambda i: (i, 0)),
        out_shape=jax.ShapeDtypeStruct((t, D_MODEL), _F32),
        compiler_params=pltpu.CompilerParams(
            dimension_semantics=("parallel",), vmem_limit_bytes=VMEM_LIMIT),
        name="combine",
    )(x1, y, y, y, y, wts, gf)


def _rope_tables(seq):
    half = SWA_HEAD_DIM // 2
    inv_freq = ROPE_THETA ** (-jnp.arange(half, dtype=_F32) / half)
    ang = jnp.arange(seq, dtype=_F32)[:, None] * inv_freq[None, :]
    cos, sin = jnp.cos(ang), jnp.sin(ang)
    cos_t = jnp.tile(jnp.concatenate([cos, cos], axis=1), (1, LANES // SWA_HEAD_DIM))
    sin_t = jnp.tile(jnp.concatenate([-sin, sin], axis=1), (1, LANES // SWA_HEAD_DIM))
    return cos_t, sin_t


def _na_bias_table(rpb):
    col = np.arange(GRID_W)
    cstart = np.clip(col - NA_COLS // 2, 0, GRID_W - NA_COLS)
    kc = np.arange(GRID_W)
    valid = (kc[None, :] >= cstart[:, None]) & (kc[None, :] < cstart[:, None] + NA_COLS)
    off = np.clip(kc[None, :] - col[:, None] + NA_COLS - 1, 0, 2 * NA_COLS - 2)
    ext = jnp.where(valid[None, None], rpb[:, :, off], NEG_BIG)
    u = np.arange(NA_ROWS)[None, :] - np.arange(NA_ROWS)[:, None] + NA_ROWS - 1
    tbl = ext[:, u]
    tbl = tbl.transpose(1, 0, 3, 2, 4).reshape(NA_ROWS, NA_HEADS // 2, 2 * GRID_W, NA_ROWS * GRID_W)
    return tbl.astype(_F32)


def kernel(x, norm1_g, w_in, b_in, na_rpb, swa_sinks, w_up_a, w_up_b, w_out, norm2_g, w_router,
           b_router, w1, b1, w2, b2, final_g):
    batch, seq, d = x.shape
    depth = w_in.shape[0]
    t = batch * seq
    assert depth == 1, "the final norm is fused into the single layer's combine step"
    assert d == D_MODEL and seq % TM_MIX == 0 and seq // GRID_W >= 2 * NA_ROWS and t % TM_OUT == 0

    group = SWA_Q_HEADS // SWA_KV_HEADS
    head_order = np.arange(SWA_Q_HEADS).reshape(SWA_KV_HEADS, group).T.reshape(-1)
    qb_cols = (head_order[:, None] * SWA_HEAD_DIM + np.arange(SWA_HEAD_DIM)[None, :]).reshape(-1)
    col_perm = np.arange(D_IN)
    col_perm[C_QB:C_KB] = C_QB + qb_cols
    cos_t, sin_t = _rope_tables(seq)
    tri = (np.arange(TM_MIX)[:, None] < np.arange(TM_MIX)[None, :]).astype(np.float32)
    tri = jnp.asarray(tri, _BF16)

    n_assign = t * TOP_K
    n_rows = n_assign + N_EXPERTS * EXPERT_BLOCK
    n_blk = n_rows // EXPERT_BLOCK
    n_tiles = t // TM_MIX

    x2 = x.reshape(t, d)
    for l in range(depth):
        w_in_l = w_in[l][:, col_perm].astype(_BF16)
        b_in_l = b_in[l][col_perm].reshape(1, D_IN)
        qkva, qb, kvb, gates = _inproj(x2, norm1_g[l].reshape(1, d), w_in_l, b_in_l, cos_t, sin_t, seq)

        x1, h2, ids, wts, rank, cnt = _mixer(
            swa_sinks[l][head_order].astype(_F32), x2, qkva, qb, kvb, gates, _na_bias_table(na_rpb[l]),
            w_up_a[l].astype(_BF16), w_up_b[l][qb_cols].astype(_BF16), w_out[l].astype(_BF16),
            norm2_g[l].reshape(1, d), w_router[l].T, b_router[l].reshape(N_EXPERTS, 1), tri, batch, seq)

        cnt = cnt.reshape(n_tiles, N_EXPERTS, LANES)[:, :, 0].astype(jnp.int32)
        counts = jnp.sum(cnt, axis=0)
        padded = (counts + EXPERT_BLOCK - 1) // EXPERT_BLOCK * EXPERT_BLOCK
        pend = jnp.cumsum(padded)
        base = (pend - padded)[None, :] + jnp.cumsum(cnt, axis=0) - cnt
        tile_of = jnp.arange(t, dtype=jnp.int32) // TM_MIX
        dest = base.reshape(-1)[tile_of[None, :] * N_EXPERTS + ids[:TOP_K]] + rank[:TOP_K]
        assign = jnp.arange(t, dtype=jnp.int32)[None, :] * TOP_K + jnp.arange(TOP_K, dtype=jnp.int32)[:, None]
        row_a = jnp.full((n_rows,), -1, jnp.int32).at[dest.reshape(-1)].set(assign.reshape(-1))
        row_a = row_a.reshape(n_blk, EXPERT_BLOCK)
        blk_exp = jnp.minimum(
            jnp.searchsorted(pend, jnp.arange(n_blk, dtype=jnp.int32) * EXPERT_BLOCK, side="right"),
            N_EXPERTS - 1).astype(jnp.int32)
        n_active = (pend[-1:] // EXPERT_BLOCK).astype(jnp.int32)
        r_in_blk = jnp.arange(EXPERT_BLOCK, dtype=jnp.int32)[None, :]
        parity = (jnp.arange(-1, n_blk, dtype=jnp.int32) % 2)[:, None]
        trash = n_assign + parity * EXPERT_BLOCK + r_in_blk
        gidx = jnp.where(row_a >= 0, row_a >> 2, 0)
        sidx = jnp.where(row_a >= 0, (row_a & (TOP_K - 1)) * t + (row_a >> 2), trash[1:])
        sidx = jnp.concatenate([trash[:1], sidx], axis=0)

        y = _experts(blk_exp, n_active, gidx, sidx, h2,
                     w1[l].astype(_BF16), b1[l].reshape(N_EXPERTS, 1, 2 * D_FF),
                     w2[l].astype(_BF16), b2[l].reshape(N_EXPERTS, 1, D_MODEL), t)
        x2 = _combine(x1, y, wts, final_g.reshape(1, d))
    return x2.reshape(batch, seq, d)
```

```python
import functools

import jax
import jax.numpy as jnp
import numpy as np
from jax import lax
from jax.experimental import pallas as pl
from jax.experimental.pallas import tpu as pltpu

D_MODEL = 1024
GRID_W = 64
NA_HEADS = 8
NA_HEAD_DIM = 64
NA_ROWS = 8
NA_COLS = 16
SWA_Q_HEADS = 8
SWA_KV_HEADS = 2
SWA_HEAD_DIM = 64
SWA_WINDOW = 128
SWA_BLOCK = 128
ROPE_THETA = 10000.0
N_EXPERTS = 32
TOP_K = 4
D_FF = 1024
SWIGLU_LIMIT = 7.0
SWIGLU_ALPHA = 1.702
EXPERT_BLOCK = 256
RMS_EPS = 1e-5

NA_WIDTH = NA_HEADS * NA_HEAD_DIM
SWA_Q_WIDTH = SWA_Q_HEADS * SWA_HEAD_DIM
SWA_KV_WIDTH = SWA_KV_HEADS * SWA_HEAD_DIM
LANES = 128
SUBLANES = 8
NEG_BIG = -1e30

C_QA, C_KA, C_VA = 0, NA_WIDTH, 2 * NA_WIDTH
C_QB = 3 * NA_WIDTH
C_KB = C_QB + SWA_Q_WIDTH
C_VB = C_KB + SWA_KV_WIDTH
C_GA = C_VB + SWA_KV_WIDTH
C_GB = C_GA + D_MODEL
D_IN = C_GB + D_MODEL

TM_PROJ = 512
TM_MIX = 512
TM_OUT = 512
FF_CHUNK = 256
VMEM_LIMIT = 56 * 1024 * 1024

_BF16 = jnp.bfloat16
_F32 = jnp.float32


def _const_spec(shape):
    nd = len(shape)
    return pl.BlockSpec(shape, lambda *_: (0,) * nd, pipeline_mode=pl.Buffered(1))


def _rope_slab(y, cos, sin_signed):
    lane = lax.broadcasted_iota(jnp.int32, y.shape, 1)
    first_half = (lane & (SWA_HEAD_DIM - 1)) < (SWA_HEAD_DIM // 2)
    rot = jnp.where(first_half, pltpu.roll(y, LANES - SWA_HEAD_DIM // 2, axis=1),
                    pltpu.roll(y, SWA_HEAD_DIM // 2, axis=1))
    return y * cos + rot * sin_signed


def _inproj_kernel(x_ref, g_ref, w_ref, b_ref, cos_ref, sin_ref,
                   qkva_ref, qb_ref, kvb_ref, gate_ref):
    x = x_ref[...]
    var = jnp.mean(x * x, axis=-1, keepdims=True)
    h = (x * lax.rsqrt(var + RMS_EPS) * g_ref[...]).astype(_BF16)

    def proj(c0, c1):
        return jnp.dot(h, w_ref[:, c0:c1], preferred_element_type=_F32) + b_ref[:, c0:c1]

    scale = NA_HEAD_DIM ** -0.5
    qkva_ref[:, C_QA:C_KA] = (proj(C_QA, C_KA) * scale).astype(_BF16)
    qkva_ref[:, C_KA:C_VA] = proj(C_KA, C_VA).astype(_BF16)
    qkva_ref[:, C_VA:C_QB] = proj(C_VA, C_QB).astype(_BF16)

    cos = cos_ref[...]
    sin = sin_ref[...]
    qb = proj(C_QB, C_KB)
    for s in range(SWA_Q_WIDTH // LANES):
        slab = _rope_slab(qb[:, s * LANES:(s + 1) * LANES], cos, sin)
        qb_ref[:, s * LANES:(s + 1) * LANES] = (slab * (SWA_HEAD_DIM ** -0.5)).astype(_BF16)
    kvb = proj(C_KB, C_GA)
    kvb_ref[:, 0:LANES] = _rope_slab(kvb[:, 0:LANES], cos, sin).astype(_BF16)
    kvb_ref[:, LANES:2 * LANES] = kvb[:, LANES:2 * LANES].astype(_BF16)

    for c0 in range(C_GA, D_IN, 512):
        gate_ref[:, c0 - C_GA:c0 - C_GA + 512] = jax.nn.sigmoid(proj(c0, c0 + 512)).astype(_BF16)


def _inproj(x2, g1, w_in, b_in, cos_t, sin_t, seq):
    t = x2.shape[0]
    tm = TM_PROJ
    nseq = seq // tm
    row = lambda i: (i, 0)
    return pl.pallas_call(
        _inproj_kernel,
        grid=(t // tm,),
        in_specs=[
            pl.BlockSpec((tm, D_MODEL), row),
            _const_spec((1, D_MODEL)),
            _const_spec((D_MODEL, D_IN)),
            _const_spec((1, D_IN)),
            pl.BlockSpec((tm, LANES), lambda i: (i % nseq, 0)),
            pl.BlockSpec((tm, LANES), lambda i: (i % nseq, 0)),
        ],
        out_specs=[
            pl.BlockSpec((tm, 3 * NA_WIDTH), row),
            pl.BlockSpec((tm, SWA_Q_WIDTH), row),
            pl.BlockSpec((tm, 2 * SWA_KV_WIDTH), row),
            pl.BlockSpec((tm, 2 * D_MODEL), row),
        ],
        out_shape=[
            jax.ShapeDtypeStruct((t, 3 * NA_WIDTH), _BF16),
            jax.ShapeDtypeStruct((t, SWA_Q_WIDTH), _BF16),
            jax.ShapeDtypeStruct((t, 2 * SWA_KV_WIDTH), _BF16),
            jax.ShapeDtypeStruct((t, 2 * D_MODEL), _BF16),
        ],
        compiler_params=pltpu.CompilerParams(
            dimension_semantics=("parallel",), vmem_limit_bytes=VMEM_LIMIT),
        name="inproj",
    )(x2, g1, w_in, b_in, cos_t, sin_t)


_NT_DIMS = (((1,), (1,)), ((), ()))


def _mixer_kernel(sink_ref, x_ref, q_ref, kp_ref, kc_ref, kn_ref, vp_ref, vc_ref, vn_ref,
                  qb_ref, kvp_ref, kvc_ref, kvn_ref, gate_ref, tbl_ref,
                  wua_ref, wub_ref, wo_ref, g2_ref, wr_ref, br_ref, tri_ref,
                  x1_ref, h2_ref, ids_ref, wts_ref, rank_ref, cnt_ref,
                  kcat, vcat, kvcat, oa_s, ob_s, *, rows, n_swa_blocks):
    j = pl.program_id(1)
    rows_per_tile = TM_MIX // GRID_W
    halo = (NA_ROWS // 2) * GRID_W
    band = NA_ROWS * GRID_W

    kcat[0:halo] = kp_ref[...]
    kcat[halo:halo + TM_MIX] = kc_ref[...]
    kcat[halo + TM_MIX:] = kn_ref[...]
    vcat[0:halo] = vp_ref[...]
    vcat[halo:halo + TM_MIX] = vc_ref[...]
    vcat[halo + TM_MIX:] = vn_ref[...]

    lane_q = lax.broadcasted_iota(jnp.int32, (GRID_W, LANES), 1)
    low_q = lane_q < NA_HEAD_DIM
    lane_o = lax.broadcasted_iota(jnp.int32, (GRID_W, LANES), 1) < NA_HEAD_DIM

    def na_row(i, carry):
        r = j * rows_per_tile + i
        rs = jnp.clip(r - NA_ROWS // 2, 0, rows - NA_ROWS)
        d = r - rs
        start = pl.multiple_of((rs - (j * rows_per_tile - NA_ROWS // 2)) * GRID_W, GRID_W)
        q0 = pl.multiple_of(i * GRID_W, GRID_W)
        for p in range(NA_HEADS // 2):
            cols = slice(p * LANES, (p + 1) * LANES)
            qpair = q_ref[pl.ds(q0, GRID_W), cols]
            zero = jnp.zeros_like(qpair)
            qs = jnp.concatenate([jnp.where(low_q, qpair, zero), jnp.where(low_q, zero, qpair)], axis=0)
            kb = kcat[pl.ds(start, band), cols]
            vb = vcat[pl.ds(start, band), cols]
            s = lax.dot_general(qs, kb, _NT_DIMS, preferred_element_type=_F32) + tbl_ref[d, p]
            m = jnp.max(s, axis=-1, keepdims=True)
            e = jnp.exp(s - m)
            l = jnp.sum(e, axis=-1, keepdims=True)
            o = jnp.dot(e.astype(_BF16), vb, preferred_element_type=_F32) * (1.0 / l)
            oa_s[pl.ds(q0, GRID_W), cols] = jnp.where(lane_o, o[:GRID_W], o[GRID_W:]).astype(_BF16)
        return carry

    lax.fori_loop(0, rows_per_tile, na_row, 0)

    kvcat[0:SWA_BLOCK] = kvp_ref[...]
    kvcat[SWA_BLOCK:SWA_BLOCK + TM_MIX] = kvc_ref[...]
    kvcat[SWA_BLOCK + TM_MIX:] = kvn_ref[...]

    n_slabs = SWA_Q_WIDTH // LANES
    stack = n_slabs * SWA_BLOCK
    lane_s = lax.broadcasted_iota(jnp.int32, (SWA_BLOCK, LANES), 1) < SWA_HEAD_DIM
    qpos = lax.broadcasted_iota(jnp.int32, (stack, 3 * SWA_BLOCK), 0) & (SWA_BLOCK - 1)
    kcol = lax.broadcasted_iota(jnp.int32, (stack, 3 * SWA_BLOCK), 1)
    rel_ok = jnp.abs(kcol - SWA_BLOCK - qpos) <= SWA_WINDOW
    rowblk = lax.broadcasted_iota(jnp.int32, (stack, 1), 0) // SWA_BLOCK

    def swa_block(n, carry):
        nb = j * (TM_MIX // SWA_BLOCK) + n
        t0 = pl.multiple_of(n * SWA_BLOCK, SWA_BLOCK)
        ok = rel_ok & ((kcol >= SWA_BLOCK) | (nb > 0)) & ((kcol < 2 * SWA_BLOCK) | (nb < n_swa_blocks - 1))
        kband = kvcat[pl.ds(t0, 3 * SWA_BLOCK), 0:LANES]
        vband = kvcat[pl.ds(t0, 3 * SWA_BLOCK), LANES:2 * LANES]
        outs = []
        for g in range(SWA_KV_HEADS):
            parts = []
            for s_ in range(n_slabs):
                slab = qb_ref[pl.ds(t0, SWA_BLOCK), s_ * LANES:(s_ + 1) * LANES]
                zero = jnp.zeros_like(slab)
                parts.append(jnp.where(lane_s, slab, zero) if g == 0 else jnp.where(lane_s, zero, slab))
            qg = jnp.concatenate(parts, axis=0)
            sink = jnp.zeros((stack, 1), _F32)
            for s_ in range(n_slabs):
                sink = jnp.where(rowblk == s_, sink_ref[SWA_KV_HEADS * s_ + g], sink)
            s = lax.dot_general(qg, kband, _NT_DIMS, preferred_element_type=_F32)
            s = jnp.where(ok, s, NEG_BIG)
            m = jnp.maximum(jnp.max(s, axis=-1, keepdims=True), sink)
            e = jnp.exp(s - m)
            den = jnp.sum(e, axis=-1, keepdims=True) + jnp.exp(sink - m)
            outs.append(jnp.dot(e.astype(_BF16), vband, preferred_element_type=_F32) * (1.0 / den))
        for s_ in range(n_slabs):
            rs_ = slice(s_ * SWA_BLOCK, (s_ + 1) * SWA_BLOCK)
            ob_s[pl.ds(t0, SWA_BLOCK), s_ * LANES:(s_ + 1) * LANES] = jnp.where(
                lane_s, outs[0][rs_], outs[1][rs_]).astype(_BF16)
        return carry

    lax.fori_loop(0, TM_MIX // SWA_BLOCK, swa_block, 0)

    ua = jnp.dot(oa_s[...], wua_ref[...], preferred_element_type=_F32)
    ub = jnp.dot(ob_s[...], wub_ref[...], preferred_element_type=_F32)
    merged = (gate_ref[:, 0:D_MODEL].astype(_F32) * ua
              + gate_ref[:, D_MODEL:].astype(_F32) * ub).astype(_BF16)
    x1 = x_ref[...] + jnp.dot(merged, wo_ref[...], preferred_element_type=_F32)
    x1_ref[...] = x1

    var = jnp.mean(x1 * x1, axis=-1, keepdims=True)
    h2 = x1 * lax.rsqrt(var + RMS_EPS) * g2_ref[...]
    h2_ref[...] = h2
    logits = lax.dot_general(wr_ref[...], h2, _NT_DIMS, preferred_element_type=_F32,
                             precision=lax.Precision.HIGHEST) + br_ref[...]
    eidx = lax.broadcasted_iota(jnp.int32, logits.shape, 0)
    vals, idxs, hots = [], [], []
    for _ in range(TOP_K):
        m = jnp.max(logits, axis=0, keepdims=True)
        idx = jnp.min(jnp.where(logits == m, eidx, N_EXPERTS), axis=0, keepdims=True)
        hot = eidx == idx
        logits = jnp.where(hot, -jnp.inf, logits)
        vals.append(m)
        idxs.append(idx)
        hots.append(hot)
    es = [jnp.exp(v - vals[0]) for v in vals]
    inv = 1.0 / (es[0] + es[1] + es[2] + es[3])
    sel = jnp.zeros(logits.shape, _F32)
    for hot in hots:
        sel = sel + jnp.where(hot, 1.0, 0.0)
    prefix = jnp.dot(sel.astype(_BF16), tri_ref[...], preferred_element_type=_F32)
    ranks = [jnp.sum(jnp.where(hot, prefix, 0.0), axis=0, keepdims=True).astype(jnp.int32) for hot in hots]
    pad_i = jnp.zeros((8 - TOP_K, TM_MIX), jnp.int32)
    ids_ref[...] = jnp.concatenate(idxs + [pad_i], axis=0)
    rank_ref[...] = jnp.concatenate(ranks + [pad_i], axis=0)
    wts_ref[...] = jnp.concatenate([e * inv for e in es] + [jnp.zeros((8 - TOP_K, TM_MIX), _F32)], axis=0)
    cnt_ref[...] = jnp.broadcast_to(jnp.sum(sel, axis=1, keepdims=True), (N_EXPERTS, LANES))


def _mixer(sinks_perm, x2, qkva, qb, kvb, gates, tbl, wua, wub, wo, g2, wr_t, br, tri, batch, seq):
    t = x2.shape[0]
    tm = TM_MIX
    nj = seq // tm
    rows = seq // GRID_W
    hb = tm // ((NA_ROWS // 2) * GRID_W)
    sb = tm // SWA_BLOCK
    n_halo = seq // ((NA_ROWS // 2) * GRID_W)
    n_swa = seq // SWA_BLOCK
    halo = (NA_ROWS // 2) * GRID_W

    tile = lambda b, j, *_: (b * nj + j, 0)

    def na_spec(col, which):
        if which == 0:
            return pl.BlockSpec((tm, NA_WIDTH), lambda b, j, *_: (b * nj + j, col))
        if which < 0:
            return pl.BlockSpec((halo, NA_WIDTH),
                                lambda b, j, *_: (b * n_halo + jnp.maximum(j * hb - 1, 0), col))
        return pl.BlockSpec((halo, NA_WIDTH),
                            lambda b, j, *_: (b * n_halo + jnp.minimum(j * hb + hb, n_halo - 1), col))

    kv_prev = pl.BlockSpec((SWA_BLOCK, 2 * SWA_KV_WIDTH),
                           lambda b, j, *_: (b * n_swa + jnp.maximum(j * sb - 1, 0), 0))
    kv_next = pl.BlockSpec((SWA_BLOCK, 2 * SWA_KV_WIDTH),
                           lambda b, j, *_: (b * n_swa + jnp.minimum(j * sb + sb, n_swa - 1), 0))

    def cspec(shape):
        nd = len(shape)
        return pl.BlockSpec(shape, lambda *_: (0,) * nd, pipeline_mode=pl.Buffered(1))

    grid_spec = pltpu.PrefetchScalarGridSpec(
        num_scalar_prefetch=1,
        grid=(batch, nj),
        in_specs=[
            pl.BlockSpec((tm, D_MODEL), tile),
            na_spec(0, 0),
            na_spec(1, -1), na_spec(1, 0), na_spec(1, 1),
            na_spec(2, -1), na_spec(2, 0), na_spec(2, 1),
            pl.BlockSpec((tm, SWA_Q_WIDTH), tile),
            kv_prev, pl.BlockSpec((tm, 2 * SWA_KV_WIDTH), tile), kv_next,
            pl.BlockSpec((tm, 2 * D_MODEL), tile),
            cspec(tbl.shape),
            cspec(wua.shape), cspec(wub.shape), cspec(wo.shape),
            cspec(g2.shape), cspec(wr_t.shape), cspec(br.shape), cspec(tri.shape),
        ],
        out_specs=[
            pl.BlockSpec((tm, D_MODEL), tile),
            pl.BlockSpec((tm, D_MODEL), tile),
            pl.BlockSpec((8, tm), lambda b, j, *_: (0, b * nj + j)),
            pl.BlockSpec((8, tm), lambda b, j, *_: (0, b * nj + j)),
            pl.BlockSpec((8, tm), lambda b, j, *_: (0, b * nj + j)),
            pl.BlockSpec((N_EXPERTS, LANES), tile),
        ],
        scratch_shapes=[
            pltpu.VMEM((tm + 2 * halo, NA_WIDTH), _BF16),
            pltpu.VMEM((tm + 2 * halo, NA_WIDTH), _BF16),
            pltpu.VMEM((tm + 2 * SWA_BLOCK, 2 * SWA_KV_WIDTH), _BF16),
            pltpu.VMEM((tm, NA_WIDTH), _BF16),
            pltpu.VMEM((tm, SWA_Q_WIDTH), _BF16),
        ],
    )
    return pl.pallas_call(
        functools.partial(_mixer_kernel, rows=rows, n_swa_blocks=n_swa),
        grid_spec=grid_spec,
        out_shape=[
            jax.ShapeDtypeStruct((t, D_MODEL), _F32),
            jax.ShapeDtypeStruct((t, D_MODEL), _F32),
            jax.ShapeDtypeStruct((8, t), jnp.int32),
            jax.ShapeDtypeStruct((8, t), _F32),
            jax.ShapeDtypeStruct((8, t), jnp.int32),
            jax.ShapeDtypeStruct((t // tm * N_EXPERTS, LANES), _F32),
        ],
        compiler_params=pltpu.CompilerParams(
            dimension_semantics=("parallel", "parallel"), vmem_limit_bytes=VMEM_LIMIT),
        name="mixer",
    )(sinks_perm, x2, qkva, qkva, qkva, qkva, qkva, qkva, qkva, qb, kvb, kvb, kvb, gates, tbl,
      wua, wub, wo, g2, wr_t, br, tri)


def _expert_kernel(bexp_ref, nact_ref, gidx_hbm, sidx_hbm, h2_hbm, w1_ref, b1_ref, w2_ref, b2_ref,
                   y_hbm, gs0, gs1, ss0, ss1, xbuf0, xbuf1, ybuf0, ybuf1, isem, gsem, ssem,
                   *, n_blocks, n_real_rows):
    i = pl.program_id(0)
    nact = nact_ref[0]
    eb = EXPERT_BLOCK
    gs, ss, xbuf, ybuf = (gs0, gs1), (ss0, ss1), (xbuf0, xbuf1), (ybuf0, ybuf1)
    rows = pl.ds(0, eb)

    def gidx_copy(blk, slot):
        return pltpu.make_async_copy(gidx_hbm.at[blk], gs[slot], isem.at[0, slot])

    def sidx_copy(entry, slot):
        return pltpu.make_async_copy(sidx_hbm.at[entry], ss[slot], isem.at[1, slot])

    def gather_row(slot, r, priority=0):
        pltpu.make_async_copy(h2_hbm.at[pl.ds(gs[slot][r], 1), :], xbuf[slot].at[pl.ds(r, 1), :],
                              gsem.at[slot]).start(priority=priority)

    def wait_gather(slot):
        pltpu.make_async_copy(h2_hbm.at[rows, :], xbuf[slot].at[rows, :], gsem.at[slot]).wait()

    def scatter_row(slot, r, priority=0):
        pltpu.make_async_copy(ybuf[slot].at[pl.ds(r, 1), :], y_hbm.at[pl.ds(ss[slot][r], 1), :],
                              ssem.at[slot]).start(priority=priority)

    def wait_scatter(slot):
        pltpu.make_async_copy(ybuf[slot].at[rows, :], y_hbm.at[rows, :], ssem.at[slot]).wait()

    def rolled(fn, slot):
        def body(r, c):
            fn(slot, r)
            return c
        lax.fori_loop(0, eb, body, 0)

    @pl.when(i == 0)
    def _():
        gidx_copy(0, 0).start()
        gidx_copy(0, 0).wait()
        rolled(gather_row, 0)
        gidx_copy(jnp.minimum(1, n_blocks - 1), 1).start()
        sidx_copy(0, 1).start()
        ybuf1[...] = jnp.zeros(ybuf1.shape, _F32)
        init = pltpu.make_async_copy(ybuf1.at[rows, :], y_hbm.at[pl.ds(n_real_rows, eb), :], ssem.at[0])
        init.start()
        init.wait()

    def step(cur):
        nxt = 1 - cur
        gidx_copy(jnp.minimum(i + 2, n_blocks - 1), cur).start()
        sidx_copy(i + 1, cur).start()
        gidx_copy(0, nxt).wait()
        sidx_copy(0, nxt).wait()
        wait_gather(cur)

        @pl.when(i >= 1)
        def _():
            wait_scatter(cur)

        x = xbuf[cur][rows, :].astype(_BF16)
        n_chunks = D_FF // FF_CHUNK
        rows_per_chunk = eb // n_chunks
        spare = pl.ds(pl.multiple_of(eb + jnp.minimum(i, 0) * SUBLANES, SUBLANES), SUBLANES)
        y = None
        for jc in range(n_chunks):
            for r in range(jc * rows_per_chunk, (jc + 1) * rows_per_chunk):
                gather_row(nxt, r, r % 2)
                scatter_row(nxt, r, r % 2)
            xbuf[nxt][spare, 0:FF_CHUNK] = jnp.zeros((SUBLANES, FF_CHUNK), _F32)
            ybuf[nxt][spare, 0:FF_CHUNK] = jnp.zeros((SUBLANES, FF_CHUNK), _F32)
            anchor = (xbuf[nxt][spare, 0:FF_CHUNK] + ybuf[nxt][spare, 0:FF_CHUNK])[0:1]
            c0 = jc * FF_CHUNK
            g = (jnp.dot(x, w1_ref[:, c0:c0 + FF_CHUNK], preferred_element_type=_F32)
                 + (b1_ref[:, c0:c0 + FF_CHUNK] + anchor))
            u = (jnp.dot(x, w1_ref[:, D_FF + c0:D_FF + c0 + FF_CHUNK], preferred_element_type=_F32)
                 + b1_ref[:, D_FF + c0:D_FF + c0 + FF_CHUNK])
            gate = jnp.minimum(g, SWIGLU_LIMIT)
            up = jnp.clip(u, -SWIGLU_LIMIT, SWIGLU_LIMIT)
            act = ((up + 1.0) * (gate * jax.nn.sigmoid(gate * SWIGLU_ALPHA))).astype(_BF16)
            part = jnp.dot(act, w2_ref[c0:c0 + FF_CHUNK, :], preferred_element_type=_F32)
            y = part + b2_ref[...] if y is None else y + part
        ybuf[cur][rows, :] = y

        @pl.when(i == nact - 1)
        def _():
            sidx_copy(0, cur).wait()
            rolled(scatter_row, cur)
            wait_scatter(nxt)
            wait_scatter(cur)
            wait_gather(nxt)
            gidx_copy(0, cur).wait()

    for parity in range(2):
        @pl.when(jnp.logical_and(i < nact, i % 2 == parity))
        def _():
            step(parity)


def _experts(blk_exp, n_active, gidx, sidx, h2, w1, b1, w2, b2, n_tokens):
    n_blk = gidx.shape[0]
    eb = EXPERT_BLOCK
    n_real = TOP_K * n_tokens
    wmap = lambda i, be, na: (be[i], 0, 0)
    grid_spec = pltpu.PrefetchScalarGridSpec(
        num_scalar_prefetch=2,
        grid=(n_blk,),
        in_specs=[
            pl.BlockSpec(memory_space=pl.ANY),
            pl.BlockSpec(memory_space=pl.ANY),
            pl.BlockSpec(memory_space=pl.ANY),
            pl.BlockSpec((None, D_MODEL, 2 * D_FF), wmap),
            pl.BlockSpec((None, 1, 2 * D_FF), wmap),
            pl.BlockSpec((None, D_FF, D_MODEL), wmap),
            pl.BlockSpec((None, 1, D_MODEL), wmap),
        ],
        out_specs=pl.BlockSpec(memory_space=pl.ANY),
        scratch_shapes=[
            pltpu.SMEM((eb,), jnp.int32), pltpu.SMEM((eb,), jnp.int32),
            pltpu.SMEM((eb,), jnp.int32), pltpu.SMEM((eb,), jnp.int32),
            pltpu.VMEM((eb + SUBLANES, D_MODEL), _F32),
            pltpu.VMEM((eb + SUBLANES, D_MODEL), _F32),
            pltpu.VMEM((eb + SUBLANES, D_MODEL), _F32),
            pltpu.VMEM((eb + SUBLANES, D_MODEL), _F32),
            pltpu.SemaphoreType.DMA((2, 2)),
            pltpu.SemaphoreType.DMA((2,)),
            pltpu.SemaphoreType.DMA((2,)),
        ],
    )
    return pl.pallas_call(
        functools.partial(_expert_kernel, n_blocks=n_blk, n_real_rows=n_real),
        grid_spec=grid_spec,
        out_shape=jax.ShapeDtypeStruct((n_real + 2 * eb, D_MODEL), _F32),
        compiler_params=pltpu.CompilerParams(
            dimension_semantics=("arbitrary",), vmem_limit_bytes=VMEM_LIMIT),
        name="experts",
    )(blk_exp, n_active, gidx, sidx, h2, w1, b1, w2, b2)


def _combine_kernel(x1_ref, y0_ref, y1_ref, y2_ref, y3_ref, wts_ref, g_ref, o_ref):
    wt = wts_ref[...].T
    y = x1_ref[...]
    for k, y_ref in enumerate((y0_ref, y1_ref, y2_ref, y3_ref)):
        y = y + wt[:, k:k + 1] * y_ref[...]
    var = jnp.mean(y * y, axis=-1, keepdims=True)
    o_ref[...] = y * lax.rsqrt(var + RMS_EPS) * g_ref[...]


def _combine(x1, y, wts, gf):
    t = x1.shape[0]
    tm = TM_OUT
    nt = t // tm
    yspec = lambda k: pl.BlockSpec((tm, D_MODEL), lambda i: (k * nt + i, 0))
    return pl.pallas_call(
        _combine_kernel,
        grid=(nt,),
        in_specs=[
            pl.BlockSpec((tm, D_MODEL), lambda i: (i, 0)),
            yspec(0), yspec(1), yspec(2), yspec(3),
            pl.BlockSpec((8, tm), lambda i: (0, i)),
            _const_spec((1, D_MODEL)),
        ],
        out_specs=pl.BlockSpec((tm, D_MODEL), lambda i: (i, 0)),
        out_shape=jax.ShapeDtypeStruct((t, D_MODEL), _F32),
        compiler_params=pltpu.CompilerParams(
            dimension_semantics=("parallel",), vmem_limit_bytes=VMEM_LIMIT),
        name="combine",
    )(x1, y, y, y, y, wts, gf)


def _rope_tables(seq):
    half = SWA_HEAD_DIM // 2
    inv_freq = ROPE_THETA ** (-jnp.arange(half, dtype=_F32) / half)
    ang = jnp.arange(seq, dtype=_F32)[:, None] * inv_freq[None, :]
    cos, sin = jnp.cos(ang), jnp.sin(ang)
    cos_t = jnp.tile(jnp.concatenate([cos, cos], axis=1), (1, LANES // SWA_HEAD_DIM))
    sin_t = jnp.tile(jnp.concatenate([-sin, sin], axis=1), (1, LANES // SWA_HEAD_DIM))
    return cos_t, sin_t


def _na_bias_table(rpb):
    col = np.arange(GRID_W)
    cstart = np.clip(col - NA_COLS // 2, 0, GRID_W - NA_COLS)
    kc = np.arange(GRID_W)
    valid = (kc[None, :] >= cstart[:, None]) & (kc[None, :] < cstart[:, None] + NA_COLS)
    off = np.clip(kc[None, :] - col[:, None] + NA_COLS - 1, 0, 2 * NA_COLS - 2)
    ext = jnp.where(valid[None, None], rpb[:, :, off], NEG_BIG)
    u = np.arange(NA_ROWS)[None, :] - np.arange(NA_ROWS)[:, None] + NA_ROWS - 1
    tbl = ext[:, u]
    tbl = tbl.transpose(1, 0, 3, 2, 4).reshape(NA_ROWS, NA_HEADS // 2, 2 * GRID_W, NA_ROWS * GRID_W)
    return tbl.astype(_F32)


def kernel(x, norm1_g, w_in, b_in, na_rpb, swa_sinks, w_up_a, w_up_b, w_out, norm2_g, w_router,
           b_router, w1, b1, w2, b2, final_g):
    batch, seq, d = x.shape
    depth = w_in.shape[0]
    t = batch * seq
    assert depth == 1, "the final norm is fused into the single layer's combine step"
    assert d == D_MODEL and seq % TM_MIX == 0 and seq // GRID_W >= 2 * NA_ROWS and t % TM_OUT == 0

    group = SWA_Q_HEADS // SWA_KV_HEADS
    head_order = np.arange(SWA_Q_HEADS).reshape(SWA_KV_HEADS, group).T.reshape(-1)
    qb_cols = (head_order[:, None] * SWA_HEAD_DIM + np.arange(SWA_HEAD_DIM)[None, :]).reshape(-1)
    col_perm = np.arange(D_IN)
    col_perm[C_QB:C_KB] = C_QB + qb_cols
    cos_t, sin_t = _rope_tables(seq)
    tri = (np.arange(TM_MIX)[:, None] < np.arange(TM_MIX)[None, :]).astype(np.float32)
    tri = jnp.asarray(tri, _BF16)

    n_assign = t * TOP_K
    n_rows = n_assign + N_EXPERTS * EXPERT_BLOCK
    n_blk = n_rows // EXPERT_BLOCK
    n_tiles = t // TM_MIX

    x2 = x.reshape(t, d)
    for l in range(depth):
        w_in_l = w_in[l][:, col_perm].astype(_BF16)
        b_in_l = b_in[l][col_perm].reshape(1, D_IN)
        qkva, qb, kvb, gates = _inproj(x2, norm1_g[l].reshape(1, d), w_in_l, b_in_l, cos_t, sin_t, seq)

        x1, h2, ids, wts, rank, cnt = _mixer(
            swa_sinks[l][head_order].astype(_F32), x2, qkva, qb, kvb, gates, _na_bias_table(na_rpb[l]),
            w_up_a[l].astype(_BF16), w_up_b[l][qb_cols].astype(_BF16), w_out[l].astype(_BF16),
            norm2_g[l].reshape(1, d), w_router[l].T, b_router[l].reshape(N_EXPERTS, 1), tri, batch, seq)

        cnt = cnt.reshape(n_tiles, N_EXPERTS, LANES)[:, :, 0].astype(jnp.int32)
        counts = jnp.sum(cnt, axis=0)
        padded = (counts + EXPERT_BLOCK - 1) // EXPERT_BLOCK * EXPERT_BLOCK
        pend = jnp.cumsum(padded)
        base = (pend - padded)[None, :] + jnp.cumsum(cnt, axis=0) - cnt
        hot = ids[:TOP_K].reshape(TOP_K, n_tiles, TM_MIX, 1) == jnp.arange(N_EXPERTS, dtype=jnp.int32)
        dest = jnp.sum(jnp.where(hot, base[None, :, None, :], 0), axis=-1).reshape(TOP_K, t) + rank[:TOP_K]
        assign = jnp.arange(t, dtype=jnp.int32)[None, :] * TOP_K + jnp.arange(TOP_K, dtype=jnp.int32)[:, None]
        row_a = jnp.full((n_rows,), -1, jnp.int32).at[dest.reshape(-1)].set(assign.reshape(-1))
        row_a = row_a.reshape(n_blk, EXPERT_BLOCK)
        blk_start = jnp.arange(n_blk, dtype=jnp.int32) * EXPERT_BLOCK
        blk_exp = jnp.minimum(jnp.sum((pend[None, :] <= blk_start[:, None]).astype(jnp.int32), axis=1),
                              N_EXPERTS - 1)
        n_active = (pend[-1:] // EXPERT_BLOCK).astype(jnp.int32)
        r_in_blk = jnp.arange(EXPERT_BLOCK, dtype=jnp.int32)[None, :]
        parity = (jnp.arange(-1, n_blk, dtype=jnp.int32) % 2)[:, None]
        trash = n_assign + parity * EXPERT_BLOCK + r_in_blk
        gidx = jnp.where(row_a >= 0, row_a >> 2, 0)
        sidx = jnp.where(row_a >= 0, (row_a & (TOP_K - 1)) * t + (row_a >> 2), trash[1:])
        sidx = jnp.concatenate([trash[:1], sidx], axis=0)

        y = _experts(blk_exp, n_active, gidx, sidx, h2,
                     w1[l].astype(_BF16), b1[l].reshape(N_EXPERTS, 1, 2 * D_FF),
                     w2[l].astype(_BF16), b2[l].reshape(N_EXPERTS, 1, D_MODEL), t)
        x2 = _combine(x1, y, wts, final_g.reshape(1, d))
    return x2.reshape(batch, seq, d)
```

```python
import functools

import jax
import jax.numpy as jnp
import numpy as np
from jax import lax
from jax.experimental import pallas as pl
from jax.experimental.pallas import tpu as pltpu

D_MODEL = 1024
GRID_W = 64
NA_HEADS = 8
NA_HEAD_DIM = 64
NA_ROWS = 8
NA_COLS = 16
SWA_Q_HEADS = 8
SWA_KV_HEADS = 2
SWA_HEAD_DIM = 64
SWA_WINDOW = 128
SWA_BLOCK = 128
ROPE_THETA = 10000.0
N_EXPERTS = 32
TOP_K = 4
D_FF = 1024
SWIGLU_LIMIT = 7.0
SWIGLU_ALPHA = 1.702
EXPERT_BLOCK = 256
RMS_EPS = 1e-5

NA_WIDTH = NA_HEADS * NA_HEAD_DIM
SWA_Q_WIDTH = SWA_Q_HEADS * SWA_HEAD_DIM
SWA_KV_WIDTH = SWA_KV_HEADS * SWA_HEAD_DIM
LANES = 128
SUBLANES = 8
NEG_BIG = -1e30

C_QA, C_KA, C_VA = 0, NA_WIDTH, 2 * NA_WIDTH
C_QB = 3 * NA_WIDTH
C_KB = C_QB + SWA_Q_WIDTH
C_VB = C_KB + SWA_KV_WIDTH
C_GA = C_VB + SWA_KV_WIDTH
C_GB = C_GA + D_MODEL
D_IN = C_GB + D_MODEL

TM_PROJ = 512
TM_MIX = 512
TM_OUT = 512
FF_CHUNK = 256
VMEM_LIMIT = 56 * 1024 * 1024

_BF16 = jnp.bfloat16
_F32 = jnp.float32


def _const_spec(shape):
    nd = len(shape)
    return pl.BlockSpec(shape, lambda *_: (0,) * nd, pipeline_mode=pl.Buffered(1))


def _rope_slab(y, cos, sin_signed):
    lane = lax.broadcasted_iota(jnp.int32, y.shape, 1)
    first_half = (lane & (SWA_HEAD_DIM - 1)) < (SWA_HEAD_DIM // 2)
    rot = jnp.where(first_half, pltpu.roll(y, LANES - SWA_HEAD_DIM // 2, axis=1),
                    pltpu.roll(y, SWA_HEAD_DIM // 2, axis=1))
    return y * cos + rot * sin_signed


def _inproj_kernel(x_ref, g_ref, w_ref, b_ref, cos_ref, sin_ref,
                   qkva_ref, qb_ref, kvb_ref, gate_ref):
    x = x_ref[...]
    var = jnp.mean(x * x, axis=-1, keepdims=True)
    h = (x * lax.rsqrt(var + RMS_EPS) * g_ref[...]).astype(_BF16)

    def proj(c0, c1):
        return jnp.dot(h, w_ref[:, c0:c1], preferred_element_type=_F32) + b_ref[:, c0:c1]

    scale = NA_HEAD_DIM ** -0.5
    qkva_ref[:, C_QA:C_KA] = (proj(C_QA, C_KA) * scale).astype(_BF16)
    qkva_ref[:, C_KA:C_VA] = proj(C_KA, C_VA).astype(_BF16)
    qkva_ref[:, C_VA:C_QB] = proj(C_VA, C_QB).astype(_BF16)

    cos = cos_ref[...]
    sin = sin_ref[...]
    qb = proj(C_QB, C_KB)
    for s in range(SWA_Q_WIDTH // LANES):
        slab = _rope_slab(qb[:, s * LANES:(s + 1) * LANES], cos, sin)
        qb_ref[:, s * LANES:(s + 1) * LANES] = (slab * (SWA_HEAD_DIM ** -0.5)).astype(_BF16)
    kvb = proj(C_KB, C_GA)
    kvb_ref[:, 0:LANES] = _rope_slab(kvb[:, 0:LANES], cos, sin).astype(_BF16)
    kvb_ref[:, LANES:2 * LANES] = kvb[:, LANES:2 * LANES].astype(_BF16)

    for c0 in range(C_GA, D_IN, 512):
        gate_ref[:, c0 - C_GA:c0 - C_GA + 512] = jax.nn.sigmoid(proj(c0, c0 + 512)).astype(_BF16)


def _inproj(x2, g1, w_in, b_in, cos_t, sin_t, seq):
    t = x2.shape[0]
    tm = TM_PROJ
    nseq = seq // tm
    row = lambda i: (i, 0)
    return pl.pallas_call(
        _inproj_kernel,
        grid=(t // tm,),
        in_specs=[
            pl.BlockSpec((tm, D_MODEL), row),
            _const_spec((1, D_MODEL)),
            _const_spec((D_MODEL, D_IN)),
            _const_spec((1, D_IN)),
            pl.BlockSpec((tm, LANES), lambda i: (i % nseq, 0)),
            pl.BlockSpec((tm, LANES), lambda i: (i % nseq, 0)),
        ],
        out_specs=[
            pl.BlockSpec((tm, 3 * NA_WIDTH), row),
            pl.BlockSpec((tm, SWA_Q_WIDTH), row),
            pl.BlockSpec((tm, 2 * SWA_KV_WIDTH), row),
            pl.BlockSpec((tm, 2 * D_MODEL), row),
        ],
        out_shape=[
            jax.ShapeDtypeStruct((t, 3 * NA_WIDTH), _BF16),
            jax.ShapeDtypeStruct((t, SWA_Q_WIDTH), _BF16),
            jax.ShapeDtypeStruct((t, 2 * SWA_KV_WIDTH), _BF16),
            jax.ShapeDtypeStruct((t, 2 * D_MODEL), _BF16),
        ],
        compiler_params=pltpu.CompilerParams(
            dimension_semantics=("parallel",), vmem_limit_bytes=VMEM_LIMIT),
        name="inproj",
    )(x2, g1, w_in, b_in, cos_t, sin_t)


_NT_DIMS = (((1,), (1,)), ((), ()))


def _mixer_kernel(sink_ref, x_ref, q_ref, kp_ref, kc_ref, kn_ref, vp_ref, vc_ref, vn_ref,
                  qb_ref, kvp_ref, kvc_ref, kvn_ref, gate_ref, tbl_ref,
                  wua_ref, wub_ref, wo_ref, g2_ref, wr_ref, br_ref, tri_ref,
                  x1_ref, h2_ref, ids_ref, wts_ref, rank_ref, cnt_ref,
                  kcat, vcat, kvcat, oa_s, ob_s, *, rows, n_swa_blocks):
    j = pl.program_id(1)
    rows_per_tile = TM_MIX // GRID_W
    halo = (NA_ROWS // 2) * GRID_W
    band = NA_ROWS * GRID_W

    kcat[0:halo] = kp_ref[...]
    kcat[halo:halo + TM_MIX] = kc_ref[...]
    kcat[halo + TM_MIX:] = kn_ref[...]
    vcat[0:halo] = vp_ref[...]
    vcat[halo:halo + TM_MIX] = vc_ref[...]
    vcat[halo + TM_MIX:] = vn_ref[...]

    lane_q = lax.broadcasted_iota(jnp.int32, (GRID_W, LANES), 1)
    low_q = lane_q < NA_HEAD_DIM
    lane_o = lax.broadcasted_iota(jnp.int32, (GRID_W, LANES), 1) < NA_HEAD_DIM

    def na_row(i, carry):
        r = j * rows_per_tile + i
        rs = jnp.clip(r - NA_ROWS // 2, 0, rows - NA_ROWS)
        d = r - rs
        start = pl.multiple_of((rs - (j * rows_per_tile - NA_ROWS // 2)) * GRID_W, GRID_W)
        q0 = pl.multiple_of(i * GRID_W, GRID_W)
        for p in range(NA_HEADS // 2):
            cols = slice(p * LANES, (p + 1) * LANES)
            qpair = q_ref[pl.ds(q0, GRID_W), cols]
            zero = jnp.zeros_like(qpair)
            qs = jnp.concatenate([jnp.where(low_q, qpair, zero), jnp.where(low_q, zero, qpair)], axis=0)
            kb = kcat[pl.ds(start, band), cols]
            vb = vcat[pl.ds(start, band), cols]
            s = lax.dot_general(qs, kb, _NT_DIMS, preferred_element_type=_F32) + tbl_ref[d, p]
            m = jnp.max(s, axis=-1, keepdims=True)
            e = jnp.exp(s - m)
            l = jnp.sum(e, axis=-1, keepdims=True)
            o = jnp.dot(e.astype(_BF16), vb, preferred_element_type=_F32) * (1.0 / l)
            oa_s[pl.ds(q0, GRID_W), cols] = jnp.where(lane_o, o[:GRID_W], o[GRID_W:]).astype(_BF16)
        return carry

    lax.fori_loop(0, rows_per_tile, na_row, 0)

    kvcat[0:SWA_BLOCK] = kvp_ref[...]
    kvcat[SWA_BLOCK:SWA_BLOCK + TM_MIX] = kvc_ref[...]
    kvcat[SWA_BLOCK + TM_MIX:] = kvn_ref[...]

    n_slabs = SWA_Q_WIDTH // LANES
    stack = n_slabs * SWA_BLOCK
    lane_s = lax.broadcasted_iota(jnp.int32, (SWA_BLOCK, LANES), 1) < SWA_HEAD_DIM
    qpos = lax.broadcasted_iota(jnp.int32, (stack, 3 * SWA_BLOCK), 0) & (SWA_BLOCK - 1)
    kcol = lax.broadcasted_iota(jnp.int32, (stack, 3 * SWA_BLOCK), 1)
    rel_ok = jnp.abs(kcol - SWA_BLOCK - qpos) <= SWA_WINDOW
    rowblk = lax.broadcasted_iota(jnp.int32, (stack, 1), 0) // SWA_BLOCK

    def swa_block(n, carry):
        nb = j * (TM_MIX // SWA_BLOCK) + n
        t0 = pl.multiple_of(n * SWA_BLOCK, SWA_BLOCK)
        ok = rel_ok & ((kcol >= SWA_BLOCK) | (nb > 0)) & ((kcol < 2 * SWA_BLOCK) | (nb < n_swa_blocks - 1))
        kband = kvcat[pl.ds(t0, 3 * SWA_BLOCK), 0:LANES]
        vband = kvcat[pl.ds(t0, 3 * SWA_BLOCK), LANES:2 * LANES]
        outs = []
        for g in range(SWA_KV_HEADS):
            parts = []
            for s_ in range(n_slabs):
                slab = qb_ref[pl.ds(t0, SWA_BLOCK), s_ * LANES:(s_ + 1) * LANES]
                zero = jnp.zeros_like(slab)
                parts.append(jnp.where(lane_s, slab, zero) if g == 0 else jnp.where(lane_s, zero, slab))
            qg = jnp.concatenate(parts, axis=0)
            sink = jnp.zeros((stack, 1), _F32)
            for s_ in range(n_slabs):
                sink = jnp.where(rowblk == s_, sink_ref[SWA_KV_HEADS * s_ + g], sink)
            s = lax.dot_general(qg, kband, _NT_DIMS, preferred_element_type=_F32)
            s = jnp.where(ok, s, NEG_BIG)
            m = jnp.maximum(jnp.max(s, axis=-1, keepdims=True), sink)
            e = jnp.exp(s - m)
            den = jnp.sum(e, axis=-1, keepdims=True) + jnp.exp(sink - m)
            outs.append(jnp.dot(e.astype(_BF16), vband, preferred_element_type=_F32) * (1.0 / den))
        for s_ in range(n_slabs):
            rs_ = slice(s_ * SWA_BLOCK, (s_ + 1) * SWA_BLOCK)
            ob_s[pl.ds(t0, SWA_BLOCK), s_ * LANES:(s_ + 1) * LANES] = jnp.where(
                lane_s, outs[0][rs_], outs[1][rs_]).astype(_BF16)
        return carry

    lax.fori_loop(0, TM_MIX // SWA_BLOCK, swa_block, 0)

    ua = jnp.dot(oa_s[...], wua_ref[...], preferred_element_type=_F32)
    ub = jnp.dot(ob_s[...], wub_ref[...], preferred_element_type=_F32)
    merged = (gate_ref[:, 0:D_MODEL].astype(_F32) * ua
              + gate_ref[:, D_MODEL:].astype(_F32) * ub).astype(_BF16)
    x1 = x_ref[...] + jnp.dot(merged, wo_ref[...], preferred_element_type=_F32)
    x1_ref[...] = x1

    var = jnp.mean(x1 * x1, axis=-1, keepdims=True)
    h2 = x1 * lax.rsqrt(var + RMS_EPS) * g2_ref[...]
    h2_ref[...] = h2
    logits = lax.dot_general(wr_ref[...], h2, _NT_DIMS, preferred_element_type=_F32,
                             precision=lax.Precision.HIGHEST) + br_ref[...]
    eidx = lax.broadcasted_iota(jnp.int32, logits.shape, 0)
    vals, idxs, hots = [], [], []
    for _ in range(TOP_K):
        m = jnp.max(logits, axis=0, keepdims=True)
        idx = jnp.min(jnp.where(logits == m, eidx, N_EXPERTS), axis=0, keepdims=True)
        hot = eidx == idx
        logits = jnp.where(hot, -jnp.inf, logits)
        vals.append(m)
        idxs.append(idx)
        hots.append(hot)
    es = [jnp.exp(v - vals[0]) for v in vals]
    inv = 1.0 / (es[0] + es[1] + es[2] + es[3])
    sel = jnp.zeros(logits.shape, _F32)
    for hot in hots:
        sel = sel + jnp.where(hot, 1.0, 0.0)
    prefix = jnp.dot(sel.astype(_BF16), tri_ref[...], preferred_element_type=_F32)
    ranks = [jnp.sum(jnp.where(hot, prefix, 0.0), axis=0, keepdims=True).astype(jnp.int32) for hot in hots]
    pad_i = jnp.zeros((8 - TOP_K, TM_MIX), jnp.int32)
    ids_ref[...] = jnp.concatenate(idxs + [pad_i], axis=0)
    rank_ref[...] = jnp.concatenate(ranks + [pad_i], axis=0)
    wts_ref[...] = jnp.concatenate([e * inv for e in es] + [jnp.zeros((8 - TOP_K, TM_MIX), _F32)], axis=0)
    cnt_ref[...] = jnp.broadcast_to(jnp.sum(sel, axis=1, keepdims=True), (N_EXPERTS, LANES))


def _mixer(sinks_perm, x2, qkva, qb, kvb, gates, tbl, wua, wub, wo, g2, wr_t, br, tri, batch, seq):
    t = x2.shape[0]
    tm = TM_MIX
    nj = seq // tm
    rows = seq // GRID_W
    hb = tm // ((NA_ROWS // 2) * GRID_W)
    sb = tm // SWA_BLOCK
    n_halo = seq // ((NA_ROWS // 2) * GRID_W)
    n_swa = seq // SWA_BLOCK
    halo = (NA_ROWS // 2) * GRID_W

    tile = lambda b, j, *_: (b * nj + j, 0)

    def na_spec(col, which):
        if which == 0:
            return pl.BlockSpec((tm, NA_WIDTH), lambda b, j, *_: (b * nj + j, col))
        if which < 0:
            return pl.BlockSpec((halo, NA_WIDTH),
                                lambda b, j, *_: (b * n_halo + jnp.maximum(j * hb - 1, 0), col))
        return pl.BlockSpec((halo, NA_WIDTH),
                            lambda b, j, *_: (b * n_halo + jnp.minimum(j * hb + hb, n_halo - 1), col))

    kv_prev = pl.BlockSpec((SWA_BLOCK, 2 * SWA_KV_WIDTH),
                           lambda b, j, *_: (b * n_swa + jnp.maximum(j * sb - 1, 0), 0))
    kv_next = pl.BlockSpec((SWA_BLOCK, 2 * SWA_KV_WIDTH),
                           lambda b, j, *_: (b * n_swa + jnp.minimum(j * sb + sb, n_swa - 1), 0))

    def cspec(shape):
        nd = len(shape)
        return pl.BlockSpec(shape, lambda *_: (0,) * nd, pipeline_mode=pl.Buffered(1))

    grid_spec = pltpu.PrefetchScalarGridSpec(
        num_scalar_prefetch=1,
        grid=(batch, nj),
        in_specs=[
            pl.BlockSpec((tm, D_MODEL), tile),
            na_spec(0, 0),
            na_spec(1, -1), na_spec(1, 0), na_spec(1, 1),
            na_spec(2, -1), na_spec(2, 0), na_spec(2, 1),
            pl.BlockSpec((tm, SWA_Q_WIDTH), tile),
            kv_prev, pl.BlockSpec((tm, 2 * SWA_KV_WIDTH), tile), kv_next,
            pl.BlockSpec((tm, 2 * D_MODEL), tile),
            cspec(tbl.shape),
            cspec(wua.shape), cspec(wub.shape), cspec(wo.shape),
            cspec(g2.shape), cspec(wr_t.shape), cspec(br.shape), cspec(tri.shape),
        ],
        out_specs=[
            pl.BlockSpec((tm, D_MODEL), tile),
            pl.BlockSpec((tm, D_MODEL), tile),
            pl.BlockSpec((8, tm), lambda b, j, *_: (0, b * nj + j)),
            pl.BlockSpec((8, tm), lambda b, j, *_: (0, b * nj + j)),
            pl.BlockSpec((8, tm), lambda b, j, *_: (0, b * nj + j)),
            pl.BlockSpec((N_EXPERTS, LANES), tile),
        ],
        scratch_shapes=[
            pltpu.VMEM((tm + 2 * halo, NA_WIDTH), _BF16),
            pltpu.VMEM((tm + 2 * halo, NA_WIDTH), _BF16),
            pltpu.VMEM((tm + 2 * SWA_BLOCK, 2 * SWA_KV_WIDTH), _BF16),
            pltpu.VMEM((tm, NA_WIDTH), _BF16),
            pltpu.VMEM((tm, SWA_Q_WIDTH), _BF16),
        ],
    )
    return pl.pallas_call(
        functools.partial(_mixer_kernel, rows=rows, n_swa_blocks=n_swa),
        grid_spec=grid_spec,
        out_shape=[
            jax.ShapeDtypeStruct((t, D_MODEL), _F32),
            jax.ShapeDtypeStruct((t, D_MODEL), _F32),
            jax.ShapeDtypeStruct((8, t), jnp.int32),
            jax.ShapeDtypeStruct((8, t), _F32),
            jax.ShapeDtypeStruct((8, t), jnp.int32),
            jax.ShapeDtypeStruct((t // tm * N_EXPERTS, LANES), _F32),
        ],
        compiler_params=pltpu.CompilerParams(
            dimension_semantics=("parallel", "parallel"), vmem_limit_bytes=VMEM_LIMIT),
        name="mixer",
    )(sinks_perm, x2, qkva, qkva, qkva, qkva, qkva, qkva, qkva, qb, kvb, kvb, kvb, gates, tbl,
      wua, wub, wo, g2, wr_t, br, tri)


def _expert_kernel(bexp_ref, nact_ref, gidx_hbm, sidx_hbm, h2_hbm, w1_ref, b1_ref, w2_ref, b2_ref,
                   y_hbm, gs0, gs1, ss0, ss1, xbuf0, xbuf1, ybuf0, ybuf1, isem, gsem, ssem,
                   *, n_blocks, n_real_rows):
    i = pl.program_id(0)
    nact = nact_ref[0]
    eb = EXPERT_BLOCK
    gs, ss, xbuf, ybuf = (gs0, gs1), (ss0, ss1), (xbuf0, xbuf1), (ybuf0, ybuf1)
    rows = pl.ds(0, eb)

    def gidx_copy(blk, slot):
        return pltpu.make_async_copy(gidx_hbm.at[blk], gs[slot], isem.at[0, slot])

    def sidx_copy(entry, slot):
        return pltpu.make_async_copy(sidx_hbm.at[entry], ss[slot], isem.at[1, slot])

    def gather_row(slot, r, priority=0):
        pltpu.make_async_copy(h2_hbm.at[pl.ds(gs[slot][r], 1), :], xbuf[slot].at[pl.ds(r, 1), :],
                              gsem.at[slot]).start(priority=priority)

    def wait_gather(slot):
        pltpu.make_async_copy(h2_hbm.at[rows, :], xbuf[slot].at[rows, :], gsem.at[slot]).wait()

    def scatter_row(slot, r, priority=0):
        pltpu.make_async_copy(ybuf[slot].at[pl.ds(r, 1), :], y_hbm.at[pl.ds(ss[slot][r], 1), :],
                              ssem.at[slot]).start(priority=priority)

    def wait_scatter(slot):
        pltpu.make_async_copy(ybuf[slot].at[rows, :], y_hbm.at[rows, :], ssem.at[slot]).wait()

    def rolled(fn, slot):
        def body(r, c):
            fn(slot, r)
            return c
        lax.fori_loop(0, eb, body, 0)

    @pl.when(i == 0)
    def _():
        gidx_copy(0, 0).start()
        gidx_copy(0, 0).wait()
        rolled(gather_row, 0)
        gidx_copy(jnp.minimum(1, n_blocks - 1), 1).start()
        sidx_copy(0, 1).start()
        ybuf1[...] = jnp.zeros(ybuf1.shape, _F32)
        init = pltpu.make_async_copy(ybuf1.at[rows, :], y_hbm.at[pl.ds(n_real_rows, eb), :], ssem.at[0])
        init.start()
        init.wait()

    def step(cur):
        nxt = 1 - cur
        gidx_copy(jnp.minimum(i + 2, n_blocks - 1), cur).start()
        sidx_copy(i + 1, cur).start()
        gidx_copy(0, nxt).wait()
        sidx_copy(0, nxt).wait()
        wait_gather(cur)

        x = xbuf[cur][rows, :].astype(_BF16)
        n_chunks = D_FF // FF_CHUNK
        rows_per_group = 2 * eb // n_chunks
        spare = pl.ds(pl.multiple_of(eb + jnp.minimum(i, 0) * SUBLANES, SUBLANES), SUBLANES)
        y = None
        for jc in range(n_chunks):
            issue = gather_row if jc < n_chunks // 2 else scatter_row
            first = (jc % (n_chunks // 2)) * rows_per_group
            for r in range(first, first + rows_per_group):
                issue(nxt, r, r % 2)
            xbuf[nxt][spare, 0:FF_CHUNK] = jnp.zeros((SUBLANES, FF_CHUNK), _F32)
            ybuf[nxt][spare, 0:FF_CHUNK] = jnp.zeros((SUBLANES, FF_CHUNK), _F32)
            anchor = (xbuf[nxt][spare, 0:FF_CHUNK] + ybuf[nxt][spare, 0:FF_CHUNK])[0:1]
            c0 = jc * FF_CHUNK
            g = (jnp.dot(x, w1_ref[:, c0:c0 + FF_CHUNK], preferred_element_type=_F32)
                 + (b1_ref[:, c0:c0 + FF_CHUNK] + anchor))
            u = (jnp.dot(x, w1_ref[:, D_FF + c0:D_FF + c0 + FF_CHUNK], preferred_element_type=_F32)
                 + b1_ref[:, D_FF + c0:D_FF + c0 + FF_CHUNK])
            gate = jnp.minimum(g, SWIGLU_LIMIT)
            up = jnp.clip(u, -SWIGLU_LIMIT, SWIGLU_LIMIT)
            act = ((up + 1.0) * (gate * jax.nn.sigmoid(gate * SWIGLU_ALPHA))).astype(_BF16)
            part = jnp.dot(act, w2_ref[c0:c0 + FF_CHUNK, :], preferred_element_type=_F32)
            y = part + b2_ref[...] if y is None else y + part

        @pl.when(i >= 1)
        def _():
            wait_scatter(cur)

        ybuf[cur][rows, :] = y

        @pl.when(i == nact - 1)
        def _():
            sidx_copy(0, cur).wait()
            rolled(scatter_row, cur)
            wait_scatter(nxt)
            wait_scatter(cur)
            wait_gather(nxt)
            gidx_copy(0, cur).wait()

    for parity in range(2):
        @pl.when(jnp.logical_and(i < nact, i % 2 == parity))
        def _():
            step(parity)


def _experts(blk_exp, n_active, gidx, sidx, h2, w1, b1, w2, b2, n_tokens):
    n_blk = gidx.shape[0]
    eb = EXPERT_BLOCK
    n_real = TOP_K * n_tokens
    wmap = lambda i, be, na: (be[i], 0, 0)
    grid_spec = pltpu.PrefetchScalarGridSpec(
        num_scalar_prefetch=2,
        grid=(n_blk,),
        in_specs=[
            pl.BlockSpec(memory_space=pl.ANY),
            pl.BlockSpec(memory_space=pl.ANY),
            pl.BlockSpec(memory_space=pl.ANY),
            pl.BlockSpec((None, D_MODEL, 2 * D_FF), wmap),
            pl.BlockSpec((None, 1, 2 * D_FF), wmap),
            pl.BlockSpec((None, D_FF, D_MODEL), wmap),
            pl.BlockSpec((None, 1, D_MODEL), wmap),
        ],
        out_specs=pl.BlockSpec(memory_space=pl.ANY),
        scratch_shapes=[
            pltpu.SMEM((eb,), jnp.int32), pltpu.SMEM((eb,), jnp.int32),
            pltpu.SMEM((eb,), jnp.int32), pltpu.SMEM((eb,), jnp.int32),
            pltpu.VMEM((eb + SUBLANES, D_MODEL), _F32),
            pltpu.VMEM((eb + SUBLANES, D_MODEL), _F32),
            pltpu.VMEM((eb + SUBLANES, D_MODEL), _F32),
            pltpu.VMEM((eb + SUBLANES, D_MODEL), _F32),
            pltpu.SemaphoreType.DMA((2, 2)),
            pltpu.SemaphoreType.DMA((2,)),
            pltpu.SemaphoreType.DMA((2,)),
        ],
    )
    return pl.pallas_call(
        functools.partial(_expert_kernel, n_blocks=n_blk, n_real_rows=n_real),
        grid_spec=grid_spec,
        out_shape=jax.ShapeDtypeStruct((n_real + 2 * eb, D_MODEL), _F32),
        compiler_params=pltpu.CompilerParams(
            dimension_semantics=("arbitrary",), vmem_limit_bytes=VMEM_LIMIT),
        name="experts",
    )(blk_exp, n_active, gidx, sidx, h2, w1, b1, w2, b2)


def _combine_kernel(x1_ref, y0_ref, y1_ref, y2_ref, y3_ref, wts_ref, g_ref, o_ref):
    wt = wts_ref[...].T
    y = x1_ref[...]
    for k, y_ref in enumerate((y0_ref, y1_ref, y2_ref, y3_ref)):
        y = y + wt[:, k:k + 1] * y_ref[...]
    var = jnp.mean(y * y, axis=-1, keepdims=True)
    o_ref[...] = y * lax.rsqrt(var + RMS_EPS) * g_ref[...]


def _combine(x1, y, wts, gf):
    t = x1.shape[0]
    tm = TM_OUT
    nt = t // tm
    yspec = lambda k: pl.BlockSpec((tm, D_MODEL), lambda i: (k * nt + i, 0))
    return pl.pallas_call(
        _combine_kernel,
        grid=(nt,),
        in_specs=[
            pl.BlockSpec((tm, D_MODEL), lambda i: (i, 0)),
            yspec(0), yspec(1), yspec(2), yspec(3),
            pl.BlockSpec((8, tm), lambda i: (0, i)),
            _const_spec((1, D_MODEL)),
        ],
        out_specs=pl.BlockSpec((tm, D_MODEL), lambda i: (i, 0)),
        out_shape=jax.ShapeDtypeStruct((t, D_MODEL), _F32),
        compiler_params=pltpu.CompilerParams(
            dimension_semantics=("parallel",), vmem_limit_bytes=VMEM_LIMIT),
        name="combine",
    )(x1, y, y, y, y, wts, gf)


def _rope_tables(seq):
    half = SWA_HEAD_DIM // 2
    inv_freq = ROPE_THETA ** (-jnp.arange(half, dtype=_F32) / half)
    ang = jnp.arange(seq, dtype=_F32)[:, None] * inv_freq[None, :]
    cos, sin = jnp.cos(ang), jnp.sin(ang)
    cos_t = jnp.tile(jnp.concatenate([cos, cos], axis=1), (1, LANES // SWA_HEAD_DIM))
    sin_t = jnp.tile(jnp.concatenate([-sin, sin], axis=1), (1, LANES // SWA_HEAD_DIM))
    return cos_t, sin_t


def _na_bias_table(rpb):
    col = np.arange(GRID_W)
    cstart = np.clip(col - NA_COLS // 2, 0, GRID_W - NA_COLS)
    kc = np.arange(GRID_W)
    valid = (kc[None, :] >= cstart[:, None]) & (kc[None, :] < cstart[:, None] + NA_COLS)
    off = np.clip(kc[None, :] - col[:, None] + NA_COLS - 1, 0, 2 * NA_COLS - 2)
    ext = jnp.where(valid[None, None], rpb[:, :, off], NEG_BIG)
    u = np.arange(NA_ROWS)[None, :] - np.arange(NA_ROWS)[:, None] + NA_ROWS - 1
    tbl = ext[:, u]
    tbl = tbl.transpose(1, 0, 3, 2, 4).reshape(NA_ROWS, NA_HEADS // 2, 2 * GRID_W, NA_ROWS * GRID_W)
    return tbl.astype(_F32)


def kernel(x, norm1_g, w_in, b_in, na_rpb, swa_sinks, w_up_a, w_up_b, w_out, norm2_g, w_router,
           b_router, w1, b1, w2, b2, final_g):
    batch, seq, d = x.shape
    depth = w_in.shape[0]
    t = batch * seq
    assert depth == 1, "the final norm is fused into the single layer's combine step"
    assert d == D_MODEL and seq % TM_MIX == 0 and seq // GRID_W >= 2 * NA_ROWS and t % TM_OUT == 0

    group = SWA_Q_HEADS // SWA_KV_HEADS
    head_order = np.arange(SWA_Q_HEADS).reshape(SWA_KV_HEADS, group).T.reshape(-1)
    qb_cols = (head_order[:, None] * SWA_HEAD_DIM + np.arange(SWA_HEAD_DIM)[None, :]).reshape(-1)
    col_perm = np.arange(D_IN)
    col_perm[C_QB:C_KB] = C_QB + qb_cols
    cos_t, sin_t = _rope_tables(seq)
    tri = (np.arange(TM_MIX)[:, None] < np.arange(TM_MIX)[None, :]).astype(np.float32)
    tri = jnp.asarray(tri, _BF16)

    n_assign = t * TOP_K
    n_rows = n_assign + N_EXPERTS * EXPERT_BLOCK
    n_blk = n_rows // EXPERT_BLOCK
    n_tiles = t // TM_MIX

    x2 = x.reshape(t, d)
    for l in range(depth):
        w_in_l = w_in[l][:, col_perm].astype(_BF16)
        b_in_l = b_in[l][col_perm].reshape(1, D_IN)
        qkva, qb, kvb, gates = _inproj(x2, norm1_g[l].reshape(1, d), w_in_l, b_in_l, cos_t, sin_t, seq)

        x1, h2, ids, wts, rank, cnt = _mixer(
            swa_sinks[l][head_order].astype(_F32), x2, qkva, qb, kvb, gates, _na_bias_table(na_rpb[l]),
            w_up_a[l].astype(_BF16), w_up_b[l][qb_cols].astype(_BF16), w_out[l].astype(_BF16),
            norm2_g[l].reshape(1, d), w_router[l].T, b_router[l].reshape(N_EXPERTS, 1), tri, batch, seq)

        cnt = cnt.reshape(n_tiles, N_EXPERTS, LANES)[:, :, 0].astype(jnp.int32)
        counts = jnp.sum(cnt, axis=0)
        padded = (counts + EXPERT_BLOCK - 1) // EXPERT_BLOCK * EXPERT_BLOCK
        pend = jnp.cumsum(padded)
        base = (pend - padded)[None, :] + jnp.cumsum(cnt, axis=0) - cnt
        hot = ids[:TOP_K].reshape(TOP_K, n_tiles, TM_MIX, 1) == jnp.arange(N_EXPERTS, dtype=jnp.int32)
        dest = jnp.sum(jnp.where(hot, base[None, :, None, :], 0), axis=-1).reshape(TOP_K, t) + rank[:TOP_K]
        assign = jnp.arange(t, dtype=jnp.int32)[None, :] * TOP_K + jnp.arange(TOP_K, dtype=jnp.int32)[:, None]
        row_a = jnp.full((n_rows,), -1, jnp.int32).at[dest.reshape(-1)].set(assign.reshape(-1))
        row_a = row_a.reshape(n_blk, EXPERT_BLOCK)
        blk_start = jnp.arange(n_blk, dtype=jnp.int32) * EXPERT_BLOCK
        blk_exp = jnp.minimum(jnp.sum((pend[None, :] <= blk_start[:, None]).astype(jnp.int32), axis=1),
                              N_EXPERTS - 1)
        n_active = (pend[-1:] // EXPERT_BLOCK).astype(jnp.int32)
        r_in_blk = jnp.arange(EXPERT_BLOCK, dtype=jnp.int32)[None, :]
        parity = (jnp.arange(-1, n_blk, dtype=jnp.int32) % 2)[:, None]
        trash = n_assign + parity * EXPERT_BLOCK + r_in_blk
        gidx = jnp.where(row_a >= 0, row_a >> 2, 0)
        sidx = jnp.where(row_a >= 0, (row_a & (TOP_K - 1)) * t + (row_a >> 2), trash[1:])
        sidx = jnp.concatenate([trash[:1], sidx], axis=0)

        y = _experts(blk_exp, n_active, gidx, sidx, h2,
                     w1[l].astype(_BF16), b1[l].reshape(N_EXPERTS, 1, 2 * D_FF),
                     w2[l].astype(_BF16), b2[l].reshape(N_EXPERTS, 1, D_MODEL), t)
        x2 = _combine(x1, y, wts, final_g.reshape(1, d))
    return x2.reshape(batch, seq, d)
```

```python
import functools

import jax
import jax.numpy as jnp
import numpy as np
from jax import lax
from jax.experimental import pallas as pl
from jax.experimental.pallas import tpu as pltpu

D_MODEL = 1024
GRID_W = 64
NA_HEADS = 8
NA_HEAD_DIM = 64
NA_ROWS = 8
NA_COLS = 16
SWA_Q_HEADS = 8
SWA_KV_HEADS = 2
SWA_HEAD_DIM = 64
SWA_WINDOW = 128
SWA_BLOCK = 128
ROPE_THETA = 10000.0
N_EXPERTS = 32
TOP_K = 4
D_FF = 1024
SWIGLU_LIMIT = 7.0
SWIGLU_ALPHA = 1.702
EXPERT_BLOCK = 256
RMS_EPS = 1e-5

NA_WIDTH = NA_HEADS * NA_HEAD_DIM
SWA_Q_WIDTH = SWA_Q_HEADS * SWA_HEAD_DIM
SWA_KV_WIDTH = SWA_KV_HEADS * SWA_HEAD_DIM
LANES = 128
SUBLANES = 8
NEG_BIG = -1e30

C_QA, C_KA, C_VA = 0, NA_WIDTH, 2 * NA_WIDTH
C_QB = 3 * NA_WIDTH
C_KB = C_QB + SWA_Q_WIDTH
C_VB = C_KB + SWA_KV_WIDTH
C_GA = C_VB + SWA_KV_WIDTH
C_GB = C_GA + D_MODEL
D_IN = C_GB + D_MODEL

TM_PROJ = 512
TM_MIX = 512
TM_OUT = 512
FF_CHUNK = 256
NA_BATCH_ROWS = 4
LOG2E = 1.4426950408889634
VMEM_LIMIT = 56 * 1024 * 1024

_BF16 = jnp.bfloat16
_F32 = jnp.float32


def _const_spec(shape):
    nd = len(shape)
    return pl.BlockSpec(shape, lambda *_: (0,) * nd, pipeline_mode=pl.Buffered(1))


def _rope_slab(y, cos, sin_signed):
    lane = lax.broadcasted_iota(jnp.int32, y.shape, 1)
    first_half = (lane & (SWA_HEAD_DIM - 1)) < (SWA_HEAD_DIM // 2)
    rot = jnp.where(first_half, pltpu.roll(y, LANES - SWA_HEAD_DIM // 2, axis=1),
                    pltpu.roll(y, SWA_HEAD_DIM // 2, axis=1))
    return y * cos + rot * sin_signed


def _inproj_kernel(x_ref, g_ref, w_ref, b_ref, cos_ref, sin_ref,
                   qkva_ref, qb_ref, kvb_ref, gate_ref):
    x = x_ref[...]
    var = jnp.mean(x * x, axis=-1, keepdims=True)
    h = (x * lax.rsqrt(var + RMS_EPS) * g_ref[...]).astype(_BF16)

    def proj(c0, c1):
        return jnp.dot(h, w_ref[:, c0:c1], preferred_element_type=_F32) + b_ref[:, c0:c1]

    scale = NA_HEAD_DIM ** -0.5 * LOG2E
    qkva_ref[:, C_QA:C_KA] = (proj(C_QA, C_KA) * scale).astype(_BF16)
    qkva_ref[:, C_KA:C_VA] = proj(C_KA, C_VA).astype(_BF16)
    qkva_ref[:, C_VA:C_QB] = proj(C_VA, C_QB).astype(_BF16)

    cos = cos_ref[...]
    sin = sin_ref[...]
    qb = proj(C_QB, C_KB)
    for s in range(SWA_Q_WIDTH // LANES):
        slab = _rope_slab(qb[:, s * LANES:(s + 1) * LANES], cos, sin)
        qb_ref[:, s * LANES:(s + 1) * LANES] = (slab * (SWA_HEAD_DIM ** -0.5 * LOG2E)).astype(_BF16)
    kvb = proj(C_KB, C_GA)
    kvb_ref[:, 0:LANES] = _rope_slab(kvb[:, 0:LANES], cos, sin).astype(_BF16)
    kvb_ref[:, LANES:2 * LANES] = kvb[:, LANES:2 * LANES].astype(_BF16)

    for c0 in range(C_GA, D_IN, 512):
        gate_ref[:, c0 - C_GA:c0 - C_GA + 512] = jax.nn.sigmoid(proj(c0, c0 + 512)).astype(_BF16)


def _inproj(x2, g1, w_in, b_in, cos_t, sin_t, seq):
    t = x2.shape[0]
    tm = TM_PROJ
    nseq = seq // tm
    row = lambda i: (i, 0)
    return pl.pallas_call(
        _inproj_kernel,
        grid=(t // tm,),
        in_specs=[
            pl.BlockSpec((tm, D_MODEL), row),
            _const_spec((1, D_MODEL)),
            _const_spec((D_MODEL, D_IN)),
            _const_spec((1, D_IN)),
            pl.BlockSpec((tm, LANES), lambda i: (i % nseq, 0)),
            pl.BlockSpec((tm, LANES), lambda i: (i % nseq, 0)),
        ],
        out_specs=[
            pl.BlockSpec((tm, 3 * NA_WIDTH), row),
            pl.BlockSpec((tm, SWA_Q_WIDTH), row),
            pl.BlockSpec((tm, 2 * SWA_KV_WIDTH), row),
            pl.BlockSpec((tm, 2 * D_MODEL), row),
        ],
        out_shape=[
            jax.ShapeDtypeStruct((t, 3 * NA_WIDTH), _BF16),
            jax.ShapeDtypeStruct((t, SWA_Q_WIDTH), _BF16),
            jax.ShapeDtypeStruct((t, 2 * SWA_KV_WIDTH), _BF16),
            jax.ShapeDtypeStruct((t, 2 * D_MODEL), _BF16),
        ],
        compiler_params=pltpu.CompilerParams(
            dimension_semantics=("parallel",), vmem_limit_bytes=VMEM_LIMIT),
        name="inproj",
    )(x2, g1, w_in, b_in, cos_t, sin_t)


_NT_DIMS = (((1,), (1,)), ((), ()))


def _mixer_kernel(sink_ref, x_ref, q_ref, kp_ref, kc_ref, kn_ref, vp_ref, vc_ref, vn_ref,
                  qb_ref, kvp_ref, kvc_ref, kvn_ref, gate_ref, tbl_ref, mask_ref,
                  wua_ref, wub_ref, wo_ref, g2_ref, wr_ref, br_ref, tri_ref,
                  x1_ref, h2_ref, ids_ref, wts_ref, rank_ref, cnt_ref,
                  kcat, vcat, kvcat, oa_s, ob_s, s_s, e_s, inv_s, *, rows, n_swa_blocks):
    j = pl.program_id(1)
    rows_per_tile = TM_MIX // GRID_W
    halo = (NA_ROWS // 2) * GRID_W
    band = NA_ROWS * GRID_W

    kcat[0:halo] = kp_ref[...]
    kcat[halo:halo + TM_MIX] = kc_ref[...]
    kcat[halo + TM_MIX:] = kn_ref[...]
    vcat[0:halo] = vp_ref[...]
    vcat[halo:halo + TM_MIX] = vc_ref[...]
    vcat[halo + TM_MIX:] = vn_ref[...]

    lane_q = lax.broadcasted_iota(jnp.int32, (GRID_W, LANES), 1)
    low_q = lane_q < NA_HEAD_DIM
    lane_o = lax.broadcasted_iota(jnp.int32, (GRID_W, LANES), 1) < NA_HEAD_DIM

    n_pairs = NA_HEADS // 2
    chain = 2 * GRID_W

    def band_start(i):
        r = j * rows_per_tile + i
        rs = jnp.clip(r - NA_ROWS // 2, 0, rows - NA_ROWS)
        start = pl.multiple_of((rs - (j * rows_per_tile - NA_ROWS // 2)) * GRID_W, GRID_W)
        return start, r - rs

    for half in range(rows_per_tile // NA_BATCH_ROWS):
        def na_scores(b, carry):
            i = half * NA_BATCH_ROWS + b
            start, d = band_start(i)
            q0 = pl.multiple_of(i * GRID_W, GRID_W)
            for p in range(n_pairs):
                cols = slice(p * LANES, (p + 1) * LANES)
                qpair = q_ref[pl.ds(q0, GRID_W), cols]
                zero = jnp.zeros_like(qpair)
                qs = jnp.concatenate([jnp.where(low_q, qpair, zero), jnp.where(low_q, zero, qpair)], axis=0)
                kb = kcat[pl.ds(start, band), cols]
                s = lax.dot_general(qs, kb, _NT_DIMS, preferred_element_type=_F32) + tbl_ref[d, p]
                s_s[pl.ds(pl.multiple_of((b * n_pairs + p) * chain, chain), chain), :] = s
            return carry

        def na_softmax(b, carry):
            rws = pl.ds(pl.multiple_of(b * n_pairs * chain, n_pairs * chain), n_pairs * chain)
            s = s_s[rws, :]
            e = jnp.exp2(s - jnp.max(s, axis=-1, keepdims=True))
            e_s[rws, :] = e.astype(_BF16)
            inv_s[rws, :] = jnp.broadcast_to(1.0 / jnp.sum(e, axis=-1, keepdims=True), (n_pairs * chain, LANES))
            return carry

        def na_values(b, carry):
            i = half * NA_BATCH_ROWS + b
            start, _ = band_start(i)
            q0 = pl.multiple_of(i * GRID_W, GRID_W)
            for p in range(n_pairs):
                cols = slice(p * LANES, (p + 1) * LANES)
                rws = pl.ds(pl.multiple_of((b * n_pairs + p) * chain, chain), chain)
                vb = vcat[pl.ds(start, band), cols]
                o = jnp.dot(e_s[rws, :], vb, preferred_element_type=_F32) * inv_s[rws, :]
                oa_s[pl.ds(q0, GRID_W), cols] = jnp.where(lane_o, o[:GRID_W], o[GRID_W:]).astype(_BF16)
            return carry

        lax.fori_loop(0, NA_BATCH_ROWS, na_scores, 0)
        lax.fori_loop(0, NA_BATCH_ROWS, na_softmax, 0)
        lax.fori_loop(0, NA_BATCH_ROWS, na_values, 0)

    kvcat[0:SWA_BLOCK] = kvp_ref[...]
    kvcat[SWA_BLOCK:SWA_BLOCK + TM_MIX] = kvc_ref[...]
    kvcat[SWA_BLOCK + TM_MIX:] = kvn_ref[...]

    n_slabs = SWA_Q_WIDTH // LANES
    stack = n_slabs * SWA_BLOCK
    lane_s = lax.broadcasted_iota(jnp.int32, (SWA_BLOCK, LANES), 1) < SWA_HEAD_DIM
    rowblk = lax.broadcasted_iota(jnp.int32, (stack, 1), 0) // SWA_BLOCK
    wband = 3 * SWA_BLOCK
    sinks = []
    for g in range(SWA_KV_HEADS):
        sink = jnp.zeros((stack, 1), _F32)
        for s_ in range(n_slabs):
            sink = jnp.where(rowblk == s_, sink_ref[SWA_KV_HEADS * s_ + g], sink)
        sinks.append(sink)

    def swa_block(n, carry):
        nb = j * (TM_MIX // SWA_BLOCK) + n
        t0 = pl.multiple_of(n * SWA_BLOCK, SWA_BLOCK)
        variant = jnp.where(nb == 0, 0, jnp.where(nb == n_swa_blocks - 1, 2, 1))
        mask = mask_ref[variant]
        mask = jnp.concatenate([mask] * n_slabs, axis=0)
        kband = kvcat[pl.ds(t0, wband), 0:LANES]
        vband = kvcat[pl.ds(t0, wband), LANES:2 * LANES]
        outs = []
        for g in range(SWA_KV_HEADS):
            parts = []
            for s_ in range(n_slabs):
                slab = qb_ref[pl.ds(t0, SWA_BLOCK), s_ * LANES:(s_ + 1) * LANES]
                zero = jnp.zeros_like(slab)
                parts.append(jnp.where(lane_s, slab, zero) if g == 0 else jnp.where(lane_s, zero, slab))
            qg = jnp.concatenate(parts, axis=0)
            s = lax.dot_general(qg, kband, _NT_DIMS, preferred_element_type=_F32) + mask
            m = jnp.maximum(jnp.max(s, axis=-1, keepdims=True), sinks[g])
            e = jnp.exp2(s - m)
            den = jnp.sum(e, axis=-1, keepdims=True) + jnp.exp2(sinks[g] - m)
            outs.append(jnp.dot(e.astype(_BF16), vband, preferred_element_type=_F32) * (1.0 / den))
        for s_ in range(n_slabs):
            rs_ = slice(s_ * SWA_BLOCK, (s_ + 1) * SWA_BLOCK)
            ob_s[pl.ds(t0, SWA_BLOCK), s_ * LANES:(s_ + 1) * LANES] = jnp.where(
                lane_s, outs[0][rs_], outs[1][rs_]).astype(_BF16)
        return carry

    lax.fori_loop(0, TM_MIX // SWA_BLOCK, swa_block, 0)

    ua = jnp.dot(oa_s[...], wua_ref[...], preferred_element_type=_F32)
    ub = jnp.dot(ob_s[...], wub_ref[...], preferred_element_type=_F32)
    merged = (gate_ref[:, 0:D_MODEL].astype(_F32) * ua
              + gate_ref[:, D_MODEL:].astype(_F32) * ub).astype(_BF16)
    x1 = x_ref[...] + jnp.dot(merged, wo_ref[...], preferred_element_type=_F32)
    x1_ref[...] = x1

    var = jnp.mean(x1 * x1, axis=-1, keepdims=True)
    h2 = x1 * lax.rsqrt(var + RMS_EPS) * g2_ref[...]
    h2_ref[...] = h2
    h2_hi = h2.astype(_BF16)
    h2_lo = (h2 - h2_hi.astype(_F32)).astype(_BF16)
    both = lax.dot_general(wr_ref[...], h2_hi, _NT_DIMS, preferred_element_type=_F32)
    cross = lax.dot_general(wr_ref[0:N_EXPERTS], h2_lo, _NT_DIMS, preferred_element_type=_F32)
    logits = both[0:N_EXPERTS] + both[N_EXPERTS:] + cross + br_ref[...]
    eidx = lax.broadcasted_iota(jnp.int32, logits.shape, 0)
    vals, idxs, hots = [], [], []
    for _ in range(TOP_K):
        m = jnp.max(logits, axis=0, keepdims=True)
        idx = jnp.min(jnp.where(logits == m, eidx, N_EXPERTS), axis=0, keepdims=True)
        hot = eidx == idx
        logits = jnp.where(hot, -jnp.inf, logits)
        vals.append(m)
        idxs.append(idx)
        hots.append(hot)
    es = [jnp.exp(v - vals[0]) for v in vals]
    inv = 1.0 / (es[0] + es[1] + es[2] + es[3])
    sel = jnp.zeros(logits.shape, _F32)
    for hot in hots:
        sel = sel + jnp.where(hot, 1.0, 0.0)
    prefix = jnp.dot(sel.astype(_BF16), tri_ref[...], preferred_element_type=_F32)
    ranks = [jnp.sum(jnp.where(hot, prefix, 0.0), axis=0, keepdims=True).astype(jnp.int32) for hot in hots]
    pad_i = jnp.zeros((8 - TOP_K, TM_MIX), jnp.int32)
    ids_ref[...] = jnp.concatenate(idxs + [pad_i], axis=0)
    rank_ref[...] = jnp.concatenate(ranks + [pad_i], axis=0)
    wts_ref[...] = jnp.concatenate([e * inv for e in es] + [jnp.zeros((8 - TOP_K, TM_MIX), _F32)], axis=0)
    cnt_ref[...] = jnp.broadcast_to(jnp.sum(sel, axis=1, keepdims=True), (N_EXPERTS, LANES))


def _swa_mask_table():
    qpos = np.arange(SWA_BLOCK)[:, None]
    koff = np.arange(3 * SWA_BLOCK)[None, :] - SWA_BLOCK
    rel_ok = np.abs(koff - qpos) <= SWA_WINDOW
    has_prev = np.array([False, True, True])[:, None, None]
    has_next = np.array([True, True, False])[:, None, None]
    ok = rel_ok[None] & ((koff >= 0)[None] | has_prev) & ((koff < SWA_BLOCK)[None] | has_next)
    return jnp.asarray(np.where(ok, 0.0, NEG_BIG), _F32)


def _mixer(sinks_perm, x2, qkva, qb, kvb, gates, tbl, wua, wub, wo, g2, wr_t, br, tri, batch, seq):
    t = x2.shape[0]
    tm = TM_MIX
    nj = seq // tm
    rows = seq // GRID_W
    hb = tm // ((NA_ROWS // 2) * GRID_W)
    sb = tm // SWA_BLOCK
    n_halo = seq // ((NA_ROWS // 2) * GRID_W)
    n_swa = seq // SWA_BLOCK
    halo = (NA_ROWS // 2) * GRID_W
    swa_mask = _swa_mask_table()
    score_rows = NA_BATCH_ROWS * NA_HEADS * GRID_W
    assert n_swa >= 2

    tile = lambda b, j, *_: (b * nj + j, 0)

    def na_spec(col, which):
        if which == 0:
            return pl.BlockSpec((tm, NA_WIDTH), lambda b, j, *_: (b * nj + j, col))
        if which < 0:
            return pl.BlockSpec((halo, NA_WIDTH),
                                lambda b, j, *_: (b * n_halo + jnp.maximum(j * hb - 1, 0), col))
        return pl.BlockSpec((halo, NA_WIDTH),
                            lambda b, j, *_: (b * n_halo + jnp.minimum(j * hb + hb, n_halo - 1), col))

    kv_prev = pl.BlockSpec((SWA_BLOCK, 2 * SWA_KV_WIDTH),
                           lambda b, j, *_: (b * n_swa + jnp.maximum(j * sb - 1, 0), 0))
    kv_next = pl.BlockSpec((SWA_BLOCK, 2 * SWA_KV_WIDTH),
                           lambda b, j, *_: (b * n_swa + jnp.minimum(j * sb + sb, n_swa - 1), 0))

    def cspec(shape):
        nd = len(shape)
        return pl.BlockSpec(shape, lambda *_: (0,) * nd, pipeline_mode=pl.Buffered(1))

    grid_spec = pltpu.PrefetchScalarGridSpec(
        num_scalar_prefetch=1,
        grid=(batch, nj),
        in_specs=[
            pl.BlockSpec((tm, D_MODEL), tile),
            na_spec(0, 0),
            na_spec(1, -1), na_spec(1, 0), na_spec(1, 1),
            na_spec(2, -1), na_spec(2, 0), na_spec(2, 1),
            pl.BlockSpec((tm, SWA_Q_WIDTH), tile),
            kv_prev, pl.BlockSpec((tm, 2 * SWA_KV_WIDTH), tile), kv_next,
            pl.BlockSpec((tm, 2 * D_MODEL), tile),
            cspec(tbl.shape), cspec(swa_mask.shape),
            cspec(wua.shape), cspec(wub.shape), cspec(wo.shape),
            cspec(g2.shape), cspec(wr_t.shape), cspec(br.shape), cspec(tri.shape),
        ],
        out_specs=[
            pl.BlockSpec((tm, D_MODEL), tile),
            pl.BlockSpec((tm, D_MODEL), tile),
            pl.BlockSpec((8, tm), lambda b, j, *_: (0, b * nj + j)),
            pl.BlockSpec((8, tm), lambda b, j, *_: (0, b * nj + j)),
            pl.BlockSpec((8, tm), lambda b, j, *_: (0, b * nj + j)),
            pl.BlockSpec((N_EXPERTS, LANES), tile),
        ],
        scratch_shapes=[
            pltpu.VMEM((tm + 2 * halo, NA_WIDTH), _BF16),
            pltpu.VMEM((tm + 2 * halo, NA_WIDTH), _BF16),
            pltpu.VMEM((tm + 2 * SWA_BLOCK, 2 * SWA_KV_WIDTH), _BF16),
            pltpu.VMEM((tm, NA_WIDTH), _BF16),
            pltpu.VMEM((tm, SWA_Q_WIDTH), _BF16),
            pltpu.VMEM((score_rows, NA_ROWS * GRID_W), _F32),
            pltpu.VMEM((score_rows, NA_ROWS * GRID_W), _BF16),
            pltpu.VMEM((score_rows, LANES), _F32),
        ],
    )
    return pl.pallas_call(
        functools.partial(_mixer_kernel, rows=rows, n_swa_blocks=n_swa),
        grid_spec=grid_spec,
        out_shape=[
            jax.ShapeDtypeStruct((t, D_MODEL), _F32),
            jax.ShapeDtypeStruct((t, D_MODEL), _F32),
            jax.ShapeDtypeStruct((8, t), jnp.int32),
            jax.ShapeDtypeStruct((8, t), _F32),
            jax.ShapeDtypeStruct((8, t), jnp.int32),
            jax.ShapeDtypeStruct((t // tm * N_EXPERTS, LANES), _F32),
        ],
        compiler_params=pltpu.CompilerParams(
            dimension_semantics=("parallel", "parallel"), vmem_limit_bytes=VMEM_LIMIT),
        name="mixer",
    )(sinks_perm, x2, qkva, qkva, qkva, qkva, qkva, qkva, qkva, qb, kvb, kvb, kvb, gates, tbl, swa_mask,
      wua, wub, wo, g2, wr_t, br, tri)


def _expert_kernel(bexp_ref, nact_ref, gidx_hbm, sidx_hbm, h2_hbm, w1_ref, b1_ref, w2_ref, b2_ref,
                   y_hbm, gs0, gs1, ss0, ss1, xbuf0, xbuf1, ybuf0, ybuf1, w1b, w2b, isem, gsem, ssem,
                   *, n_blocks, n_real_rows):
    i = pl.program_id(0)
    nact = nact_ref[0]
    eb = EXPERT_BLOCK
    gs, ss, xbuf, ybuf = (gs0, gs1), (ss0, ss1), (xbuf0, xbuf1), (ybuf0, ybuf1)
    rows = pl.ds(0, eb)

    def gidx_copy(blk, slot):
        return pltpu.make_async_copy(gidx_hbm.at[blk], gs[slot], isem.at[0, slot])

    def sidx_copy(entry, slot):
        return pltpu.make_async_copy(sidx_hbm.at[entry], ss[slot], isem.at[1, slot])

    def gather_row(slot, r, priority=0):
        pltpu.make_async_copy(h2_hbm.at[pl.ds(gs[slot][r], 1), :], xbuf[slot].at[pl.ds(r, 1), :],
                              gsem.at[slot]).start(priority=priority)

    def wait_gather(slot):
        pltpu.make_async_copy(h2_hbm.at[rows, :], xbuf[slot].at[rows, :], gsem.at[slot]).wait()

    def scatter_row(slot, r, priority=0):
        pltpu.make_async_copy(ybuf[slot].at[pl.ds(r, 1), :], y_hbm.at[pl.ds(ss[slot][r], 1), :],
                              ssem.at[slot]).start(priority=priority)

    def wait_scatter(slot):
        pltpu.make_async_copy(ybuf[slot].at[rows, :], y_hbm.at[rows, :], ssem.at[slot]).wait()

    def rolled(fn, slot):
        def body(r, c):
            fn(slot, r)
            return c
        lax.fori_loop(0, eb, body, 0)

    @pl.when(i == 0)
    def _():
        gidx_copy(0, 0).start()
        gidx_copy(0, 0).wait()
        rolled(gather_row, 0)
        gidx_copy(jnp.minimum(1, n_blocks - 1), 1).start()
        sidx_copy(0, 1).start()
        ybuf1[...] = jnp.zeros(ybuf1.shape, _F32)
        init = pltpu.make_async_copy(ybuf1.at[rows, :], y_hbm.at[pl.ds(n_real_rows, eb), :], ssem.at[0])
        init.start()
        init.wait()

    def step(cur):
        nxt = 1 - cur
        gidx_copy(jnp.minimum(i + 2, n_blocks - 1), cur).start()
        sidx_copy(i + 1, cur).start()
        gidx_copy(0, nxt).wait()
        sidx_copy(0, nxt).wait()
        wait_gather(cur)

        x = xbuf[cur][rows, :].astype(_BF16)
        n_chunks = D_FF // FF_CHUNK
        rows_per_group = 2 * eb // n_chunks
        spare = pl.ds(pl.multiple_of(eb + jnp.minimum(i, 0) * SUBLANES, SUBLANES), SUBLANES)
        y = None
        for jc in range(n_chunks):
            issue = gather_row if jc < n_chunks // 2 else scatter_row
            first = (jc % (n_chunks // 2)) * rows_per_group
            for r in range(first, first + rows_per_group):
                issue(nxt, r, r % 2)
            xbuf[nxt][spare, 0:FF_CHUNK] = jnp.zeros((SUBLANES, FF_CHUNK), _F32)
            ybuf[nxt][spare, 0:FF_CHUNK] = jnp.zeros((SUBLANES, FF_CHUNK), _F32)
            anchor = (xbuf[nxt][spare, 0:FF_CHUNK] + ybuf[nxt][spare, 0:FF_CHUNK])[0:1]
            c0 = jc * FF_CHUNK
            g = (jnp.dot(x, w1b[:, c0:c0 + FF_CHUNK], preferred_element_type=_F32)
                 + (b1_ref[:, c0:c0 + FF_CHUNK] + anchor))
            u = (jnp.dot(x, w1b[:, D_FF + c0:D_FF + c0 + FF_CHUNK], preferred_element_type=_F32)
                 + b1_ref[:, D_FF + c0:D_FF + c0 + FF_CHUNK])
            gate = jnp.minimum(g, SWIGLU_LIMIT)
            up = jnp.clip(u, -SWIGLU_LIMIT, SWIGLU_LIMIT)
            act = ((up + 1.0) * (gate * jax.nn.sigmoid(gate * SWIGLU_ALPHA))).astype(_BF16)
            part = jnp.dot(act, w2b[c0:c0 + FF_CHUNK, :], preferred_element_type=_F32)
            y = part + b2_ref[...] if y is None else y + part

        @pl.when(i >= 1)
        def _():
            wait_scatter(cur)

        ybuf[cur][rows, :] = y

        @pl.when(i == nact - 1)
        def _():
            sidx_copy(0, cur).wait()
            rolled(scatter_row, cur)
            wait_scatter(nxt)
            wait_scatter(cur)
            wait_gather(nxt)
            gidx_copy(0, cur).wait()

    new_expert = jnp.logical_or(i == 0, bexp_ref[i] != bexp_ref[jnp.maximum(i - 1, 0)])

    @pl.when(jnp.logical_and(i < nact, new_expert))
    def _():
        w1b[...] = w1_ref[...].astype(_BF16)
        w2b[...] = w2_ref[...].astype(_BF16)

    for parity in range(2):
        @pl.when(jnp.logical_and(i < nact, i % 2 == parity))
        def _():
            step(parity)


def _experts(blk_exp, n_active, gidx, sidx, h2, w1, b1, w2, b2, n_tokens):
    n_blk = gidx.shape[0]
    eb = EXPERT_BLOCK
    n_real = TOP_K * n_tokens
    wmap = lambda i, be, na: (be[i], 0, 0)
    grid_spec = pltpu.PrefetchScalarGridSpec(
        num_scalar_prefetch=2,
        grid=(n_blk,),
        in_specs=[
            pl.BlockSpec(memory_space=pl.ANY),
            pl.BlockSpec(memory_space=pl.ANY),
            pl.BlockSpec(memory_space=pl.ANY),
            pl.BlockSpec((None, D_MODEL, 2 * D_FF), wmap),
            pl.BlockSpec((None, 1, 2 * D_FF), wmap),
            pl.BlockSpec((None, D_FF, D_MODEL), wmap),
            pl.BlockSpec((None, 1, D_MODEL), wmap),
        ],
        out_specs=pl.BlockSpec(memory_space=pl.ANY),
        scratch_shapes=[
            pltpu.SMEM((eb,), jnp.int32), pltpu.SMEM((eb,), jnp.int32),
            pltpu.SMEM((eb,), jnp.int32), pltpu.SMEM((eb,), jnp.int32),
            pltpu.VMEM((eb + SUBLANES, D_MODEL), _F32),
            pltpu.VMEM((eb + SUBLANES, D_MODEL), _F32),
            pltpu.VMEM((eb + SUBLANES, D_MODEL), _F32),
            pltpu.VMEM((eb + SUBLANES, D_MODEL), _F32),
            pltpu.VMEM((D_MODEL, 2 * D_FF), _BF16),
            pltpu.VMEM((D_FF, D_MODEL), _BF16),
            pltpu.SemaphoreType.DMA((2, 2)),
            pltpu.SemaphoreType.DMA((2,)),
            pltpu.SemaphoreType.DMA((2,)),
        ],
    )
    return pl.pallas_call(
        functools.partial(_expert_kernel, n_blocks=n_blk, n_real_rows=n_real),
        grid_spec=grid_spec,
        out_shape=jax.ShapeDtypeStruct((n_real + 2 * eb, D_MODEL), _F32),
        compiler_params=pltpu.CompilerParams(
            dimension_semantics=("arbitrary",), vmem_limit_bytes=VMEM_LIMIT),
        name="experts",
    )(blk_exp, n_active, gidx, sidx, h2, w1, b1, w2, b2)


def _combine_kernel(x1_ref, y0_ref, y1_ref, y2_ref, y3_ref, wts_ref, g_ref, o_ref):
    wt = wts_ref[...].T
    y = x1_ref[...]
    for k, y_ref in enumerate((y0_ref, y1_ref, y2_ref, y3_ref)):
        y = y + wt[:, k:k + 1] * y_ref[...]
    var = jnp.mean(y * y, axis=-1, keepdims=True)
    o_ref[...] = y * lax.rsqrt(var + RMS_EPS) * g_ref[...]


def _combine(x1, y, wts, gf):
    t = x1.shape[0]
    tm = TM_OUT
    nt = t // tm
    yspec = lambda k: pl.BlockSpec((tm, D_MODEL), lambda i: (k * nt + i, 0))
    return pl.pallas_call(
        _combine_kernel,
        grid=(nt,),
        in_specs=[
            pl.BlockSpec((tm, D_MODEL), lambda i: (i, 0)),
            yspec(0), yspec(1), yspec(2), yspec(3),
            pl.BlockSpec((8, tm), lambda i: (0, i)),
            _const_spec((1, D_MODEL)),
        ],
        out_specs=pl.BlockSpec((tm, D_MODEL), lambda i: (i, 0)),
        out_shape=jax.ShapeDtypeStruct((t, D_MODEL), _F32),
        compiler_params=pltpu.CompilerParams(
            dimension_semantics=("parallel",), vmem_limit_bytes=VMEM_LIMIT),
        name="combine",
    )(x1, y, y, y, y, wts, gf)


def _split_bf16(w):
    hi = w.astype(_BF16)
    lo = (w - hi.astype(_F32)).astype(_BF16)
    return jnp.concatenate([hi, lo], axis=0)


def _rope_tables(seq):
    half = SWA_HEAD_DIM // 2
    inv_freq = ROPE_THETA ** (-jnp.arange(half, dtype=_F32) / half)
    ang = jnp.arange(seq, dtype=_F32)[:, None] * inv_freq[None, :]
    cos, sin = jnp.cos(ang), jnp.sin(ang)
    cos_t = jnp.tile(jnp.concatenate([cos, cos], axis=1), (1, LANES // SWA_HEAD_DIM))
    sin_t = jnp.tile(jnp.concatenate([-sin, sin], axis=1), (1, LANES // SWA_HEAD_DIM))
    return cos_t, sin_t


def _na_bias_table(rpb):
    col = np.arange(GRID_W)
    cstart = np.clip(col - NA_COLS // 2, 0, GRID_W - NA_COLS)
    kc = np.arange(GRID_W)
    valid = (kc[None, :] >= cstart[:, None]) & (kc[None, :] < cstart[:, None] + NA_COLS)
    off = np.clip(kc[None, :] - col[:, None] + NA_COLS - 1, 0, 2 * NA_COLS - 2)
    ext = jnp.where(valid[None, None], rpb[:, :, off], NEG_BIG)
    u = np.arange(NA_ROWS)[None, :] - np.arange(NA_ROWS)[:, None] + NA_ROWS - 1
    tbl = ext[:, u]
    tbl = tbl.transpose(1, 0, 3, 2, 4).reshape(NA_ROWS, NA_HEADS // 2, 2 * GRID_W, NA_ROWS * GRID_W)
    return (tbl * LOG2E).astype(_F32)


def kernel(x, norm1_g, w_in, b_in, na_rpb, swa_sinks, w_up_a, w_up_b, w_out, norm2_g, w_router,
           b_router, w1, b1, w2, b2, final_g):
    batch, seq, d = x.shape
    depth = w_in.shape[0]
    t = batch * seq
    assert depth == 1, "the final norm is fused into the single layer's combine step"
    assert d == D_MODEL and seq % TM_MIX == 0 and seq // GRID_W >= 2 * NA_ROWS and t % TM_OUT == 0

    group = SWA_Q_HEADS // SWA_KV_HEADS
    head_order = np.arange(SWA_Q_HEADS).reshape(SWA_KV_HEADS, group).T.reshape(-1)
    qb_cols = (head_order[:, None] * SWA_HEAD_DIM + np.arange(SWA_HEAD_DIM)[None, :]).reshape(-1)
    col_perm = np.arange(D_IN)
    col_perm[C_QB:C_KB] = C_QB + qb_cols
    cos_t, sin_t = _rope_tables(seq)
    tri = (np.arange(TM_MIX)[:, None] < np.arange(TM_MIX)[None, :]).astype(np.float32)
    tri = jnp.asarray(tri, _BF16)

    n_assign = t * TOP_K
    n_rows = n_assign + N_EXPERTS * EXPERT_BLOCK
    n_blk = n_rows // EXPERT_BLOCK
    n_tiles = t // TM_MIX

    x2 = x.reshape(t, d)
    for l in range(depth):
        w_in_l = w_in[l][:, col_perm].astype(_BF16)
        b_in_l = b_in[l][col_perm].reshape(1, D_IN)
        qkva, qb, kvb, gates = _inproj(x2, norm1_g[l].reshape(1, d), w_in_l, b_in_l, cos_t, sin_t, seq)

        x1, h2, ids, wts, rank, cnt = _mixer(
            swa_sinks[l][head_order].astype(_F32) * LOG2E, x2, qkva, qb, kvb, gates, _na_bias_table(na_rpb[l]),
            w_up_a[l].astype(_BF16), w_up_b[l][qb_cols].astype(_BF16), w_out[l].astype(_BF16),
            norm2_g[l].reshape(1, d), _split_bf16(w_router[l].T), b_router[l].reshape(N_EXPERTS, 1),
            tri, batch, seq)

        cnt = cnt.reshape(n_tiles, N_EXPERTS, LANES)[:, :, 0].astype(jnp.int32)
        counts = jnp.sum(cnt, axis=0)
        padded = (counts + EXPERT_BLOCK - 1) // EXPERT_BLOCK * EXPERT_BLOCK
        pend = jnp.cumsum(padded)
        base = (pend - padded)[None, :] + jnp.cumsum(cnt, axis=0) - cnt
        hot = ids[:TOP_K].reshape(TOP_K, n_tiles, TM_MIX, 1) == jnp.arange(N_EXPERTS, dtype=jnp.int32)
        dest = jnp.sum(jnp.where(hot, base[None, :, None, :], 0), axis=-1).reshape(TOP_K, t) + rank[:TOP_K]
        assign = jnp.arange(t, dtype=jnp.int32)[None, :] * TOP_K + jnp.arange(TOP_K, dtype=jnp.int32)[:, None]
        row_a = jnp.full((n_rows,), -1, jnp.int32).at[dest.reshape(-1)].set(assign.reshape(-1))
        row_a = row_a.reshape(n_blk, EXPERT_BLOCK)
        blk_start = jnp.arange(n_blk, dtype=jnp.int32) * EXPERT_BLOCK
        blk_exp = jnp.minimum(jnp.sum((pend[None, :] <= blk_start[:, None]).astype(jnp.int32), axis=1),
                              N_EXPERTS - 1)
        n_active = (pend[-1:] // EXPERT_BLOCK).astype(jnp.int32)
        r_in_blk = jnp.arange(EXPERT_BLOCK, dtype=jnp.int32)[None, :]
        parity = (jnp.arange(-1, n_blk, dtype=jnp.int32) % 2)[:, None]
        trash = n_assign + parity * EXPERT_BLOCK + r_in_blk
        gidx = jnp.where(row_a >= 0, row_a >> 2, 0)
        sidx = jnp.where(row_a >= 0, (row_a & (TOP_K - 1)) * t + (row_a >> 2), trash[1:])
        sidx = jnp.concatenate([trash[:1], sidx], axis=0)

        y = _experts(blk_exp, n_active, gidx, sidx, h2,
                     w1[l], b1[l].reshape(N_EXPERTS, 1, 2 * D_FF),
                     w2[l], b2[l].reshape(N_EXPERTS, 1, D_MODEL), t)
        x2 = _combine(x1, y, wts, final_g.reshape(1, d))
    return x2.reshape(batch, seq, d)
```

```python
import functools

import jax
import jax.numpy as jnp
import numpy as np
from jax import lax
from jax.experimental import pallas as pl
from jax.experimental.pallas import tpu as pltpu

D_MODEL = 1024
GRID_W = 64
NA_HEADS = 8
NA_HEAD_DIM = 64
NA_ROWS = 8
NA_COLS = 16
SWA_Q_HEADS = 8
SWA_KV_HEADS = 2
SWA_HEAD_DIM = 64
SWA_WINDOW = 128
SWA_BLOCK = 128
ROPE_THETA = 10000.0
N_EXPERTS = 32
TOP_K = 4
D_FF = 1024
SWIGLU_LIMIT = 7.0
SWIGLU_ALPHA = 1.702
EXPERT_BLOCK = 256
RMS_EPS = 1e-5

NA_WIDTH = NA_HEADS * NA_HEAD_DIM
SWA_Q_WIDTH = SWA_Q_HEADS * SWA_HEAD_DIM
SWA_KV_WIDTH = SWA_KV_HEADS * SWA_HEAD_DIM
LANES = 128
SUBLANES = 8
NEG_BIG = -1e30

C_QA, C_KA, C_VA = 0, NA_WIDTH, 2 * NA_WIDTH
C_QB = 3 * NA_WIDTH
C_KB = C_QB + SWA_Q_WIDTH
C_VB = C_KB + SWA_KV_WIDTH
C_GA = C_VB + SWA_KV_WIDTH
C_GB = C_GA + D_MODEL
D_IN = C_GB + D_MODEL

TM_PROJ = 512
TM_MIX = 512
TM_OUT = 512
FF_CHUNK = 256
NA_BATCH_ROWS = 4
LOG2E = 1.4426950408889634
VMEM_LIMIT = 56 * 1024 * 1024

_BF16 = jnp.bfloat16
_F32 = jnp.float32


def _const_spec(shape):
    nd = len(shape)
    return pl.BlockSpec(shape, lambda *_: (0,) * nd, pipeline_mode=pl.Buffered(1))


def _rope_slab(y, cos, sin_signed):
    lane = lax.broadcasted_iota(jnp.int32, y.shape, 1)
    first_half = (lane & (SWA_HEAD_DIM - 1)) < (SWA_HEAD_DIM // 2)
    rot = jnp.where(first_half, pltpu.roll(y, LANES - SWA_HEAD_DIM // 2, axis=1),
                    pltpu.roll(y, SWA_HEAD_DIM // 2, axis=1))
    return y * cos + rot * sin_signed


def _inproj_kernel(x_ref, g_ref, w_ref, b_ref, cos_ref, sin_ref,
                   qkva_ref, qb_ref, kvb_ref, gate_ref):
    x = x_ref[...]
    var = jnp.mean(x * x, axis=-1, keepdims=True)
    h = (x * lax.rsqrt(var + RMS_EPS) * g_ref[...]).astype(_BF16)

    def proj(c0, c1):
        return jnp.dot(h, w_ref[:, c0:c1], preferred_element_type=_F32) + b_ref[:, c0:c1]

    scale = NA_HEAD_DIM ** -0.5 * LOG2E
    qkva_ref[:, C_QA:C_KA] = (proj(C_QA, C_KA) * scale).astype(_BF16)
    qkva_ref[:, C_KA:C_VA] = proj(C_KA, C_VA).astype(_BF16)
    qkva_ref[:, C_VA:C_QB] = proj(C_VA, C_QB).astype(_BF16)

    cos = cos_ref[...]
    sin = sin_ref[...]
    qb = proj(C_QB, C_KB)
    for s in range(SWA_Q_WIDTH // LANES):
        slab = _rope_slab(qb[:, s * LANES:(s + 1) * LANES], cos, sin)
        qb_ref[:, s * LANES:(s + 1) * LANES] = (slab * (SWA_HEAD_DIM ** -0.5 * LOG2E)).astype(_BF16)
    kvb = proj(C_KB, C_GA)
    kvb_ref[:, 0:LANES] = _rope_slab(kvb[:, 0:LANES], cos, sin).astype(_BF16)
    kvb_ref[:, LANES:2 * LANES] = kvb[:, LANES:2 * LANES].astype(_BF16)

    for c0 in range(C_GA, D_IN, 512):
        gate_ref[:, c0 - C_GA:c0 - C_GA + 512] = jax.nn.sigmoid(proj(c0, c0 + 512)).astype(_BF16)


def _inproj(x2, g1, w_in, b_in, cos_t, sin_t, seq):
    t = x2.shape[0]
    tm = TM_PROJ
    nseq = seq // tm
    row = lambda i: (i, 0)
    return pl.pallas_call(
        _inproj_kernel,
        grid=(t // tm,),
        in_specs=[
            pl.BlockSpec((tm, D_MODEL), row),
            _const_spec((1, D_MODEL)),
            _const_spec((D_MODEL, D_IN)),
            _const_spec((1, D_IN)),
            pl.BlockSpec((tm, LANES), lambda i: (i % nseq, 0)),
            pl.BlockSpec((tm, LANES), lambda i: (i % nseq, 0)),
        ],
        out_specs=[
            pl.BlockSpec((tm, 3 * NA_WIDTH), row),
            pl.BlockSpec((tm, SWA_Q_WIDTH), row),
            pl.BlockSpec((tm, 2 * SWA_KV_WIDTH), row),
            pl.BlockSpec((tm, 2 * D_MODEL), row),
        ],
        out_shape=[
            jax.ShapeDtypeStruct((t, 3 * NA_WIDTH), _BF16),
            jax.ShapeDtypeStruct((t, SWA_Q_WIDTH), _BF16),
            jax.ShapeDtypeStruct((t, 2 * SWA_KV_WIDTH), _BF16),
            jax.ShapeDtypeStruct((t, 2 * D_MODEL), _BF16),
        ],
        compiler_params=pltpu.CompilerParams(
            dimension_semantics=("parallel",), vmem_limit_bytes=VMEM_LIMIT),
        name="inproj",
    )(x2, g1, w_in, b_in, cos_t, sin_t)


_NT_DIMS = (((1,), (1,)), ((), ()))


def _mixer_kernel(sink_ref, x_ref, q_ref, kp_ref, kc_ref, kn_ref, vp_ref, vc_ref, vn_ref,
                  qb_ref, kvp_ref, kvc_ref, kvn_ref, gate_ref, tbl_ref, mask_ref,
                  wua_ref, wub_ref, wo_ref, g2_ref, wr_ref, br_ref, tri_ref,
                  x1_ref, h2_ref, ids_ref, wts_ref, rank_ref, cnt_ref,
                  kcat, vcat, kvcat, oa_s, ob_s, s_s, e_s, inv_s, *, rows, n_swa_blocks):
    j = pl.program_id(1)
    rows_per_tile = TM_MIX // GRID_W
    halo = (NA_ROWS // 2) * GRID_W
    band = NA_ROWS * GRID_W

    kcat[0:halo] = kp_ref[...]
    kcat[halo:halo + TM_MIX] = kc_ref[...]
    kcat[halo + TM_MIX:] = kn_ref[...]
    vcat[0:halo] = vp_ref[...]
    vcat[halo:halo + TM_MIX] = vc_ref[...]
    vcat[halo + TM_MIX:] = vn_ref[...]

    lane_q = lax.broadcasted_iota(jnp.int32, (GRID_W, LANES), 1)
    low_q = lane_q < NA_HEAD_DIM
    lane_o = lax.broadcasted_iota(jnp.int32, (GRID_W, LANES), 1) < NA_HEAD_DIM

    n_pairs = NA_HEADS // 2
    chain = 2 * GRID_W

    def band_start(i):
        r = j * rows_per_tile + i
        rs = jnp.clip(r - NA_ROWS // 2, 0, rows - NA_ROWS)
        start = pl.multiple_of((rs - (j * rows_per_tile - NA_ROWS // 2)) * GRID_W, GRID_W)
        return start, r - rs

    for half in range(rows_per_tile // NA_BATCH_ROWS):
        def na_scores(b, carry):
            i = half * NA_BATCH_ROWS + b
            start, d = band_start(i)
            q0 = pl.multiple_of(i * GRID_W, GRID_W)
            for p in range(n_pairs):
                cols = slice(p * LANES, (p + 1) * LANES)
                qpair = q_ref[pl.ds(q0, GRID_W), cols]
                zero = jnp.zeros_like(qpair)
                qs = jnp.concatenate([jnp.where(low_q, qpair, zero), jnp.where(low_q, zero, qpair)], axis=0)
                kb = kcat[pl.ds(start, band), cols]
                s = lax.dot_general(qs, kb, _NT_DIMS, preferred_element_type=_F32) + tbl_ref[d, p]
                s_s[pl.ds(pl.multiple_of((b * n_pairs + p) * chain, chain), chain), :] = s
            return carry

        def na_softmax(b, carry):
            rws = pl.ds(pl.multiple_of(b * n_pairs * chain, n_pairs * chain), n_pairs * chain)
            s = s_s[rws, :]
            e = jnp.exp2(s - jnp.max(s, axis=-1, keepdims=True))
            e_s[rws, :] = e.astype(_BF16)
            inv_s[rws, :] = jnp.broadcast_to(1.0 / jnp.sum(e, axis=-1, keepdims=True), (n_pairs * chain, LANES))
            return carry

        def na_values(b, carry):
            i = half * NA_BATCH_ROWS + b
            start, _ = band_start(i)
            q0 = pl.multiple_of(i * GRID_W, GRID_W)
            for p in range(n_pairs):
                cols = slice(p * LANES, (p + 1) * LANES)
                rws = pl.ds(pl.multiple_of((b * n_pairs + p) * chain, chain), chain)
                vb = vcat[pl.ds(start, band), cols]
                o = jnp.dot(e_s[rws, :], vb, preferred_element_type=_F32) * inv_s[rws, :]
                oa_s[pl.ds(q0, GRID_W), cols] = jnp.where(lane_o, o[:GRID_W], o[GRID_W:]).astype(_BF16)
            return carry

        lax.fori_loop(0, NA_BATCH_ROWS, na_scores, 0)
        lax.fori_loop(0, NA_BATCH_ROWS, na_softmax, 0)
        lax.fori_loop(0, NA_BATCH_ROWS, na_values, 0)

    kvcat[0:SWA_BLOCK] = kvp_ref[...]
    kvcat[SWA_BLOCK:SWA_BLOCK + TM_MIX] = kvc_ref[...]
    kvcat[SWA_BLOCK + TM_MIX:] = kvn_ref[...]

    n_slabs = SWA_Q_WIDTH // LANES
    stack = n_slabs * SWA_BLOCK
    lane_s = lax.broadcasted_iota(jnp.int32, (SWA_BLOCK, LANES), 1) < SWA_HEAD_DIM
    rowblk = lax.broadcasted_iota(jnp.int32, (stack, 1), 0) // SWA_BLOCK
    wband = 3 * SWA_BLOCK
    sinks = []
    for g in range(SWA_KV_HEADS):
        sink = jnp.zeros((stack, 1), _F32)
        for s_ in range(n_slabs):
            sink = jnp.where(rowblk == s_, sink_ref[SWA_KV_HEADS * s_ + g], sink)
        sinks.append(sink)

    def swa_block(n, carry):
        nb = j * (TM_MIX // SWA_BLOCK) + n
        t0 = pl.multiple_of(n * SWA_BLOCK, SWA_BLOCK)
        variant = jnp.where(nb == 0, 0, jnp.where(nb == n_swa_blocks - 1, 2, 1))
        mask = mask_ref[variant]
        mask = jnp.concatenate([mask] * n_slabs, axis=0)
        kband = kvcat[pl.ds(t0, wband), 0:LANES]
        vband = kvcat[pl.ds(t0, wband), LANES:2 * LANES]
        outs = []
        for g in range(SWA_KV_HEADS):
            parts = []
            for s_ in range(n_slabs):
                slab = qb_ref[pl.ds(t0, SWA_BLOCK), s_ * LANES:(s_ + 1) * LANES]
                zero = jnp.zeros_like(slab)
                parts.append(jnp.where(lane_s, slab, zero) if g == 0 else jnp.where(lane_s, zero, slab))
            qg = jnp.concatenate(parts, axis=0)
            s = lax.dot_general(qg, kband, _NT_DIMS, preferred_element_type=_F32) + mask
            m = jnp.maximum(jnp.max(s, axis=-1, keepdims=True), sinks[g])
            e = jnp.exp2(s - m)
            den = jnp.sum(e, axis=-1, keepdims=True) + jnp.exp2(sinks[g] - m)
            outs.append(jnp.dot(e.astype(_BF16), vband, preferred_element_type=_F32) * (1.0 / den))
        for s_ in range(n_slabs):
            rs_ = slice(s_ * SWA_BLOCK, (s_ + 1) * SWA_BLOCK)
            ob_s[pl.ds(t0, SWA_BLOCK), s_ * LANES:(s_ + 1) * LANES] = jnp.where(
                lane_s, outs[0][rs_], outs[1][rs_]).astype(_BF16)
        return carry

    lax.fori_loop(0, TM_MIX // SWA_BLOCK, swa_block, 0)

    ua = jnp.dot(oa_s[...], wua_ref[...], preferred_element_type=_F32)
    ub = jnp.dot(ob_s[...], wub_ref[...], preferred_element_type=_F32)
    merged = (gate_ref[:, 0:D_MODEL].astype(_F32) * ua
              + gate_ref[:, D_MODEL:].astype(_F32) * ub).astype(_BF16)
    x1 = x_ref[...] + jnp.dot(merged, wo_ref[...], preferred_element_type=_F32)
    x1_ref[...] = x1

    var = jnp.mean(x1 * x1, axis=-1, keepdims=True)
    h2 = x1 * lax.rsqrt(var + RMS_EPS) * g2_ref[...]
    h2_ref[...] = h2
    h2_hi = h2.astype(_BF16)
    h2_lo = (h2 - h2_hi.astype(_F32)).astype(_BF16)
    both = lax.dot_general(wr_ref[...], h2_hi, _NT_DIMS, preferred_element_type=_F32)
    cross = lax.dot_general(wr_ref[0:N_EXPERTS], h2_lo, _NT_DIMS, preferred_element_type=_F32)
    logits = both[0:N_EXPERTS] + both[N_EXPERTS:] + cross + br_ref[...]
    eidx = lax.broadcasted_iota(jnp.int32, logits.shape, 0)
    vals, idxs, hots = [], [], []
    for _ in range(TOP_K):
        m = jnp.max(logits, axis=0, keepdims=True)
        idx = jnp.min(jnp.where(logits == m, eidx, N_EXPERTS), axis=0, keepdims=True)
        hot = eidx == idx
        logits = jnp.where(hot, -jnp.inf, logits)
        vals.append(m)
        idxs.append(idx)
        hots.append(hot)
    es = [jnp.exp(v - vals[0]) for v in vals]
    inv = 1.0 / (es[0] + es[1] + es[2] + es[3])
    sel = jnp.zeros(logits.shape, _F32)
    for hot in hots:
        sel = sel + jnp.where(hot, 1.0, 0.0)
    prefix = jnp.dot(sel.astype(_BF16), tri_ref[...], preferred_element_type=_F32)
    ranks = [jnp.sum(jnp.where(hot, prefix, 0.0), axis=0, keepdims=True).astype(jnp.int32) for hot in hots]
    pad_i = jnp.zeros((8 - TOP_K, TM_MIX), jnp.int32)
    ids_ref[...] = jnp.concatenate(idxs + [pad_i], axis=0)
    rank_ref[...] = jnp.concatenate(ranks + [pad_i], axis=0)
    wts_ref[...] = jnp.concatenate([e * inv for e in es] + [jnp.zeros((8 - TOP_K, TM_MIX), _F32)], axis=0)
    cnt_ref[...] = jnp.broadcast_to(jnp.sum(sel, axis=1, keepdims=True), (N_EXPERTS, LANES))


def _swa_mask_table():
    qpos = np.arange(SWA_BLOCK)[:, None]
    koff = np.arange(3 * SWA_BLOCK)[None, :] - SWA_BLOCK
    rel_ok = np.abs(koff - qpos) <= SWA_WINDOW
    has_prev = np.array([False, True, True])[:, None, None]
    has_next = np.array([True, True, False])[:, None, None]
    ok = rel_ok[None] & ((koff >= 0)[None] | has_prev) & ((koff < SWA_BLOCK)[None] | has_next)
    return jnp.asarray(np.where(ok, 0.0, NEG_BIG), _F32)


def _mixer(sinks_perm, x2, qkva, qb, kvb, gates, tbl, wua, wub, wo, g2, wr_t, br, tri, batch, seq):
    t = x2.shape[0]
    tm = TM_MIX
    nj = seq // tm
    rows = seq // GRID_W
    hb = tm // ((NA_ROWS // 2) * GRID_W)
    sb = tm // SWA_BLOCK
    n_halo = seq // ((NA_ROWS // 2) * GRID_W)
    n_swa = seq // SWA_BLOCK
    halo = (NA_ROWS // 2) * GRID_W
    swa_mask = _swa_mask_table()
    score_rows = NA_BATCH_ROWS * NA_HEADS * GRID_W
    assert n_swa >= 2

    tile = lambda b, j, *_: (b * nj + j, 0)

    def na_spec(col, which):
        if which == 0:
            return pl.BlockSpec((tm, NA_WIDTH), lambda b, j, *_: (b * nj + j, col))
        if which < 0:
            return pl.BlockSpec((halo, NA_WIDTH),
                                lambda b, j, *_: (b * n_halo + jnp.maximum(j * hb - 1, 0), col))
        return pl.BlockSpec((halo, NA_WIDTH),
                            lambda b, j, *_: (b * n_halo + jnp.minimum(j * hb + hb, n_halo - 1), col))

    kv_prev = pl.BlockSpec((SWA_BLOCK, 2 * SWA_KV_WIDTH),
                           lambda b, j, *_: (b * n_swa + jnp.maximum(j * sb - 1, 0), 0))
    kv_next = pl.BlockSpec((SWA_BLOCK, 2 * SWA_KV_WIDTH),
                           lambda b, j, *_: (b * n_swa + jnp.minimum(j * sb + sb, n_swa - 1), 0))

    def cspec(shape):
        nd = len(shape)
        return pl.BlockSpec(shape, lambda *_: (0,) * nd, pipeline_mode=pl.Buffered(1))

    grid_spec = pltpu.PrefetchScalarGridSpec(
        num_scalar_prefetch=1,
        grid=(batch, nj),
        in_specs=[
            pl.BlockSpec((tm, D_MODEL), tile),
            na_spec(0, 0),
            na_spec(1, -1), na_spec(1, 0), na_spec(1, 1),
            na_spec(2, -1), na_spec(2, 0), na_spec(2, 1),
            pl.BlockSpec((tm, SWA_Q_WIDTH), tile),
            kv_prev, pl.BlockSpec((tm, 2 * SWA_KV_WIDTH), tile), kv_next,
            pl.BlockSpec((tm, 2 * D_MODEL), tile),
            cspec(tbl.shape), cspec(swa_mask.shape),
            cspec(wua.shape), cspec(wub.shape), cspec(wo.shape),
            cspec(g2.shape), cspec(wr_t.shape), cspec(br.shape), cspec(tri.shape),
        ],
        out_specs=[
            pl.BlockSpec((tm, D_MODEL), tile),
            pl.BlockSpec((tm, D_MODEL), tile),
            pl.BlockSpec((8, tm), lambda b, j, *_: (0, b * nj + j)),
            pl.BlockSpec((8, tm), lambda b, j, *_: (0, b * nj + j)),
            pl.BlockSpec((8, tm), lambda b, j, *_: (0, b * nj + j)),
            pl.BlockSpec((N_EXPERTS, LANES), tile),
        ],
        scratch_shapes=[
            pltpu.VMEM((tm + 2 * halo, NA_WIDTH), _BF16),
            pltpu.VMEM((tm + 2 * halo, NA_WIDTH), _BF16),
            pltpu.VMEM((tm + 2 * SWA_BLOCK, 2 * SWA_KV_WIDTH), _BF16),
            pltpu.VMEM((tm, NA_WIDTH), _BF16),
            pltpu.VMEM((tm, SWA_Q_WIDTH), _BF16),
            pltpu.VMEM((score_rows, NA_ROWS * GRID_W), _F32),
            pltpu.VMEM((score_rows, NA_ROWS * GRID_W), _BF16),
            pltpu.VMEM((score_rows, LANES), _F32),
        ],
    )
    return pl.pallas_call(
        functools.partial(_mixer_kernel, rows=rows, n_swa_blocks=n_swa),
        grid_spec=grid_spec,
        out_shape=[
            jax.ShapeDtypeStruct((t, D_MODEL), _F32),
            jax.ShapeDtypeStruct((t, D_MODEL), _F32),
            jax.ShapeDtypeStruct((8, t), jnp.int32),
            jax.ShapeDtypeStruct((8, t), _F32),
            jax.ShapeDtypeStruct((8, t), jnp.int32),
            jax.ShapeDtypeStruct((t // tm * N_EXPERTS, LANES), _F32),
        ],
        compiler_params=pltpu.CompilerParams(
            dimension_semantics=("parallel", "parallel"), vmem_limit_bytes=VMEM_LIMIT),
        name="mixer",
    )(sinks_perm, x2, qkva, qkva, qkva, qkva, qkva, qkva, qkva, qb, kvb, kvb, kvb, gates, tbl, swa_mask,
      wua, wub, wo, g2, wr_t, br, tri)


def _expert_kernel(bexp_ref, nact_ref, gidx_hbm, sidx_hbm, h2_hbm, w1_ref, b1_ref, w2_ref, b2_ref,
                   y_hbm, gs0, gs1, ss0, ss1, xbuf0, xbuf1, ybuf0, ybuf1, w1b, w2b, isem, gsem, ssem,
                   *, n_blocks, n_real_rows):
    i = pl.program_id(0)
    nact = nact_ref[0]
    eb = EXPERT_BLOCK
    gs, ss, xbuf, ybuf = (gs0, gs1), (ss0, ss1), (xbuf0, xbuf1), (ybuf0, ybuf1)
    rows = pl.ds(0, eb)

    def gidx_copy(blk, slot):
        return pltpu.make_async_copy(gidx_hbm.at[blk], gs[slot], isem.at[0, slot])

    def sidx_copy(entry, slot):
        return pltpu.make_async_copy(sidx_hbm.at[entry], ss[slot], isem.at[1, slot])

    def gather_row(slot, r, priority=0):
        pltpu.make_async_copy(h2_hbm.at[pl.ds(gs[slot][r], 1), :], xbuf[slot].at[pl.ds(r, 1), :],
                              gsem.at[slot]).start(priority=priority)

    def wait_gather(slot):
        pltpu.make_async_copy(h2_hbm.at[rows, :], xbuf[slot].at[rows, :], gsem.at[slot]).wait()

    def scatter_row(slot, r, priority=0):
        pltpu.make_async_copy(ybuf[slot].at[pl.ds(r, 1), :], y_hbm.at[pl.ds(ss[slot][r], 1), :],
                              ssem.at[slot]).start(priority=priority)

    def wait_scatter(slot):
        pltpu.make_async_copy(ybuf[slot].at[rows, :], y_hbm.at[rows, :], ssem.at[slot]).wait()

    def rolled(fn, slot):
        def body(r, c):
            fn(slot, r)
            return c
        lax.fori_loop(0, eb, body, 0)

    @pl.when(i == 0)
    def _():
        gidx_copy(0, 0).start()
        gidx_copy(0, 0).wait()
        rolled(gather_row, 0)
        gidx_copy(jnp.minimum(1, n_blocks - 1), 1).start()
        sidx_copy(0, 1).start()
        ybuf1[...] = jnp.zeros(ybuf1.shape, _F32)
        init = pltpu.make_async_copy(ybuf1.at[rows, :], y_hbm.at[pl.ds(n_real_rows, eb), :], ssem.at[0])
        init.start()
        init.wait()

    def step(cur):
        nxt = 1 - cur
        gidx_copy(jnp.minimum(i + 2, n_blocks - 1), cur).start()
        sidx_copy(i + 1, cur).start()
        gidx_copy(0, nxt).wait()
        sidx_copy(0, nxt).wait()
        wait_gather(cur)

        x = xbuf[cur][rows, :].astype(_BF16)
        n_chunks = D_FF // FF_CHUNK
        rows_per_group = 2 * eb // n_chunks
        spare = pl.ds(pl.multiple_of(eb + jnp.minimum(i, 0) * SUBLANES, SUBLANES), SUBLANES)
        y = None
        for jc in range(n_chunks):
            issue = gather_row if jc < n_chunks // 2 else scatter_row
            first = (jc % (n_chunks // 2)) * rows_per_group
            for r in range(first, first + rows_per_group):
                issue(nxt, r, r % 2)
            xbuf[nxt][spare, 0:FF_CHUNK] = jnp.zeros((SUBLANES, FF_CHUNK), _F32)
            ybuf[nxt][spare, 0:FF_CHUNK] = jnp.zeros((SUBLANES, FF_CHUNK), _F32)
            anchor = (xbuf[nxt][spare, 0:FF_CHUNK] + ybuf[nxt][spare, 0:FF_CHUNK])[0:1]
            c0 = jc * FF_CHUNK
            g = (jnp.dot(x, w1b[:, c0:c0 + FF_CHUNK], preferred_element_type=_F32)
                 + (b1_ref[:, c0:c0 + FF_CHUNK] + anchor))
            u = (jnp.dot(x, w1b[:, D_FF + c0:D_FF + c0 + FF_CHUNK], preferred_element_type=_F32)
                 + b1_ref[:, D_FF + c0:D_FF + c0 + FF_CHUNK])
            gate = jnp.minimum(g, SWIGLU_LIMIT)
            up = jnp.clip(u, -SWIGLU_LIMIT, SWIGLU_LIMIT)
            act = ((up + 1.0) * (gate * jax.nn.sigmoid(gate * SWIGLU_ALPHA))).astype(_BF16)
            part = jnp.dot(act, w2b[c0:c0 + FF_CHUNK, :], preferred_element_type=_F32)
            y = part + b2_ref[...] if y is None else y + part

        @pl.when(i >= 1)
        def _():
            wait_scatter(cur)

        ybuf[cur][rows, :] = y

        @pl.when(i == nact - 1)
        def _():
            sidx_copy(0, cur).wait()
            rolled(scatter_row, cur)
            wait_scatter(nxt)
            wait_scatter(cur)
            wait_gather(nxt)
            gidx_copy(0, cur).wait()

    new_expert = jnp.logical_or(i == 0, bexp_ref[i] != bexp_ref[jnp.maximum(i - 1, 0)])

    @pl.when(jnp.logical_and(i < nact, new_expert))
    def _():
        w1b[...] = w1_ref[...].astype(_BF16)
        w2b[...] = w2_ref[...].astype(_BF16)

    for parity in range(2):
        @pl.when(jnp.logical_and(i < nact, i % 2 == parity))
        def _():
            step(parity)


def _experts(blk_exp, n_active, gidx, sidx, h2, w1, b1, w2, b2, n_tokens):
    n_blk = gidx.shape[0]
    eb = EXPERT_BLOCK
    n_real = TOP_K * n_tokens
    wmap = lambda i, be, na: (be[i], 0, 0)
    grid_spec = pltpu.PrefetchScalarGridSpec(
        num_scalar_prefetch=2,
        grid=(n_blk,),
        in_specs=[
            pl.BlockSpec(memory_space=pl.ANY),
            pl.BlockSpec(memory_space=pl.ANY),
            pl.BlockSpec(memory_space=pl.ANY),
            pl.BlockSpec((None, D_MODEL, 2 * D_FF), wmap),
            pl.BlockSpec((None, 1, 2 * D_FF), wmap),
            pl.BlockSpec((None, D_FF, D_MODEL), wmap),
            pl.BlockSpec((None, 1, D_MODEL), wmap),
        ],
        out_specs=pl.BlockSpec(memory_space=pl.ANY),
        scratch_shapes=[
            pltpu.SMEM((eb,), jnp.int32), pltpu.SMEM((eb,), jnp.int32),
            pltpu.SMEM((eb,), jnp.int32), pltpu.SMEM((eb,), jnp.int32),
            pltpu.VMEM((eb + SUBLANES, D_MODEL), _F32),
            pltpu.VMEM((eb + SUBLANES, D_MODEL), _F32),
            pltpu.VMEM((eb + SUBLANES, D_MODEL), _F32),
            pltpu.VMEM((eb + SUBLANES, D_MODEL), _F32),
            pltpu.VMEM((D_MODEL, 2 * D_FF), _BF16),
            pltpu.VMEM((D_FF, D_MODEL), _BF16),
            pltpu.SemaphoreType.DMA((2, 2)),
            pltpu.SemaphoreType.DMA((2,)),
            pltpu.SemaphoreType.DMA((2,)),
        ],
    )
    return pl.pallas_call(
        functools.partial(_expert_kernel, n_blocks=n_blk, n_real_rows=n_real),
        grid_spec=grid_spec,
        out_shape=jax.ShapeDtypeStruct((n_real + 2 * eb, D_MODEL), _F32),
        compiler_params=pltpu.CompilerParams(
            dimension_semantics=("arbitrary",), vmem_limit_bytes=VMEM_LIMIT),
        name="experts",
    )(blk_exp, n_active, gidx, sidx, h2, w1, b1, w2, b2)


def _combine_kernel(x1_ref, y0_ref, y1_ref, y2_ref, y3_ref, wts_ref, g_ref, o_ref):
    wt = wts_ref[...].T
    y = x1_ref[...]
    for k, y_ref in enumerate((y0_ref, y1_ref, y2_ref, y3_ref)):
        y = y + wt[:, k:k + 1] * y_ref[...]
    var = jnp.mean(y * y, axis=-1, keepdims=True)
    o_ref[...] = y * lax.rsqrt(var + RMS_EPS) * g_ref[...]


def _combine(x1, y, wts, gf):
    t = x1.shape[0]
    tm = TM_OUT
    nt = t // tm
    yspec = lambda k: pl.BlockSpec((tm, D_MODEL), lambda i: (k * nt + i, 0))
    return pl.pallas_call(
        _combine_kernel,
        grid=(nt,),
        in_specs=[
            pl.BlockSpec((tm, D_MODEL), lambda i: (i, 0)),
            yspec(0), yspec(1), yspec(2), yspec(3),
            pl.BlockSpec((8, tm), lambda i: (0, i)),
            _const_spec((1, D_MODEL)),
        ],
        out_specs=pl.BlockSpec((tm, D_MODEL), lambda i: (i, 0)),
        out_shape=jax.ShapeDtypeStruct((t, D_MODEL), _F32),
        compiler_params=pltpu.CompilerParams(
            dimension_semantics=("parallel",), vmem_limit_bytes=VMEM_LIMIT),
        name="combine",
    )(x1, y, y, y, y, wts, gf)


def _split_bf16(w):
    hi = w.astype(_BF16)
    lo = (w - hi.astype(_F32)).astype(_BF16)
    return jnp.concatenate([hi, lo], axis=0)


def _rope_tables(seq):
    half = SWA_HEAD_DIM // 2
    inv_freq = ROPE_THETA ** (-jnp.arange(half, dtype=_F32) / half)
    ang = jnp.arange(seq, dtype=_F32)[:, None] * inv_freq[None, :]
    cos, sin = jnp.cos(ang), jnp.sin(ang)
    cos_t = jnp.tile(jnp.concatenate([cos, cos], axis=1), (1, LANES // SWA_HEAD_DIM))
    sin_t = jnp.tile(jnp.concatenate([-sin, sin], axis=1), (1, LANES // SWA_HEAD_DIM))
    return cos_t, sin_t


def _na_bias_table(rpb):
    col = np.arange(GRID_W)
    cstart = np.clip(col - NA_COLS // 2, 0, GRID_W - NA_COLS)
    kc = np.arange(GRID_W)
    valid = (kc[None, :] >= cstart[:, None]) & (kc[None, :] < cstart[:, None] + NA_COLS)
    off = np.clip(kc[None, :] - col[:, None] + NA_COLS - 1, 0, 2 * NA_COLS - 2)
    ext = jnp.where(valid[None, None], rpb[:, :, off], NEG_BIG)
    u = np.arange(NA_ROWS)[None, :] - np.arange(NA_ROWS)[:, None] + NA_ROWS - 1
    tbl = ext[:, u]
    tbl = tbl.transpose(1, 0, 3, 2, 4).reshape(NA_ROWS, NA_HEADS // 2, 2 * GRID_W, NA_ROWS * GRID_W)
    return (tbl * LOG2E).astype(_F32)


def kernel(x, norm1_g, w_in, b_in, na_rpb, swa_sinks, w_up_a, w_up_b, w_out, norm2_g, w_router,
           b_router, w1, b1, w2, b2, final_g):
    batch, seq, d = x.shape
    depth = w_in.shape[0]
    t = batch * seq
    assert depth == 1, "the final norm is fused into the single layer's combine step"
    assert d == D_MODEL and seq % TM_MIX == 0 and seq // GRID_W >= 2 * NA_ROWS and t % TM_OUT == 0

    group = SWA_Q_HEADS // SWA_KV_HEADS
    head_order = np.arange(SWA_Q_HEADS).reshape(SWA_KV_HEADS, group).T.reshape(-1)

    def reorder_heads(a, axis, start):
        take = lambda lo, hi: lax.slice_in_dim(a, lo, hi, axis=axis)
        heads = [take(start + h * SWA_HEAD_DIM, start + (h + 1) * SWA_HEAD_DIM) for h in head_order]
        return jnp.concatenate([take(0, start)] + heads + [take(start + SWA_Q_WIDTH, a.shape[axis])], axis=axis)

    cos_t, sin_t = _rope_tables(seq)
    tri = (np.arange(TM_MIX)[:, None] < np.arange(TM_MIX)[None, :]).astype(np.float32)
    tri = jnp.asarray(tri, _BF16)

    n_assign = t * TOP_K
    n_rows = n_assign + N_EXPERTS * EXPERT_BLOCK
    n_blk = n_rows // EXPERT_BLOCK
    n_tiles = t // TM_MIX

    x2 = x.reshape(t, d)
    for l in range(depth):
        w_in_l = reorder_heads(w_in[l], 1, C_QB).astype(_BF16)
        b_in_l = reorder_heads(b_in[l], 0, C_QB).reshape(1, D_IN)
        qkva, qb, kvb, gates = _inproj(x2, norm1_g[l].reshape(1, d), w_in_l, b_in_l, cos_t, sin_t, seq)

        x1, h2, ids, wts, rank, cnt = _mixer(
            swa_sinks[l][head_order].astype(_F32) * LOG2E, x2, qkva, qb, kvb, gates, _na_bias_table(na_rpb[l]),
            w_up_a[l].astype(_BF16), reorder_heads(w_up_b[l], 0, 0).astype(_BF16), w_out[l].astype(_BF16),
            norm2_g[l].reshape(1, d), _split_bf16(w_router[l].T), b_router[l].reshape(N_EXPERTS, 1),
            tri, batch, seq)

        cnt = cnt.reshape(n_tiles, N_EXPERTS, LANES)[:, :, 0].astype(jnp.int32)
        counts = jnp.sum(cnt, axis=0)
        padded = (counts + EXPERT_BLOCK - 1) // EXPERT_BLOCK * EXPERT_BLOCK
        pend = jnp.cumsum(padded)
        base = (pend - padded)[None, :] + jnp.cumsum(cnt, axis=0) - cnt
        hot = ids[:TOP_K].reshape(TOP_K, n_tiles, TM_MIX, 1) == jnp.arange(N_EXPERTS, dtype=jnp.int32)
        dest = jnp.sum(jnp.where(hot, base[None, :, None, :], 0), axis=-1).reshape(TOP_K, t) + rank[:TOP_K]
        assign = jnp.arange(t, dtype=jnp.int32)[None, :] * TOP_K + jnp.arange(TOP_K, dtype=jnp.int32)[:, None]
        row_a = jnp.full((n_rows,), -1, jnp.int32).at[dest.reshape(-1)].set(
            assign.reshape(-1), unique_indices=True, mode="promise_in_bounds")
        row_a = row_a.reshape(n_blk, EXPERT_BLOCK)
        blk_start = jnp.arange(n_blk, dtype=jnp.int32) * EXPERT_BLOCK
        blk_exp = jnp.minimum(jnp.sum((pend[None, :] <= blk_start[:, None]).astype(jnp.int32), axis=1),
                              N_EXPERTS - 1)
        n_active = (pend[-1:] // EXPERT_BLOCK).astype(jnp.int32)
        r_in_blk = jnp.arange(EXPERT_BLOCK, dtype=jnp.int32)[None, :]
        parity = (jnp.arange(-1, n_blk, dtype=jnp.int32) % 2)[:, None]
        trash = n_assign + parity * EXPERT_BLOCK + r_in_blk
        gidx = jnp.where(row_a >= 0, row_a >> 2, 0)
        sidx = jnp.where(row_a >= 0, (row_a & (TOP_K - 1)) * t + (row_a >> 2), trash[1:])
        sidx = jnp.concatenate([trash[:1], sidx], axis=0)

        y = _experts(blk_exp, n_active, gidx, sidx, h2,
                     w1[l], b1[l].reshape(N_EXPERTS, 1, 2 * D_FF),
                     w2[l], b2[l].reshape(N_EXPERTS, 1, D_MODEL), t)
        x2 = _combine(x1, y, wts, final_g.reshape(1, d))
    return x2.reshape(batch, seq, d)
```

```python
import functools

import jax
import jax.numpy as jnp
import numpy as np
from jax import lax
from jax.experimental import pallas as pl
from jax.experimental.pallas import tpu as pltpu

D_MODEL = 1024
GRID_W = 64
NA_HEADS = 8
NA_HEAD_DIM = 64
NA_ROWS = 8
NA_COLS = 16
SWA_Q_HEADS = 8
SWA_KV_HEADS = 2
SWA_HEAD_DIM = 64
SWA_WINDOW = 128
SWA_BLOCK = 128
ROPE_THETA = 10000.0
N_EXPERTS = 32
TOP_K = 4
D_FF = 1024
SWIGLU_LIMIT = 7.0
SWIGLU_ALPHA = 1.702
EXPERT_BLOCK = 256
RMS_EPS = 1e-5

NA_WIDTH = NA_HEADS * NA_HEAD_DIM
SWA_Q_WIDTH = SWA_Q_HEADS * SWA_HEAD_DIM
SWA_KV_WIDTH = SWA_KV_HEADS * SWA_HEAD_DIM
LANES = 128
SUBLANES = 8
NEG_BIG = -1e30

C_QA, C_KA, C_VA = 0, NA_WIDTH, 2 * NA_WIDTH
C_QB = 3 * NA_WIDTH
C_KB = C_QB + SWA_Q_WIDTH
C_VB = C_KB + SWA_KV_WIDTH
C_GA = C_VB + SWA_KV_WIDTH
C_GB = C_GA + D_MODEL
D_IN = C_GB + D_MODEL

TM_PROJ = 512
TM_MIX = 512
TM_OUT = 512
FF_CHUNK = 256
NA_BATCH_ROWS = 4
LOG2E = 1.4426950408889634
KEY_SHIFT = 18
PAD_FLAG = 1 << (KEY_SHIFT - 1)
VMEM_LIMIT = 56 * 1024 * 1024

_BF16 = jnp.bfloat16
_F32 = jnp.float32


def _const_spec(shape):
    nd = len(shape)
    return pl.BlockSpec(shape, lambda *_: (0,) * nd, pipeline_mode=pl.Buffered(1))


def _rope_slab(y, cos, sin_signed):
    lane = lax.broadcasted_iota(jnp.int32, y.shape, 1)
    first_half = (lane & (SWA_HEAD_DIM - 1)) < (SWA_HEAD_DIM // 2)
    rot = jnp.where(first_half, pltpu.roll(y, LANES - SWA_HEAD_DIM // 2, axis=1),
                    pltpu.roll(y, SWA_HEAD_DIM // 2, axis=1))
    return y * cos + rot * sin_signed


def _inproj_kernel(x_ref, g_ref, w_ref, b_ref, cos_ref, sin_ref,
                   qkva_ref, qb_ref, kvb_ref, gate_ref):
    x = x_ref[...]
    var = jnp.mean(x * x, axis=-1, keepdims=True)
    h = (x * lax.rsqrt(var + RMS_EPS) * g_ref[...]).astype(_BF16)

    def proj(c0, c1):
        return jnp.dot(h, w_ref[:, c0:c1], preferred_element_type=_F32) + b_ref[:, c0:c1]

    scale = NA_HEAD_DIM ** -0.5 * LOG2E
    qkva_ref[:, C_QA:C_KA] = (proj(C_QA, C_KA) * scale).astype(_BF16)
    qkva_ref[:, C_KA:C_VA] = proj(C_KA, C_VA).astype(_BF16)
    qkva_ref[:, C_VA:C_QB] = proj(C_VA, C_QB).astype(_BF16)

    cos = cos_ref[...]
    sin = sin_ref[...]
    qb = proj(C_QB, C_KB)
    for s in range(SWA_Q_WIDTH // LANES):
        slab = _rope_slab(qb[:, s * LANES:(s + 1) * LANES], cos, sin)
        qb_ref[:, s * LANES:(s + 1) * LANES] = (slab * (SWA_HEAD_DIM ** -0.5 * LOG2E)).astype(_BF16)
    kvb = proj(C_KB, C_GA)
    kvb_ref[:, 0:LANES] = _rope_slab(kvb[:, 0:LANES], cos, sin).astype(_BF16)
    kvb_ref[:, LANES:2 * LANES] = kvb[:, LANES:2 * LANES].astype(_BF16)

    for c0 in range(C_GA, D_IN, 512):
        gate_ref[:, c0 - C_GA:c0 - C_GA + 512] = jax.nn.sigmoid(proj(c0, c0 + 512)).astype(_BF16)


def _inproj(x2, g1, w_in, b_in, cos_t, sin_t, seq):
    t = x2.shape[0]
    tm = TM_PROJ
    nseq = seq // tm
    row = lambda i: (i, 0)
    return pl.pallas_call(
        _inproj_kernel,
        grid=(t // tm,),
        in_specs=[
            pl.BlockSpec((tm, D_MODEL), row),
            _const_spec((1, D_MODEL)),
            _const_spec((D_MODEL, D_IN)),
            _const_spec((1, D_IN)),
            pl.BlockSpec((tm, LANES), lambda i: (i % nseq, 0)),
            pl.BlockSpec((tm, LANES), lambda i: (i % nseq, 0)),
        ],
        out_specs=[
            pl.BlockSpec((tm, 3 * NA_WIDTH), row),
            pl.BlockSpec((tm, SWA_Q_WIDTH), row),
            pl.BlockSpec((tm, 2 * SWA_KV_WIDTH), row),
            pl.BlockSpec((tm, 2 * D_MODEL), row),
        ],
        out_shape=[
            jax.ShapeDtypeStruct((t, 3 * NA_WIDTH), _BF16),
            jax.ShapeDtypeStruct((t, SWA_Q_WIDTH), _BF16),
            jax.ShapeDtypeStruct((t, 2 * SWA_KV_WIDTH), _BF16),
            jax.ShapeDtypeStruct((t, 2 * D_MODEL), _BF16),
        ],
        compiler_params=pltpu.CompilerParams(
            dimension_semantics=("parallel",), vmem_limit_bytes=VMEM_LIMIT),
        name="inproj",
    )(x2, g1, w_in, b_in, cos_t, sin_t)


_NT_DIMS = (((1,), (1,)), ((), ()))


def _mixer_kernel(sink_ref, x_ref, q_ref, kp_ref, kc_ref, kn_ref, vp_ref, vc_ref, vn_ref,
                  qb_ref, kvp_ref, kvc_ref, kvn_ref, gate_ref, tbl_ref, mask_ref,
                  wua_ref, wub_ref, wo_ref, g2_ref, wr_ref, br_ref,
                  x1_ref, h2_ref, key_ref, wts_ref, cnt_ref,
                  kcat, vcat, kvcat, oa_s, ob_s, s_s, e_s, inv_s, *, rows, n_swa_blocks):
    j = pl.program_id(1)
    rows_per_tile = TM_MIX // GRID_W
    halo = (NA_ROWS // 2) * GRID_W
    band = NA_ROWS * GRID_W

    kcat[0:halo] = kp_ref[...]
    kcat[halo:halo + TM_MIX] = kc_ref[...]
    kcat[halo + TM_MIX:] = kn_ref[...]
    vcat[0:halo] = vp_ref[...]
    vcat[halo:halo + TM_MIX] = vc_ref[...]
    vcat[halo + TM_MIX:] = vn_ref[...]

    lane_q = lax.broadcasted_iota(jnp.int32, (GRID_W, LANES), 1)
    low_q = lane_q < NA_HEAD_DIM
    lane_o = lax.broadcasted_iota(jnp.int32, (GRID_W, LANES), 1) < NA_HEAD_DIM

    n_pairs = NA_HEADS // 2
    chain = 2 * GRID_W

    def band_start(i):
        r = j * rows_per_tile + i
        rs = jnp.clip(r - NA_ROWS // 2, 0, rows - NA_ROWS)
        start = pl.multiple_of((rs - (j * rows_per_tile - NA_ROWS // 2)) * GRID_W, GRID_W)
        return start, r - rs

    for half in range(rows_per_tile // NA_BATCH_ROWS):
        def na_scores(b, carry):
            i = half * NA_BATCH_ROWS + b
            start, d = band_start(i)
            q0 = pl.multiple_of(i * GRID_W, GRID_W)
            for p in range(n_pairs):
                cols = slice(p * LANES, (p + 1) * LANES)
                qpair = q_ref[pl.ds(q0, GRID_W), cols]
                zero = jnp.zeros_like(qpair)
                qs = jnp.concatenate([jnp.where(low_q, qpair, zero), jnp.where(low_q, zero, qpair)], axis=0)
                kb = kcat[pl.ds(start, band), cols]
                s = lax.dot_general(qs, kb, _NT_DIMS, preferred_element_type=_F32) + tbl_ref[d, p]
                s_s[pl.ds(pl.multiple_of((b * n_pairs + p) * chain, chain), chain), :] = s
            return carry

        def na_softmax(b, carry):
            rws = pl.ds(pl.multiple_of(b * n_pairs * chain, n_pairs * chain), n_pairs * chain)
            s = s_s[rws, :]
            e = jnp.exp2(s - jnp.max(s, axis=-1, keepdims=True))
            e_s[rws, :] = e.astype(_BF16)
            inv_s[rws, :] = jnp.broadcast_to(1.0 / jnp.sum(e, axis=-1, keepdims=True), (n_pairs * chain, LANES))
            return carry

        def na_values(b, carry):
            i = half * NA_BATCH_ROWS + b
            start, _ = band_start(i)
            q0 = pl.multiple_of(i * GRID_W, GRID_W)
            for p in range(n_pairs):
                cols = slice(p * LANES, (p + 1) * LANES)
                rws = pl.ds(pl.multiple_of((b * n_pairs + p) * chain, chain), chain)
                vb = vcat[pl.ds(start, band), cols]
                o = jnp.dot(e_s[rws, :], vb, preferred_element_type=_F32) * inv_s[rws, :]
                oa_s[pl.ds(q0, GRID_W), cols] = jnp.where(lane_o, o[:GRID_W], o[GRID_W:]).astype(_BF16)
            return carry

        lax.fori_loop(0, NA_BATCH_ROWS, na_scores, 0)
        lax.fori_loop(0, NA_BATCH_ROWS, na_softmax, 0)
        lax.fori_loop(0, NA_BATCH_ROWS, na_values, 0)

    kvcat[0:SWA_BLOCK] = kvp_ref[...]
    kvcat[SWA_BLOCK:SWA_BLOCK + TM_MIX] = kvc_ref[...]
    kvcat[SWA_BLOCK + TM_MIX:] = kvn_ref[...]

    n_slabs = SWA_Q_WIDTH // LANES
    stack = n_slabs * SWA_BLOCK
    lane_s = lax.broadcasted_iota(jnp.int32, (SWA_BLOCK, LANES), 1) < SWA_HEAD_DIM
    rowblk = lax.broadcasted_iota(jnp.int32, (stack, 1), 0) // SWA_BLOCK
    wband = 3 * SWA_BLOCK
    sinks = []
    for g in range(SWA_KV_HEADS):
        sink = jnp.zeros((stack, 1), _F32)
        for s_ in range(n_slabs):
            sink = jnp.where(rowblk == s_, sink_ref[SWA_KV_HEADS * s_ + g], sink)
        sinks.append(sink)

    def swa_block(n, carry):
        nb = j * (TM_MIX // SWA_BLOCK) + n
        t0 = pl.multiple_of(n * SWA_BLOCK, SWA_BLOCK)
        variant = jnp.where(nb == 0, 0, jnp.where(nb == n_swa_blocks - 1, 2, 1))
        mask = mask_ref[variant]
        mask = jnp.concatenate([mask] * n_slabs, axis=0)
        kband = kvcat[pl.ds(t0, wband), 0:LANES]
        vband = kvcat[pl.ds(t0, wband), LANES:2 * LANES]
        outs = []
        for g in range(SWA_KV_HEADS):
            parts = []
            for s_ in range(n_slabs):
                slab = qb_ref[pl.ds(t0, SWA_BLOCK), s_ * LANES:(s_ + 1) * LANES]
                zero = jnp.zeros_like(slab)
                parts.append(jnp.where(lane_s, slab, zero) if g == 0 else jnp.where(lane_s, zero, slab))
            qg = jnp.concatenate(parts, axis=0)
            s = lax.dot_general(qg, kband, _NT_DIMS, preferred_element_type=_F32) + mask
            m = jnp.maximum(jnp.max(s, axis=-1, keepdims=True), sinks[g])
            e = jnp.exp2(s - m)
            den = jnp.sum(e, axis=-1, keepdims=True) + jnp.exp2(sinks[g] - m)
            outs.append(jnp.dot(e.astype(_BF16), vband, preferred_element_type=_F32) * (1.0 / den))
        for s_ in range(n_slabs):
            rs_ = slice(s_ * SWA_BLOCK, (s_ + 1) * SWA_BLOCK)
            ob_s[pl.ds(t0, SWA_BLOCK), s_ * LANES:(s_ + 1) * LANES] = jnp.where(
                lane_s, outs[0][rs_], outs[1][rs_]).astype(_BF16)
        return carry

    lax.fori_loop(0, TM_MIX // SWA_BLOCK, swa_block, 0)

    ua = jnp.dot(oa_s[...], wua_ref[...], preferred_element_type=_F32)
    ub = jnp.dot(ob_s[...], wub_ref[...], preferred_element_type=_F32)
    merged = (gate_ref[:, 0:D_MODEL].astype(_F32) * ua
              + gate_ref[:, D_MODEL:].astype(_F32) * ub).astype(_BF16)
    x1 = x_ref[...] + jnp.dot(merged, wo_ref[...], preferred_element_type=_F32)
    x1_ref[...] = x1

    var = jnp.mean(x1 * x1, axis=-1, keepdims=True)
    h2 = x1 * lax.rsqrt(var + RMS_EPS) * g2_ref[...]
    h2_ref[...] = h2
    h2_hi = h2.astype(_BF16)
    h2_lo = (h2 - h2_hi.astype(_F32)).astype(_BF16)
    both = lax.dot_general(wr_ref[...], h2_hi, _NT_DIMS, preferred_element_type=_F32)
    cross = lax.dot_general(wr_ref[0:N_EXPERTS], h2_lo, _NT_DIMS, preferred_element_type=_F32)
    logits = both[0:N_EXPERTS] + both[N_EXPERTS:] + cross + br_ref[...]
    eidx = lax.broadcasted_iota(jnp.int32, logits.shape, 0)
    vals, idxs, hots = [], [], []
    for _ in range(TOP_K):
        m = jnp.max(logits, axis=0, keepdims=True)
        idx = jnp.min(jnp.where(logits == m, eidx, N_EXPERTS), axis=0, keepdims=True)
        hot = eidx == idx
        logits = jnp.where(hot, -jnp.inf, logits)
        vals.append(m)
        idxs.append(idx)
        hots.append(hot)
    es = [jnp.exp(v - vals[0]) for v in vals]
    inv = 1.0 / (es[0] + es[1] + es[2] + es[3])
    sel = jnp.zeros(logits.shape, _F32)
    for hot in hots:
        sel = sel + jnp.where(hot, 1.0, 0.0)
    tile_idx = pl.program_id(0) * pl.num_programs(1) + j
    tok = tile_idx * TM_MIX + lax.broadcasted_iota(jnp.int32, (1, TM_MIX), 1)
    keys = [(idx << KEY_SHIFT) | (tok * TOP_K + k) for k, idx in enumerate(idxs)]
    key_ref[...] = jnp.concatenate(keys + [jnp.zeros((8 - TOP_K, TM_MIX), jnp.int32)], axis=0)
    wts_ref[...] = jnp.concatenate([e * inv for e in es] + [jnp.zeros((8 - TOP_K, TM_MIX), _F32)], axis=0)
    cnt_ref[...] = jnp.broadcast_to(jnp.sum(sel, axis=1, keepdims=True), (N_EXPERTS, LANES))


def _swa_mask_table():
    qpos = np.arange(SWA_BLOCK)[:, None]
    koff = np.arange(3 * SWA_BLOCK)[None, :] - SWA_BLOCK
    rel_ok = np.abs(koff - qpos) <= SWA_WINDOW
    has_prev = np.array([False, True, True])[:, None, None]
    has_next = np.array([True, True, False])[:, None, None]
    ok = rel_ok[None] & ((koff >= 0)[None] | has_prev) & ((koff < SWA_BLOCK)[None] | has_next)
    return jnp.asarray(np.where(ok, 0.0, NEG_BIG), _F32)


def _mixer(sinks_perm, x2, qkva, qb, kvb, gates, tbl, wua, wub, wo, g2, wr_t, br, batch, seq):
    t = x2.shape[0]
    tm = TM_MIX
    nj = seq // tm
    rows = seq // GRID_W
    hb = tm // ((NA_ROWS // 2) * GRID_W)
    sb = tm // SWA_BLOCK
    n_halo = seq // ((NA_ROWS // 2) * GRID_W)
    n_swa = seq // SWA_BLOCK
    halo = (NA_ROWS // 2) * GRID_W
    swa_mask = _swa_mask_table()
    score_rows = NA_BATCH_ROWS * NA_HEADS * GRID_W
    assert n_swa >= 2

    tile = lambda b, j, *_: (b * nj + j, 0)

    def na_spec(col, which):
        if which == 0:
            return pl.BlockSpec((tm, NA_WIDTH), lambda b, j, *_: (b * nj + j, col))
        if which < 0:
            return pl.BlockSpec((halo, NA_WIDTH),
                                lambda b, j, *_: (b * n_halo + jnp.maximum(j * hb - 1, 0), col))
        return pl.BlockSpec((halo, NA_WIDTH),
                            lambda b, j, *_: (b * n_halo + jnp.minimum(j * hb + hb, n_halo - 1), col))

    kv_prev = pl.BlockSpec((SWA_BLOCK, 2 * SWA_KV_WIDTH),
                           lambda b, j, *_: (b * n_swa + jnp.maximum(j * sb - 1, 0), 0))
    kv_next = pl.BlockSpec((SWA_BLOCK, 2 * SWA_KV_WIDTH),
                           lambda b, j, *_: (b * n_swa + jnp.minimum(j * sb + sb, n_swa - 1), 0))

    def cspec(shape):
        nd = len(shape)
        return pl.BlockSpec(shape, lambda *_: (0,) * nd, pipeline_mode=pl.Buffered(1))

    grid_spec = pltpu.PrefetchScalarGridSpec(
        num_scalar_prefetch=1,
        grid=(batch, nj),
        in_specs=[
            pl.BlockSpec((tm, D_MODEL), tile),
            na_spec(0, 0),
            na_spec(1, -1), na_spec(1, 0), na_spec(1, 1),
            na_spec(2, -1), na_spec(2, 0), na_spec(2, 1),
            pl.BlockSpec((tm, SWA_Q_WIDTH), tile),
            kv_prev, pl.BlockSpec((tm, 2 * SWA_KV_WIDTH), tile), kv_next,
            pl.BlockSpec((tm, 2 * D_MODEL), tile),
            cspec(tbl.shape), cspec(swa_mask.shape),
            cspec(wua.shape), cspec(wub.shape), cspec(wo.shape),
            cspec(g2.shape), cspec(wr_t.shape), cspec(br.shape),
        ],
        out_specs=[
            pl.BlockSpec((tm, D_MODEL), tile),
            pl.BlockSpec((tm, D_MODEL), tile),
            pl.BlockSpec((8, tm), lambda b, j, *_: (0, b * nj + j)),
            pl.BlockSpec((8, tm), lambda b, j, *_: (0, b * nj + j)),
            pl.BlockSpec((N_EXPERTS, LANES), tile),
        ],
        scratch_shapes=[
            pltpu.VMEM((tm + 2 * halo, NA_WIDTH), _BF16),
            pltpu.VMEM((tm + 2 * halo, NA_WIDTH), _BF16),
            pltpu.VMEM((tm + 2 * SWA_BLOCK, 2 * SWA_KV_WIDTH), _BF16),
            pltpu.VMEM((tm, NA_WIDTH), _BF16),
            pltpu.VMEM((tm, SWA_Q_WIDTH), _BF16),
            pltpu.VMEM((score_rows, NA_ROWS * GRID_W), _F32),
            pltpu.VMEM((score_rows, NA_ROWS * GRID_W), _BF16),
            pltpu.VMEM((score_rows, LANES), _F32),
        ],
    )
    return pl.pallas_call(
        functools.partial(_mixer_kernel, rows=rows, n_swa_blocks=n_swa),
        grid_spec=grid_spec,
        out_shape=[
            jax.ShapeDtypeStruct((t, D_MODEL), _F32),
            jax.ShapeDtypeStruct((t, D_MODEL), _F32),
            jax.ShapeDtypeStruct((8, t), jnp.int32),
            jax.ShapeDtypeStruct((8, t), _F32),
            jax.ShapeDtypeStruct((t // tm * N_EXPERTS, LANES), _F32),
        ],
        compiler_params=pltpu.CompilerParams(
            dimension_semantics=("parallel", "parallel"), vmem_limit_bytes=VMEM_LIMIT),
        name="mixer",
    )(sinks_perm, x2, qkva, qkva, qkva, qkva, qkva, qkva, qkva, qb, kvb, kvb, kvb, gates, tbl, swa_mask,
      wua, wub, wo, g2, wr_t, br)


def _expert_kernel(bexp_ref, nact_ref, gidx_hbm, sidx_hbm, h2_hbm, w1_ref, b1_ref, w2_ref, b2_ref,
                   y_hbm, gs0, gs1, ss0, ss1, xbuf0, xbuf1, ybuf0, ybuf1, w1b, w2b, isem, gsem, ssem,
                   *, n_blocks, n_real_rows):
    i = pl.program_id(0)
    nact = nact_ref[0]
    eb = EXPERT_BLOCK
    gs, ss, xbuf, ybuf = (gs0, gs1), (ss0, ss1), (xbuf0, xbuf1), (ybuf0, ybuf1)
    rows = pl.ds(0, eb)

    def gidx_copy(blk, slot):
        return pltpu.make_async_copy(gidx_hbm.at[blk], gs[slot], isem.at[0, slot])

    def sidx_copy(entry, slot):
        return pltpu.make_async_copy(sidx_hbm.at[entry], ss[slot], isem.at[1, slot])

    def gather_row(slot, r, priority=0):
        pltpu.make_async_copy(h2_hbm.at[pl.ds(gs[slot][r], 1), :], xbuf[slot].at[pl.ds(r, 1), :],
                              gsem.at[slot]).start(priority=priority)

    def wait_gather(slot):
        pltpu.make_async_copy(h2_hbm.at[rows, :], xbuf[slot].at[rows, :], gsem.at[slot]).wait()

    def scatter_row(slot, r, priority=0):
        pltpu.make_async_copy(ybuf[slot].at[pl.ds(r, 1), :], y_hbm.at[pl.ds(ss[slot][r], 1), :],
                              ssem.at[slot]).start(priority=priority)

    def wait_scatter(slot):
        pltpu.make_async_copy(ybuf[slot].at[rows, :], y_hbm.at[rows, :], ssem.at[slot]).wait()

    def rolled(fn, slot):
        def body(r, c):
            fn(slot, r)
            return c
        lax.fori_loop(0, eb, body, 0)

    @pl.when(i == 0)
    def _():
        gidx_copy(0, 0).start()
        gidx_copy(0, 0).wait()
        rolled(gather_row, 0)
        gidx_copy(jnp.minimum(1, n_blocks - 1), 1).start()
        sidx_copy(0, 1).start()
        ybuf1[...] = jnp.zeros(ybuf1.shape, _F32)
        init = pltpu.make_async_copy(ybuf1.at[rows, :], y_hbm.at[pl.ds(n_real_rows, eb), :], ssem.at[0])
        init.start()
        init.wait()

    def step(cur):
        nxt = 1 - cur
        gidx_copy(jnp.minimum(i + 2, n_blocks - 1), cur).start()
        sidx_copy(i + 1, cur).start()
        gidx_copy(0, nxt).wait()
        sidx_copy(0, nxt).wait()
        wait_gather(cur)

        x = xbuf[cur][rows, :].astype(_BF16)
        n_chunks = D_FF // FF_CHUNK
        rows_per_group = 2 * eb // n_chunks
        spare = pl.ds(pl.multiple_of(eb + jnp.minimum(i, 0) * SUBLANES, SUBLANES), SUBLANES)
        y = None
        for jc in range(n_chunks):
            issue = gather_row if jc < n_chunks // 2 else scatter_row
            first = (jc % (n_chunks // 2)) * rows_per_group
            for r in range(first, first + rows_per_group):
                issue(nxt, r, r % 2)
            xbuf[nxt][spare, 0:FF_CHUNK] = jnp.zeros((SUBLANES, FF_CHUNK), _F32)
            ybuf[nxt][spare, 0:FF_CHUNK] = jnp.zeros((SUBLANES, FF_CHUNK), _F32)
            anchor = (xbuf[nxt][spare, 0:FF_CHUNK] + ybuf[nxt][spare, 0:FF_CHUNK])[0:1]
            c0 = jc * FF_CHUNK
            g = (jnp.dot(x, w1b[:, c0:c0 + FF_CHUNK], preferred_element_type=_F32)
                 + (b1_ref[:, c0:c0 + FF_CHUNK] + anchor))
            u = (jnp.dot(x, w1b[:, D_FF + c0:D_FF + c0 + FF_CHUNK], preferred_element_type=_F32)
                 + b1_ref[:, D_FF + c0:D_FF + c0 + FF_CHUNK])
            gate = jnp.minimum(g, SWIGLU_LIMIT)
            up = jnp.clip(u, -SWIGLU_LIMIT, SWIGLU_LIMIT)
            act = ((up + 1.0) * (gate * jax.nn.sigmoid(gate * SWIGLU_ALPHA))).astype(_BF16)
            part = jnp.dot(act, w2b[c0:c0 + FF_CHUNK, :], preferred_element_type=_F32)
            y = part + b2_ref[...] if y is None else y + part

        @pl.when(i >= 1)
        def _():
            wait_scatter(cur)

        ybuf[cur][rows, :] = y

        @pl.when(i == nact - 1)
        def _():
            sidx_copy(0, cur).wait()
            rolled(scatter_row, cur)
            wait_scatter(nxt)
            wait_scatter(cur)
            wait_gather(nxt)
            gidx_copy(0, cur).wait()

    new_expert = jnp.logical_or(i == 0, bexp_ref[i] != bexp_ref[jnp.maximum(i - 1, 0)])

    @pl.when(jnp.logical_and(i < nact, new_expert))
    def _():
        w1b[...] = w1_ref[...].astype(_BF16)
        w2b[...] = w2_ref[...].astype(_BF16)

    for parity in range(2):
        @pl.when(jnp.logical_and(i < nact, i % 2 == parity))
        def _():
            step(parity)


def _experts(blk_exp, n_active, gidx, sidx, h2, w1, b1, w2, b2, n_tokens):
    n_blk = gidx.shape[0]
    eb = EXPERT_BLOCK
    n_real = TOP_K * n_tokens
    wmap = lambda i, be, na: (be[i], 0, 0)
    grid_spec = pltpu.PrefetchScalarGridSpec(
        num_scalar_prefetch=2,
        grid=(n_blk,),
        in_specs=[
            pl.BlockSpec(memory_space=pl.ANY),
            pl.BlockSpec(memory_space=pl.ANY),
            pl.BlockSpec(memory_space=pl.ANY),
            pl.BlockSpec((None, D_MODEL, 2 * D_FF), wmap),
            pl.BlockSpec((None, 1, 2 * D_FF), wmap),
            pl.BlockSpec((None, D_FF, D_MODEL), wmap),
            pl.BlockSpec((None, 1, D_MODEL), wmap),
        ],
        out_specs=pl.BlockSpec(memory_space=pl.ANY),
        scratch_shapes=[
            pltpu.SMEM((eb,), jnp.int32), pltpu.SMEM((eb,), jnp.int32),
            pltpu.SMEM((eb,), jnp.int32), pltpu.SMEM((eb,), jnp.int32),
            pltpu.VMEM((eb + SUBLANES, D_MODEL), _F32),
            pltpu.VMEM((eb + SUBLANES, D_MODEL), _F32),
            pltpu.VMEM((eb + SUBLANES, D_MODEL), _F32),
            pltpu.VMEM((eb + SUBLANES, D_MODEL), _F32),
            pltpu.VMEM((D_MODEL, 2 * D_FF), _BF16),
            pltpu.VMEM((D_FF, D_MODEL), _BF16),
            pltpu.SemaphoreType.DMA((2, 2)),
            pltpu.SemaphoreType.DMA((2,)),
            pltpu.SemaphoreType.DMA((2,)),
        ],
    )
    return pl.pallas_call(
        functools.partial(_expert_kernel, n_blocks=n_blk, n_real_rows=n_real),
        grid_spec=grid_spec,
        out_shape=jax.ShapeDtypeStruct((n_real + 2 * eb, D_MODEL), _F32),
        compiler_params=pltpu.CompilerParams(
            dimension_semantics=("arbitrary",), vmem_limit_bytes=VMEM_LIMIT),
        name="experts",
    )(blk_exp, n_active, gidx, sidx, h2, w1, b1, w2, b2)


def _combine_kernel(x1_ref, y0_ref, y1_ref, y2_ref, y3_ref, wts_ref, g_ref, o_ref):
    wt = wts_ref[...].T
    y = x1_ref[...]
    for k, y_ref in enumerate((y0_ref, y1_ref, y2_ref, y3_ref)):
        y = y + wt[:, k:k + 1] * y_ref[...]
    var = jnp.mean(y * y, axis=-1, keepdims=True)
    o_ref[...] = y * lax.rsqrt(var + RMS_EPS) * g_ref[...]


def _combine(x1, y, wts, gf):
    t = x1.shape[0]
    tm = TM_OUT
    nt = t // tm
    yspec = lambda k: pl.BlockSpec((tm, D_MODEL), lambda i: (k * nt + i, 0))
    return pl.pallas_call(
        _combine_kernel,
        grid=(nt,),
        in_specs=[
            pl.BlockSpec((tm, D_MODEL), lambda i: (i, 0)),
            yspec(0), yspec(1), yspec(2), yspec(3),
            pl.BlockSpec((8, tm), lambda i: (0, i)),
            _const_spec((1, D_MODEL)),
        ],
        out_specs=pl.BlockSpec((tm, D_MODEL), lambda i: (i, 0)),
        out_shape=jax.ShapeDtypeStruct((t, D_MODEL), _F32),
        compiler_params=pltpu.CompilerParams(
            dimension_semantics=("parallel",), vmem_limit_bytes=VMEM_LIMIT),
        name="combine",
    )(x1, y, y, y, y, wts, gf)


def _split_bf16(w):
    hi = w.astype(_BF16)
    lo = (w - hi.astype(_F32)).astype(_BF16)
    return jnp.concatenate([hi, lo], axis=0)


def _rope_tables(seq):
    half = SWA_HEAD_DIM // 2
    inv_freq = ROPE_THETA ** (-jnp.arange(half, dtype=_F32) / half)
    ang = jnp.arange(seq, dtype=_F32)[:, None] * inv_freq[None, :]
    cos, sin = jnp.cos(ang), jnp.sin(ang)
    cos_t = jnp.tile(jnp.concatenate([cos, cos], axis=1), (1, LANES // SWA_HEAD_DIM))
    sin_t = jnp.tile(jnp.concatenate([-sin, sin], axis=1), (1, LANES // SWA_HEAD_DIM))
    return cos_t, sin_t


def _na_bias_table(rpb):
    col = np.arange(GRID_W)
    cstart = np.clip(col - NA_COLS // 2, 0, GRID_W - NA_COLS)
    kc = np.arange(GRID_W)
    valid = (kc[None, :] >= cstart[:, None]) & (kc[None, :] < cstart[:, None] + NA_COLS)
    off = np.clip(kc[None, :] - col[:, None] + NA_COLS - 1, 0, 2 * NA_COLS - 2)
    ext = jnp.where(valid[None, None], rpb[:, :, off], NEG_BIG)
    u = np.arange(NA_ROWS)[None, :] - np.arange(NA_ROWS)[:, None] + NA_ROWS - 1
    tbl = ext[:, u]
    tbl = tbl.transpose(1, 0, 3, 2, 4).reshape(NA_ROWS, NA_HEADS // 2, 2 * GRID_W, NA_ROWS * GRID_W)
    return (tbl * LOG2E).astype(_F32)


def kernel(x, norm1_g, w_in, b_in, na_rpb, swa_sinks, w_up_a, w_up_b, w_out, norm2_g, w_router,
           b_router, w1, b1, w2, b2, final_g):
    batch, seq, d = x.shape
    depth = w_in.shape[0]
    t = batch * seq
    assert depth == 1, "the final norm is fused into the single layer's combine step"
    assert d == D_MODEL and seq % TM_MIX == 0 and seq // GRID_W >= 2 * NA_ROWS and t % TM_OUT == 0

    group = SWA_Q_HEADS // SWA_KV_HEADS
    head_order = np.arange(SWA_Q_HEADS).reshape(SWA_KV_HEADS, group).T.reshape(-1)

    def reorder_heads(a, axis, start):
        take = lambda lo, hi: lax.slice_in_dim(a, lo, hi, axis=axis)
        heads = [take(start + h * SWA_HEAD_DIM, start + (h + 1) * SWA_HEAD_DIM) for h in head_order]
        return jnp.concatenate([take(0, start)] + heads + [take(start + SWA_Q_WIDTH, a.shape[axis])], axis=axis)

    cos_t, sin_t = _rope_tables(seq)

    n_assign = t * TOP_K
    assert n_assign <= PAD_FLAG
    n_rows = n_assign + N_EXPERTS * EXPERT_BLOCK
    n_blk = n_rows // EXPERT_BLOCK
    n_tiles = t // TM_MIX

    x2 = x.reshape(t, d)
    for l in range(depth):
        w_in_l = reorder_heads(w_in[l], 1, C_QB).astype(_BF16)
        b_in_l = reorder_heads(b_in[l], 0, C_QB).reshape(1, D_IN)
        qkva, qb, kvb, gates = _inproj(x2, norm1_g[l].reshape(1, d), w_in_l, b_in_l, cos_t, sin_t, seq)

        x1, h2, keys, wts, cnt = _mixer(
            swa_sinks[l][head_order].astype(_F32) * LOG2E, x2, qkva, qb, kvb, gates, _na_bias_table(na_rpb[l]),
            w_up_a[l].astype(_BF16), reorder_heads(w_up_b[l], 0, 0).astype(_BF16), w_out[l].astype(_BF16),
            norm2_g[l].reshape(1, d), _split_bf16(w_router[l].T), b_router[l].reshape(N_EXPERTS, 1),
            batch, seq)

        cnt = cnt.reshape(n_tiles, N_EXPERTS, LANES)[:, :, 0].astype(jnp.int32)
        counts = jnp.sum(cnt, axis=0)
        padded = (counts + EXPERT_BLOCK - 1) // EXPERT_BLOCK * EXPERT_BLOCK
        pend = jnp.cumsum(padded)
        pad_i = jnp.arange(EXPERT_BLOCK, dtype=jnp.int32)[None, :]
        pad_keys = jnp.where(pad_i < (padded - counts)[:, None],
                             (jnp.arange(N_EXPERTS, dtype=jnp.int32)[:, None] << KEY_SHIFT) | PAD_FLAG | pad_i,
                             jnp.iinfo(jnp.int32).max)
        sorted_keys = jnp.sort(jnp.concatenate([keys[:TOP_K].reshape(-1), pad_keys.reshape(-1)]))
        row_a = jnp.where((sorted_keys & PAD_FLAG) == 0, sorted_keys & (PAD_FLAG - 1), -1)
        row_a = row_a.reshape(n_blk, EXPERT_BLOCK)
        blk_start = jnp.arange(n_blk, dtype=jnp.int32) * EXPERT_BLOCK
        blk_exp = jnp.minimum(jnp.sum((pend[None, :] <= blk_start[:, None]).astype(jnp.int32), axis=1),
                              N_EXPERTS - 1)
        n_active = (pend[-1:] // EXPERT_BLOCK).astype(jnp.int32)
        r_in_blk = jnp.arange(EXPERT_BLOCK, dtype=jnp.int32)[None, :]
        parity = (jnp.arange(-1, n_blk, dtype=jnp.int32) % 2)[:, None]
        trash = n_assign + parity * EXPERT_BLOCK + r_in_blk
        gidx = jnp.where(row_a >= 0, row_a >> 2, 0)
        sidx = jnp.where(row_a >= 0, (row_a & (TOP_K - 1)) * t + (row_a >> 2), trash[1:])
        sidx = jnp.concatenate([trash[:1], sidx], axis=0)

        y = _experts(blk_exp, n_active, gidx, sidx, h2,
                     w1[l], b1[l].reshape(N_EXPERTS, 1, 2 * D_FF),
                     w2[l], b2[l].reshape(N_EXPERTS, 1, D_MODEL), t)
        x2 = _combine(x1, y, wts, final_g.reshape(1, d))
    return x2.reshape(batch, seq, d)
```

```python
import functools

import jax
import jax.numpy as jnp
import numpy as np
from jax import lax
from jax.experimental import pallas as pl
from jax.experimental.pallas import tpu as pltpu

D_MODEL = 1024
GRID_W = 64
NA_HEADS = 8
NA_HEAD_DIM = 64
NA_ROWS = 8
NA_COLS = 16
SWA_Q_HEADS = 8
SWA_KV_HEADS = 2
SWA_HEAD_DIM = 64
SWA_WINDOW = 128
SWA_BLOCK = 128
ROPE_THETA = 10000.0
N_EXPERTS = 32
TOP_K = 4
D_FF = 1024
SWIGLU_LIMIT = 7.0
SWIGLU_ALPHA = 1.702
EXPERT_BLOCK = 256
RMS_EPS = 1e-5

NA_WIDTH = NA_HEADS * NA_HEAD_DIM
SWA_Q_WIDTH = SWA_Q_HEADS * SWA_HEAD_DIM
SWA_KV_WIDTH = SWA_KV_HEADS * SWA_HEAD_DIM
LANES = 128
SUBLANES = 8
NEG_BIG = -1e30

C_QA, C_KA, C_VA = 0, NA_WIDTH, 2 * NA_WIDTH
C_QB = 3 * NA_WIDTH
C_KB = C_QB + SWA_Q_WIDTH
C_VB = C_KB + SWA_KV_WIDTH
C_GA = C_VB + SWA_KV_WIDTH
C_GB = C_GA + D_MODEL
D_IN = C_GB + D_MODEL

TM_PROJ = 512
TM_MIX = 512
TM_OUT = 512
FF_CHUNK = 256
NA_BATCH_ROWS = 4
LOG2E = 1.4426950408889634
KEY_SHIFT = 18
PAD_FLAG = 1 << (KEY_SHIFT - 1)
VMEM_LIMIT = 56 * 1024 * 1024

_BF16 = jnp.bfloat16
_F32 = jnp.float32


def _const_spec(shape):
    nd = len(shape)
    return pl.BlockSpec(shape, lambda *_: (0,) * nd, pipeline_mode=pl.Buffered(1))


HALF = D_MODEL // 2
_HI_MASK = 0xFFFF0000


def _pack_rows(lo_bf16, hi_bf16):
    lo = lax.bitcast_convert_type(lo_bf16.astype(_F32), jnp.uint32) >> 16
    hi = lax.bitcast_convert_type(hi_bf16.astype(_F32), jnp.uint32) & jnp.uint32(_HI_MASK)
    return lo | hi


def _unpack_rows(u):
    lo = lax.bitcast_convert_type(u << 16, _F32)
    hi = lax.bitcast_convert_type(u & jnp.uint32(_HI_MASK), _F32)
    return lo, hi


def _rope_slab(y, cos, sin_signed):
    lane = lax.broadcasted_iota(jnp.int32, y.shape, 1)
    first_half = (lane & (SWA_HEAD_DIM - 1)) < (SWA_HEAD_DIM // 2)
    rot = jnp.where(first_half, pltpu.roll(y, LANES - SWA_HEAD_DIM // 2, axis=1),
                    pltpu.roll(y, SWA_HEAD_DIM // 2, axis=1))
    return y * cos + rot * sin_signed


def _inproj_kernel(x_ref, g_ref, w_ref, b_ref, cos_ref, sin_ref,
                   qkva_ref, qb_ref, kvb_ref, gate_ref):
    x = x_ref[...]
    var = jnp.mean(x * x, axis=-1, keepdims=True)
    h = (x * lax.rsqrt(var + RMS_EPS) * g_ref[...]).astype(_BF16)

    def proj(c0, c1):
        return jnp.dot(h, w_ref[:, c0:c1], preferred_element_type=_F32) + b_ref[:, c0:c1]

    scale = NA_HEAD_DIM ** -0.5 * LOG2E
    qkva_ref[:, C_QA:C_KA] = (proj(C_QA, C_KA) * scale).astype(_BF16)
    qkva_ref[:, C_KA:C_VA] = proj(C_KA, C_VA).astype(_BF16)
    qkva_ref[:, C_VA:C_QB] = proj(C_VA, C_QB).astype(_BF16)

    cos = cos_ref[...]
    sin = sin_ref[...]
    qb = proj(C_QB, C_KB)
    for s in range(SWA_Q_WIDTH // LANES):
        slab = _rope_slab(qb[:, s * LANES:(s + 1) * LANES], cos, sin)
        qb_ref[:, s * LANES:(s + 1) * LANES] = (slab * (SWA_HEAD_DIM ** -0.5 * LOG2E)).astype(_BF16)
    kvb = proj(C_KB, C_GA)
    kvb_ref[:, 0:LANES] = _rope_slab(kvb[:, 0:LANES], cos, sin).astype(_BF16)
    kvb_ref[:, LANES:2 * LANES] = kvb[:, LANES:2 * LANES].astype(_BF16)

    for c0 in range(C_GA, D_IN, 512):
        gate_ref[:, c0 - C_GA:c0 - C_GA + 512] = jax.nn.sigmoid(proj(c0, c0 + 512)).astype(_BF16)


def _inproj(x2, g1, w_in, b_in, cos_t, sin_t, seq):
    t = x2.shape[0]
    tm = TM_PROJ
    nseq = seq // tm
    row = lambda i: (i, 0)
    return pl.pallas_call(
        _inproj_kernel,
        grid=(t // tm,),
        in_specs=[
            pl.BlockSpec((tm, D_MODEL), row),
            _const_spec((1, D_MODEL)),
            _const_spec((D_MODEL, D_IN)),
            _const_spec((1, D_IN)),
            pl.BlockSpec((tm, LANES), lambda i: (i % nseq, 0)),
            pl.BlockSpec((tm, LANES), lambda i: (i % nseq, 0)),
        ],
        out_specs=[
            pl.BlockSpec((tm, 3 * NA_WIDTH), row),
            pl.BlockSpec((tm, SWA_Q_WIDTH), row),
            pl.BlockSpec((tm, 2 * SWA_KV_WIDTH), row),
            pl.BlockSpec((tm, 2 * D_MODEL), row),
        ],
        out_shape=[
            jax.ShapeDtypeStruct((t, 3 * NA_WIDTH), _BF16),
            jax.ShapeDtypeStruct((t, SWA_Q_WIDTH), _BF16),
            jax.ShapeDtypeStruct((t, 2 * SWA_KV_WIDTH), _BF16),
            jax.ShapeDtypeStruct((t, 2 * D_MODEL), _BF16),
        ],
        compiler_params=pltpu.CompilerParams(
            dimension_semantics=("parallel",), vmem_limit_bytes=VMEM_LIMIT),
        name="inproj",
    )(x2, g1, w_in, b_in, cos_t, sin_t)


_NT_DIMS = (((1,), (1,)), ((), ()))


def _mixer_kernel(sink_ref, x_ref, q_ref, kp_ref, kc_ref, kn_ref, vp_ref, vc_ref, vn_ref,
                  qb_ref, kvp_ref, kvc_ref, kvn_ref, gate_ref, tbl_ref, mask_ref,
                  wua_ref, wub_ref, wo_ref, g2_ref, wr_ref, br_ref,
                  x1_ref, h2_ref, key_ref, wts_ref, cnt_ref,
                  kcat, vcat, kvcat, oa_s, ob_s, s_s, e_s, inv_s, *, rows, n_swa_blocks):
    j = pl.program_id(1)
    rows_per_tile = TM_MIX // GRID_W
    halo = (NA_ROWS // 2) * GRID_W
    band = NA_ROWS * GRID_W

    kcat[0:halo] = kp_ref[...]
    kcat[halo:halo + TM_MIX] = kc_ref[...]
    kcat[halo + TM_MIX:] = kn_ref[...]
    vcat[0:halo] = vp_ref[...]
    vcat[halo:halo + TM_MIX] = vc_ref[...]
    vcat[halo + TM_MIX:] = vn_ref[...]

    lane_q = lax.broadcasted_iota(jnp.int32, (GRID_W, LANES), 1)
    low_q = lane_q < NA_HEAD_DIM
    lane_o = lax.broadcasted_iota(jnp.int32, (GRID_W, LANES), 1) < NA_HEAD_DIM

    n_pairs = NA_HEADS // 2
    chain = 2 * GRID_W

    def band_start(i):
        r = j * rows_per_tile + i
        rs = jnp.clip(r - NA_ROWS // 2, 0, rows - NA_ROWS)
        start = pl.multiple_of((rs - (j * rows_per_tile - NA_ROWS // 2)) * GRID_W, GRID_W)
        return start, r - rs

    for half in range(rows_per_tile // NA_BATCH_ROWS):
        def na_scores(b, carry):
            i = half * NA_BATCH_ROWS + b
            start, d = band_start(i)
            q0 = pl.multiple_of(i * GRID_W, GRID_W)
            for p in range(n_pairs):
                cols = slice(p * LANES, (p + 1) * LANES)
                qpair = q_ref[pl.ds(q0, GRID_W), cols]
                zero = jnp.zeros_like(qpair)
                qs = jnp.concatenate([jnp.where(low_q, qpair, zero), jnp.where(low_q, zero, qpair)], axis=0)
                kb = kcat[pl.ds(start, band), cols]
                s = lax.dot_general(qs, kb, _NT_DIMS, preferred_element_type=_F32) + tbl_ref[d, p]
                s_s[pl.ds(pl.multiple_of((b * n_pairs + p) * chain, chain), chain), :] = s
            return carry

        def na_softmax(b, carry):
            rws = pl.ds(pl.multiple_of(b * n_pairs * chain, n_pairs * chain), n_pairs * chain)
            s = s_s[rws, :]
            e = jnp.exp2(s - jnp.max(s, axis=-1, keepdims=True))
            e_s[rws, :] = e.astype(_BF16)
            inv_s[rws, :] = jnp.broadcast_to(1.0 / jnp.sum(e, axis=-1, keepdims=True), (n_pairs * chain, LANES))
            return carry

        def na_values(b, carry):
            i = half * NA_BATCH_ROWS + b
            start, _ = band_start(i)
            q0 = pl.multiple_of(i * GRID_W, GRID_W)
            for p in range(n_pairs):
                cols = slice(p * LANES, (p + 1) * LANES)
                rws = pl.ds(pl.multiple_of((b * n_pairs + p) * chain, chain), chain)
                vb = vcat[pl.ds(start, band), cols]
                o = jnp.dot(e_s[rws, :], vb, preferred_element_type=_F32) * inv_s[rws, :]
                oa_s[pl.ds(q0, GRID_W), cols] = jnp.where(lane_o, o[:GRID_W], o[GRID_W:]).astype(_BF16)
            return carry

        lax.fori_loop(0, NA_BATCH_ROWS, na_scores, 0)
        lax.fori_loop(0, NA_BATCH_ROWS, na_softmax, 0)
        lax.fori_loop(0, NA_BATCH_ROWS, na_values, 0)

    kvcat[0:SWA_BLOCK] = kvp_ref[...]
    kvcat[SWA_BLOCK:SWA_BLOCK + TM_MIX] = kvc_ref[...]
    kvcat[SWA_BLOCK + TM_MIX:] = kvn_ref[...]

    n_slabs = SWA_Q_WIDTH // LANES
    stack = n_slabs * SWA_BLOCK
    lane_s = lax.broadcasted_iota(jnp.int32, (SWA_BLOCK, LANES), 1) < SWA_HEAD_DIM
    rowblk = lax.broadcasted_iota(jnp.int32, (stack, 1), 0) // SWA_BLOCK
    wband = 3 * SWA_BLOCK
    sinks = []
    for g in range(SWA_KV_HEADS):
        sink = jnp.zeros((stack, 1), _F32)
        for s_ in range(n_slabs):
            sink = jnp.where(rowblk == s_, sink_ref[SWA_KV_HEADS * s_ + g], sink)
        sinks.append(sink)

    def swa_block(n, carry):
        nb = j * (TM_MIX // SWA_BLOCK) + n
        t0 = pl.multiple_of(n * SWA_BLOCK, SWA_BLOCK)
        variant = jnp.where(nb == 0, 0, jnp.where(nb == n_swa_blocks - 1, 2, 1))
        mask = mask_ref[variant]
        mask = jnp.concatenate([mask] * n_slabs, axis=0)
        kband = kvcat[pl.ds(t0, wband), 0:LANES]
        vband = kvcat[pl.ds(t0, wband), LANES:2 * LANES]
        outs = []
        for g in range(SWA_KV_HEADS):
            parts = []
            for s_ in range(n_slabs):
                slab = qb_ref[pl.ds(t0, SWA_BLOCK), s_ * LANES:(s_ + 1) * LANES]
                zero = jnp.zeros_like(slab)
                parts.append(jnp.where(lane_s, slab, zero) if g == 0 else jnp.where(lane_s, zero, slab))
            qg = jnp.concatenate(parts, axis=0)
            s = lax.dot_general(qg, kband, _NT_DIMS, preferred_element_type=_F32) + mask
            m = jnp.maximum(jnp.max(s, axis=-1, keepdims=True), sinks[g])
            e = jnp.exp2(s - m)
            den = jnp.sum(e, axis=-1, keepdims=True) + jnp.exp2(sinks[g] - m)
            outs.append(jnp.dot(e.astype(_BF16), vband, preferred_element_type=_F32) * (1.0 / den))
        for s_ in range(n_slabs):
            rs_ = slice(s_ * SWA_BLOCK, (s_ + 1) * SWA_BLOCK)
            ob_s[pl.ds(t0, SWA_BLOCK), s_ * LANES:(s_ + 1) * LANES] = jnp.where(
                lane_s, outs[0][rs_], outs[1][rs_]).astype(_BF16)
        return carry

    lax.fori_loop(0, TM_MIX // SWA_BLOCK, swa_block, 0)

    ua = jnp.dot(oa_s[...], wua_ref[...], preferred_element_type=_F32)
    ub = jnp.dot(ob_s[...], wub_ref[...], preferred_element_type=_F32)
    merged = (gate_ref[:, 0:D_MODEL].astype(_F32) * ua
              + gate_ref[:, D_MODEL:].astype(_F32) * ub).astype(_BF16)
    x1 = x_ref[...] + jnp.dot(merged, wo_ref[...], preferred_element_type=_F32)
    x1_ref[...] = x1

    var = jnp.mean(x1 * x1, axis=-1, keepdims=True)
    h2 = x1 * lax.rsqrt(var + RMS_EPS) * g2_ref[...]
    h2_hi = h2.astype(_BF16)
    h2_ref[...] = _pack_rows(h2_hi[:, :HALF], h2_hi[:, HALF:])
    h2_lo = (h2 - h2_hi.astype(_F32)).astype(_BF16)
    both = lax.dot_general(wr_ref[...], h2_hi, _NT_DIMS, preferred_element_type=_F32)
    cross = lax.dot_general(wr_ref[0:N_EXPERTS], h2_lo, _NT_DIMS, preferred_element_type=_F32)
    logits = both[0:N_EXPERTS] + both[N_EXPERTS:] + cross + br_ref[...]
    eidx = lax.broadcasted_iota(jnp.int32, logits.shape, 0)
    vals, idxs, hots = [], [], []
    for _ in range(TOP_K):
        m = jnp.max(logits, axis=0, keepdims=True)
        idx = jnp.min(jnp.where(logits == m, eidx, N_EXPERTS), axis=0, keepdims=True)
        hot = eidx == idx
        logits = jnp.where(hot, -jnp.inf, logits)
        vals.append(m)
        idxs.append(idx)
        hots.append(hot)
    es = [jnp.exp(v - vals[0]) for v in vals]
    inv = 1.0 / (es[0] + es[1] + es[2] + es[3])
    sel = jnp.zeros(logits.shape, _F32)
    for hot in hots:
        sel = sel + jnp.where(hot, 1.0, 0.0)
    tile_idx = pl.program_id(0) * pl.num_programs(1) + j
    tok = tile_idx * TM_MIX + lax.broadcasted_iota(jnp.int32, (1, TM_MIX), 1)
    keys = [(idx << KEY_SHIFT) | (tok * TOP_K + k) for k, idx in enumerate(idxs)]
    key_ref[...] = jnp.concatenate(keys + [jnp.zeros((8 - TOP_K, TM_MIX), jnp.int32)], axis=0)
    wts_ref[...] = jnp.concatenate([e * inv for e in es] + [jnp.zeros((8 - TOP_K, TM_MIX), _F32)], axis=0)
    cnt_ref[...] = jnp.broadcast_to(jnp.sum(sel, axis=1, keepdims=True), (N_EXPERTS, LANES))


def _swa_mask_table():
    qpos = np.arange(SWA_BLOCK)[:, None]
    koff = np.arange(3 * SWA_BLOCK)[None, :] - SWA_BLOCK
    rel_ok = np.abs(koff - qpos) <= SWA_WINDOW
    has_prev = np.array([False, True, True])[:, None, None]
    has_next = np.array([True, True, False])[:, None, None]
    ok = rel_ok[None] & ((koff >= 0)[None] | has_prev) & ((koff < SWA_BLOCK)[None] | has_next)
    return jnp.asarray(np.where(ok, 0.0, NEG_BIG), _F32)


def _mixer(sinks_perm, x2, qkva, qb, kvb, gates, tbl, wua, wub, wo, g2, wr_t, br, batch, seq):
    t = x2.shape[0]
    tm = TM_MIX
    nj = seq // tm
    rows = seq // GRID_W
    hb = tm // ((NA_ROWS // 2) * GRID_W)
    sb = tm // SWA_BLOCK
    n_halo = seq // ((NA_ROWS // 2) * GRID_W)
    n_swa = seq // SWA_BLOCK
    halo = (NA_ROWS // 2) * GRID_W
    swa_mask = _swa_mask_table()
    score_rows = NA_BATCH_ROWS * NA_HEADS * GRID_W
    assert n_swa >= 2

    tile = lambda b, j, *_: (b * nj + j, 0)

    def na_spec(col, which):
        if which == 0:
            return pl.BlockSpec((tm, NA_WIDTH), lambda b, j, *_: (b * nj + j, col))
        if which < 0:
            return pl.BlockSpec((halo, NA_WIDTH),
                                lambda b, j, *_: (b * n_halo + jnp.maximum(j * hb - 1, 0), col))
        return pl.BlockSpec((halo, NA_WIDTH),
                            lambda b, j, *_: (b * n_halo + jnp.minimum(j * hb + hb, n_halo - 1), col))

    kv_prev = pl.BlockSpec((SWA_BLOCK, 2 * SWA_KV_WIDTH),
                           lambda b, j, *_: (b * n_swa + jnp.maximum(j * sb - 1, 0), 0))
    kv_next = pl.BlockSpec((SWA_BLOCK, 2 * SWA_KV_WIDTH),
                           lambda b, j, *_: (b * n_swa + jnp.minimum(j * sb + sb, n_swa - 1), 0))

    def cspec(shape):
        nd = len(shape)
        return pl.BlockSpec(shape, lambda *_: (0,) * nd, pipeline_mode=pl.Buffered(1))

    grid_spec = pltpu.PrefetchScalarGridSpec(
        num_scalar_prefetch=1,
        grid=(batch, nj),
        in_specs=[
            pl.BlockSpec((tm, D_MODEL), tile),
            na_spec(0, 0),
            na_spec(1, -1), na_spec(1, 0), na_spec(1, 1),
            na_spec(2, -1), na_spec(2, 0), na_spec(2, 1),
            pl.BlockSpec((tm, SWA_Q_WIDTH), tile),
            kv_prev, pl.BlockSpec((tm, 2 * SWA_KV_WIDTH), tile), kv_next,
            pl.BlockSpec((tm, 2 * D_MODEL), tile),
            cspec(tbl.shape), cspec(swa_mask.shape),
            cspec(wua.shape), cspec(wub.shape), cspec(wo.shape),
            cspec(g2.shape), cspec(wr_t.shape), cspec(br.shape),
        ],
        out_specs=[
            pl.BlockSpec((tm, D_MODEL), tile),
            pl.BlockSpec((tm, HALF), tile),
            pl.BlockSpec((8, tm), lambda b, j, *_: (0, b * nj + j)),
            pl.BlockSpec((8, tm), lambda b, j, *_: (0, b * nj + j)),
            pl.BlockSpec((N_EXPERTS, LANES), tile),
        ],
        scratch_shapes=[
            pltpu.VMEM((tm + 2 * halo, NA_WIDTH), _BF16),
            pltpu.VMEM((tm + 2 * halo, NA_WIDTH), _BF16),
            pltpu.VMEM((tm + 2 * SWA_BLOCK, 2 * SWA_KV_WIDTH), _BF16),
            pltpu.VMEM((tm, NA_WIDTH), _BF16),
            pltpu.VMEM((tm, SWA_Q_WIDTH), _BF16),
            pltpu.VMEM((score_rows, NA_ROWS * GRID_W), _F32),
            pltpu.VMEM((score_rows, NA_ROWS * GRID_W), _BF16),
            pltpu.VMEM((score_rows, LANES), _F32),
        ],
    )
    return pl.pallas_call(
        functools.partial(_mixer_kernel, rows=rows, n_swa_blocks=n_swa),
        grid_spec=grid_spec,
        out_shape=[
            jax.ShapeDtypeStruct((t, D_MODEL), _F32),
            jax.ShapeDtypeStruct((t, HALF), jnp.uint32),
            jax.ShapeDtypeStruct((8, t), jnp.int32),
            jax.ShapeDtypeStruct((8, t), _F32),
            jax.ShapeDtypeStruct((t // tm * N_EXPERTS, LANES), _F32),
        ],
        compiler_params=pltpu.CompilerParams(
            dimension_semantics=("parallel", "parallel"), vmem_limit_bytes=VMEM_LIMIT),
        name="mixer",
    )(sinks_perm, x2, qkva, qkva, qkva, qkva, qkva, qkva, qkva, qb, kvb, kvb, kvb, gates, tbl, swa_mask,
      wua, wub, wo, g2, wr_t, br)


def _expert_kernel(bexp_ref, nact_ref, gidx_hbm, sidx_hbm, h2_hbm, w1_ref, b1_ref, w2_ref, b2_ref,
                   y_hbm, gs0, gs1, ss0, ss1, xbuf0, xbuf1, ybuf0, ybuf1, w1b, w2b, isem, gsem, ssem,
                   *, n_blocks, n_real_rows):
    i = pl.program_id(0)
    nact = nact_ref[0]
    eb = EXPERT_BLOCK
    gs, ss, xbuf, ybuf = (gs0, gs1), (ss0, ss1), (xbuf0, xbuf1), (ybuf0, ybuf1)
    rows = pl.ds(0, eb)

    def gidx_copy(blk, slot):
        return pltpu.make_async_copy(gidx_hbm.at[blk], gs[slot], isem.at[0, slot])

    def sidx_copy(entry, slot):
        return pltpu.make_async_copy(sidx_hbm.at[entry], ss[slot], isem.at[1, slot])

    def gather_row(slot, r, priority=0):
        pltpu.make_async_copy(h2_hbm.at[pl.ds(gs[slot][r], 1), :], xbuf[slot].at[pl.ds(r, 1), :],
                              gsem.at[slot]).start(priority=priority)

    def wait_gather(slot):
        pltpu.make_async_copy(h2_hbm.at[rows, :], xbuf[slot].at[rows, :], gsem.at[slot]).wait()

    def scatter_row(slot, r, priority=0):
        pltpu.make_async_copy(ybuf[slot].at[pl.ds(r, 1), :], y_hbm.at[pl.ds(ss[slot][r], 1), :],
                              ssem.at[slot]).start(priority=priority)

    def wait_scatter(slot):
        pltpu.make_async_copy(ybuf[slot].at[rows, :], y_hbm.at[rows, :], ssem.at[slot]).wait()

    def rolled(fn, slot):
        def body(r, c):
            fn(slot, r)
            return c
        lax.fori_loop(0, eb, body, 0)

    @pl.when(i == 0)
    def _():
        gidx_copy(0, 0).start()
        gidx_copy(0, 0).wait()
        rolled(gather_row, 0)
        gidx_copy(jnp.minimum(1, n_blocks - 1), 1).start()
        sidx_copy(0, 1).start()
        ybuf1[...] = jnp.zeros(ybuf1.shape, jnp.uint32)
        init = pltpu.make_async_copy(ybuf1.at[rows, :], y_hbm.at[pl.ds(n_real_rows, eb), :], ssem.at[0])
        init.start()
        init.wait()

    def step(cur):
        nxt = 1 - cur
        gidx_copy(jnp.minimum(i + 2, n_blocks - 1), cur).start()
        sidx_copy(i + 1, cur).start()
        gidx_copy(0, nxt).wait()
        sidx_copy(0, nxt).wait()
        wait_gather(cur)

        x_lo, x_hi = (half.astype(_BF16) for half in _unpack_rows(xbuf[cur][rows, :]))

        def x_dot(c0):
            return (jnp.dot(x_lo, w1b[0:HALF, c0:c0 + FF_CHUNK], preferred_element_type=_F32)
                    + jnp.dot(x_hi, w1b[HALF:, c0:c0 + FF_CHUNK], preferred_element_type=_F32))

        n_chunks = D_FF // FF_CHUNK
        rows_per_group = 2 * eb // n_chunks
        spare = pl.ds(pl.multiple_of(eb + jnp.minimum(i, 0) * SUBLANES, SUBLANES), SUBLANES)
        y = None
        for jc in range(n_chunks):
            issue = gather_row if jc < n_chunks // 2 else scatter_row
            first = (jc % (n_chunks // 2)) * rows_per_group
            for r in range(first, first + rows_per_group):
                issue(nxt, r, r % 2)
            xbuf[nxt][spare, 0:FF_CHUNK] = jnp.zeros((SUBLANES, FF_CHUNK), jnp.uint32)
            ybuf[nxt][spare, 0:FF_CHUNK] = jnp.zeros((SUBLANES, FF_CHUNK), jnp.uint32)
            anchor = lax.bitcast_convert_type(
                (xbuf[nxt][spare, 0:FF_CHUNK] | ybuf[nxt][spare, 0:FF_CHUNK])[0:1], _F32)
            c0 = jc * FF_CHUNK
            g = x_dot(c0) + (b1_ref[:, c0:c0 + FF_CHUNK] + anchor)
            u = x_dot(D_FF + c0) + b1_ref[:, D_FF + c0:D_FF + c0 + FF_CHUNK]
            gate = jnp.minimum(g, SWIGLU_LIMIT)
            up = jnp.clip(u, -SWIGLU_LIMIT, SWIGLU_LIMIT)
            act = ((up + 1.0) * (gate * jax.nn.sigmoid(gate * SWIGLU_ALPHA))).astype(_BF16)
            part = jnp.dot(act, w2b[c0:c0 + FF_CHUNK, :], preferred_element_type=_F32)
            y = part + b2_ref[...] if y is None else y + part

        @pl.when(i >= 1)
        def _():
            wait_scatter(cur)

        ybuf[cur][rows, :] = _pack_rows(y[:, :HALF].astype(_BF16), y[:, HALF:].astype(_BF16))

        @pl.when(i == nact - 1)
        def _():
            sidx_copy(0, cur).wait()
            rolled(scatter_row, cur)
            wait_scatter(nxt)
            wait_scatter(cur)
            wait_gather(nxt)
            gidx_copy(0, cur).wait()

    new_expert = jnp.logical_or(i == 0, bexp_ref[i] != bexp_ref[jnp.maximum(i - 1, 0)])

    @pl.when(jnp.logical_and(i < nact, new_expert))
    def _():
        w1b[...] = w1_ref[...].astype(_BF16)
        w2b[...] = w2_ref[...].astype(_BF16)

    for parity in range(2):
        @pl.when(jnp.logical_and(i < nact, i % 2 == parity))
        def _():
            step(parity)


def _experts(blk_exp, n_active, gidx, sidx, h2, w1, b1, w2, b2, n_tokens):
    n_blk = gidx.shape[0]
    eb = EXPERT_BLOCK
    n_real = TOP_K * n_tokens
    wmap = lambda i, be, na: (be[i], 0, 0)
    grid_spec = pltpu.PrefetchScalarGridSpec(
        num_scalar_prefetch=2,
        grid=(n_blk,),
        in_specs=[
            pl.BlockSpec(memory_space=pl.ANY),
            pl.BlockSpec(memory_space=pl.ANY),
            pl.BlockSpec(memory_space=pl.ANY),
            pl.BlockSpec((None, D_MODEL, 2 * D_FF), wmap),
            pl.BlockSpec((None, 1, 2 * D_FF), wmap),
            pl.BlockSpec((None, D_FF, D_MODEL), wmap),
            pl.BlockSpec((None, 1, D_MODEL), wmap),
        ],
        out_specs=pl.BlockSpec(memory_space=pl.ANY),
        scratch_shapes=[
            pltpu.SMEM((eb,), jnp.int32), pltpu.SMEM((eb,), jnp.int32),
            pltpu.SMEM((eb,), jnp.int32), pltpu.SMEM((eb,), jnp.int32),
            pltpu.VMEM((eb + SUBLANES, HALF), jnp.uint32),
            pltpu.VMEM((eb + SUBLANES, HALF), jnp.uint32),
            pltpu.VMEM((eb + SUBLANES, HALF), jnp.uint32),
            pltpu.VMEM((eb + SUBLANES, HALF), jnp.uint32),
            pltpu.VMEM((D_MODEL, 2 * D_FF), _BF16),
            pltpu.VMEM((D_FF, D_MODEL), _BF16),
            pltpu.SemaphoreType.DMA((2, 2)),
            pltpu.SemaphoreType.DMA((2,)),
            pltpu.SemaphoreType.DMA((2,)),
        ],
    )
    return pl.pallas_call(
        functools.partial(_expert_kernel, n_blocks=n_blk, n_real_rows=n_real),
        grid_spec=grid_spec,
        out_shape=jax.ShapeDtypeStruct((n_real + 2 * eb, HALF), jnp.uint32),
        compiler_params=pltpu.CompilerParams(
            dimension_semantics=("arbitrary",), vmem_limit_bytes=VMEM_LIMIT),
        name="experts",
    )(blk_exp, n_active, gidx, sidx, h2, w1, b1, w2, b2)


def _combine_kernel(x1_ref, y0_ref, y1_ref, y2_ref, y3_ref, wts_ref, g_ref, o_ref):
    wt = wts_ref[...].T
    y = x1_ref[...]
    for k, y_ref in enumerate((y0_ref, y1_ref, y2_ref, y3_ref)):
        y = y + wt[:, k:k + 1] * jnp.concatenate(_unpack_rows(y_ref[...]), axis=1)
    var = jnp.mean(y * y, axis=-1, keepdims=True)
    o_ref[...] = y * lax.rsqrt(var + RMS_EPS) * g_ref[...]


def _combine(x1, y, wts, gf):
    t = x1.shape[0]
    tm = TM_OUT
    nt = t // tm
    yspec = lambda k: pl.BlockSpec((tm, HALF), lambda i: (k * nt + i, 0))
    return pl.pallas_call(
        _combine_kernel,
        grid=(nt,),
        in_specs=[
            pl.BlockSpec((tm, D_MODEL), lambda i: (i, 0)),
            yspec(0), yspec(1), yspec(2), yspec(3),
            pl.BlockSpec((8, tm), lambda i: (0, i)),
            _const_spec((1, D_MODEL)),
        ],
        out_specs=pl.BlockSpec((tm, D_MODEL), lambda i: (i, 0)),
        out_shape=jax.ShapeDtypeStruct((t, D_MODEL), _F32),
        compiler_params=pltpu.CompilerParams(
            dimension_semantics=("parallel",), vmem_limit_bytes=VMEM_LIMIT),
        name="combine",
    )(x1, y, y, y, y, wts, gf)


def _split_bf16(w):
    hi = w.astype(_BF16)
    lo = (w - hi.astype(_F32)).astype(_BF16)
    return jnp.concatenate([hi, lo], axis=0)


def _rope_tables(seq):
    half = SWA_HEAD_DIM // 2
    inv_freq = ROPE_THETA ** (-jnp.arange(half, dtype=_F32) / half)
    ang = jnp.arange(seq, dtype=_F32)[:, None] * inv_freq[None, :]
    cos, sin = jnp.cos(ang), jnp.sin(ang)
    cos_t = jnp.tile(jnp.concatenate([cos, cos], axis=1), (1, LANES // SWA_HEAD_DIM))
    sin_t = jnp.tile(jnp.concatenate([-sin, sin], axis=1), (1, LANES // SWA_HEAD_DIM))
    return cos_t, sin_t


def _na_bias_table(rpb):
    col = np.arange(GRID_W)
    cstart = np.clip(col - NA_COLS // 2, 0, GRID_W - NA_COLS)
    kc = np.arange(GRID_W)
    valid = (kc[None, :] >= cstart[:, None]) & (kc[None, :] < cstart[:, None] + NA_COLS)
    off = np.clip(kc[None, :] - col[:, None] + NA_COLS - 1, 0, 2 * NA_COLS - 2)
    ext = jnp.where(valid[None, None], rpb[:, :, off], NEG_BIG)
    u = np.arange(NA_ROWS)[None, :] - np.arange(NA_ROWS)[:, None] + NA_ROWS - 1
    tbl = ext[:, u]
    tbl = tbl.transpose(1, 0, 3, 2, 4).reshape(NA_ROWS, NA_HEADS // 2, 2 * GRID_W, NA_ROWS * GRID_W)
    return (tbl * LOG2E).astype(_F32)


def kernel(x, norm1_g, w_in, b_in, na_rpb, swa_sinks, w_up_a, w_up_b, w_out, norm2_g, w_router,
           b_router, w1, b1, w2, b2, final_g):
    batch, seq, d = x.shape
    depth = w_in.shape[0]
    t = batch * seq
    assert depth == 1, "the final norm is fused into the single layer's combine step"
    assert d == D_MODEL and seq % TM_MIX == 0 and seq // GRID_W >= 2 * NA_ROWS and t % TM_OUT == 0

    group = SWA_Q_HEADS // SWA_KV_HEADS
    head_order = np.arange(SWA_Q_HEADS).reshape(SWA_KV_HEADS, group).T.reshape(-1)

    def reorder_heads(a, axis, start):
        take = lambda lo, hi: lax.slice_in_dim(a, lo, hi, axis=axis)
        heads = [take(start + h * SWA_HEAD_DIM, start + (h + 1) * SWA_HEAD_DIM) for h in head_order]
        return jnp.concatenate([take(0, start)] + heads + [take(start + SWA_Q_WIDTH, a.shape[axis])], axis=axis)

    cos_t, sin_t = _rope_tables(seq)

    n_assign = t * TOP_K
    assert n_assign <= PAD_FLAG
    n_rows = n_assign + N_EXPERTS * EXPERT_BLOCK
    n_blk = n_rows // EXPERT_BLOCK
    n_tiles = t // TM_MIX

    x2 = x.reshape(t, d)
    for l in range(depth):
        w_in_l = reorder_heads(w_in[l], 1, C_QB).astype(_BF16)
        b_in_l = reorder_heads(b_in[l], 0, C_QB).reshape(1, D_IN)
        qkva, qb, kvb, gates = _inproj(x2, norm1_g[l].reshape(1, d), w_in_l, b_in_l, cos_t, sin_t, seq)

        x1, h2, keys, wts, cnt = _mixer(
            swa_sinks[l][head_order].astype(_F32) * LOG2E, x2, qkva, qb, kvb, gates, _na_bias_table(na_rpb[l]),
            w_up_a[l].astype(_BF16), reorder_heads(w_up_b[l], 0, 0).astype(_BF16), w_out[l].astype(_BF16),
            norm2_g[l].reshape(1, d), _split_bf16(w_router[l].T), b_router[l].reshape(N_EXPERTS, 1),
            batch, seq)

        cnt = cnt.reshape(n_tiles, N_EXPERTS, LANES)[:, :, 0].astype(jnp.int32)
        counts = jnp.sum(cnt, axis=0)
        padded = (counts + EXPERT_BLOCK - 1) // EXPERT_BLOCK * EXPERT_BLOCK
        pend = jnp.cumsum(padded)
        pad_i = jnp.arange(EXPERT_BLOCK, dtype=jnp.int32)[None, :]
        pad_keys = jnp.where(pad_i < (padded - counts)[:, None],
                             (jnp.arange(N_EXPERTS, dtype=jnp.int32)[:, None] << KEY_SHIFT) | PAD_FLAG | pad_i,
                             jnp.iinfo(jnp.int32).max)
        sorted_keys = jnp.sort(jnp.concatenate([keys[:TOP_K].reshape(-1), pad_keys.reshape(-1)]))
        row_a = jnp.where((sorted_keys & PAD_FLAG) == 0, sorted_keys & (PAD_FLAG - 1), -1)
        row_a = row_a.reshape(n_blk, EXPERT_BLOCK)
        blk_start = jnp.arange(n_blk, dtype=jnp.int32) * EXPERT_BLOCK
        blk_exp = jnp.minimum(jnp.sum((pend[None, :] <= blk_start[:, None]).astype(jnp.int32), axis=1),
                              N_EXPERTS - 1)
        n_active = (pend[-1:] // EXPERT_BLOCK).astype(jnp.int32)
        r_in_blk = jnp.arange(EXPERT_BLOCK, dtype=jnp.int32)[None, :]
        parity = (jnp.arange(-1, n_blk, dtype=jnp.int32) % 2)[:, None]
        trash = n_assign + parity * EXPERT_BLOCK + r_in_blk
        gidx = jnp.where(row_a >= 0, row_a >> 2, 0)
        sidx = jnp.where(row_a >= 0, (row_a & (TOP_K - 1)) * t + (row_a >> 2), trash[1:])
        sidx = jnp.concatenate([trash[:1], sidx], axis=0)

        y = _experts(blk_exp, n_active, gidx, sidx, h2,
                     w1[l], b1[l].reshape(N_EXPERTS, 1, 2 * D_FF),
                     w2[l], b2[l].reshape(N_EXPERTS, 1, D_MODEL), t)
        x2 = _combine(x1, y, wts, final_g.reshape(1, d))
    return x2.reshape(batch, seq, d)
```

```python
import functools

import jax
import jax.numpy as jnp
import numpy as np
from jax import lax
from jax.experimental import pallas as pl
from jax.experimental.pallas import tpu as pltpu
from jax.experimental.pallas import tpu_sc as plsc

D_MODEL = 1024
GRID_W = 64
NA_HEADS = 8
NA_HEAD_DIM = 64
NA_ROWS = 8
NA_COLS = 16
SWA_Q_HEADS = 8
SWA_KV_HEADS = 2
SWA_HEAD_DIM = 64
SWA_WINDOW = 128
SWA_BLOCK = 128
ROPE_THETA = 10000.0
N_EXPERTS = 32
TOP_K = 4
D_FF = 1024
SWIGLU_LIMIT = 7.0
SWIGLU_ALPHA = 1.702
EXPERT_BLOCK = 256
RMS_EPS = 1e-5

NA_WIDTH = NA_HEADS * NA_HEAD_DIM
SWA_Q_WIDTH = SWA_Q_HEADS * SWA_HEAD_DIM
SWA_KV_WIDTH = SWA_KV_HEADS * SWA_HEAD_DIM
LANES = 128
SUBLANES = 8
NEG_BIG = -1e30

C_QA, C_KA, C_VA = 0, NA_WIDTH, 2 * NA_WIDTH
C_QB = 3 * NA_WIDTH
C_KB = C_QB + SWA_Q_WIDTH
C_VB = C_KB + SWA_KV_WIDTH
C_GA = C_VB + SWA_KV_WIDTH
C_GB = C_GA + D_MODEL
D_IN = C_GB + D_MODEL

TM_PROJ = 512
TM_MIX = 512
TM_OUT = 512
FF_CHUNK = 256
NA_BATCH_ROWS = 4
LOG2E = 1.4426950408889634
KEY_SHIFT = 18
PAD_FLAG = 1 << (KEY_SHIFT - 1)
VMEM_LIMIT = 56 * 1024 * 1024

_BF16 = jnp.bfloat16
_F32 = jnp.float32


def _const_spec(shape):
    nd = len(shape)
    return pl.BlockSpec(shape, lambda *_: (0,) * nd, pipeline_mode=pl.Buffered(1))


HALF = D_MODEL // 2
_HI_MASK = 0xFFFF0000


def _pack_rows(lo_bf16, hi_bf16):
    lo = lax.bitcast_convert_type(lo_bf16.astype(_F32), jnp.uint32) >> 16
    hi = lax.bitcast_convert_type(hi_bf16.astype(_F32), jnp.uint32) & jnp.uint32(_HI_MASK)
    return lo | hi


def _unpack_rows(u):
    lo = lax.bitcast_convert_type(u << 16, _F32)
    hi = lax.bitcast_convert_type(u & jnp.uint32(_HI_MASK), _F32)
    return lo, hi


def _rope_slab(y, cos, sin_signed):
    lane = lax.broadcasted_iota(jnp.int32, y.shape, 1)
    first_half = (lane & (SWA_HEAD_DIM - 1)) < (SWA_HEAD_DIM // 2)
    rot = jnp.where(first_half, pltpu.roll(y, LANES - SWA_HEAD_DIM // 2, axis=1),
                    pltpu.roll(y, SWA_HEAD_DIM // 2, axis=1))
    return y * cos + rot * sin_signed


def _inproj_kernel(x_ref, g_ref, w_ref, b_ref, cos_ref, sin_ref,
                   qkva_ref, qb_ref, kvb_ref, gate_ref):
    x = x_ref[...]
    var = jnp.mean(x * x, axis=-1, keepdims=True)
    h = (x * lax.rsqrt(var + RMS_EPS) * g_ref[...]).astype(_BF16)

    def proj(c0, c1):
        return jnp.dot(h, w_ref[:, c0:c1], preferred_element_type=_F32) + b_ref[:, c0:c1]

    scale = NA_HEAD_DIM ** -0.5 * LOG2E
    qkva_ref[:, C_QA:C_KA] = (proj(C_QA, C_KA) * scale).astype(_BF16)
    qkva_ref[:, C_KA:C_VA] = proj(C_KA, C_VA).astype(_BF16)
    qkva_ref[:, C_VA:C_QB] = proj(C_VA, C_QB).astype(_BF16)

    cos = cos_ref[...]
    sin = sin_ref[...]
    qb = proj(C_QB, C_KB)
    for s in range(SWA_Q_WIDTH // LANES):
        slab = _rope_slab(qb[:, s * LANES:(s + 1) * LANES], cos, sin)
        qb_ref[:, s * LANES:(s + 1) * LANES] = (slab * (SWA_HEAD_DIM ** -0.5 * LOG2E)).astype(_BF16)
    kvb = proj(C_KB, C_GA)
    kvb_ref[:, 0:LANES] = _rope_slab(kvb[:, 0:LANES], cos, sin).astype(_BF16)
    kvb_ref[:, LANES:2 * LANES] = kvb[:, LANES:2 * LANES].astype(_BF16)

    for c0 in range(C_GA, D_IN, 512):
        gate_ref[:, c0 - C_GA:c0 - C_GA + 512] = jax.nn.sigmoid(proj(c0, c0 + 512)).astype(_BF16)


def _inproj(x2, g1, w_in, b_in, cos_t, sin_t, seq):
    t = x2.shape[0]
    tm = TM_PROJ
    nseq = seq // tm
    row = lambda i: (i, 0)
    return pl.pallas_call(
        _inproj_kernel,
        grid=(t // tm,),
        in_specs=[
            pl.BlockSpec((tm, D_MODEL), row),
            _const_spec((1, D_MODEL)),
            _const_spec((D_MODEL, D_IN)),
            _const_spec((1, D_IN)),
            pl.BlockSpec((tm, LANES), lambda i: (i % nseq, 0)),
            pl.BlockSpec((tm, LANES), lambda i: (i % nseq, 0)),
        ],
        out_specs=[
            pl.BlockSpec((tm, 3 * NA_WIDTH), row),
            pl.BlockSpec((tm, SWA_Q_WIDTH), row),
            pl.BlockSpec((tm, 2 * SWA_KV_WIDTH), row),
            pl.BlockSpec((tm, 2 * D_MODEL), row),
        ],
        out_shape=[
            jax.ShapeDtypeStruct((t, 3 * NA_WIDTH), _BF16),
            jax.ShapeDtypeStruct((t, SWA_Q_WIDTH), _BF16),
            jax.ShapeDtypeStruct((t, 2 * SWA_KV_WIDTH), _BF16),
            jax.ShapeDtypeStruct((t, 2 * D_MODEL), _BF16),
        ],
        compiler_params=pltpu.CompilerParams(
            dimension_semantics=("parallel",), vmem_limit_bytes=VMEM_LIMIT),
        name="inproj",
    )(x2, g1, w_in, b_in, cos_t, sin_t)


_NT_DIMS = (((1,), (1,)), ((), ()))


def _mixer_kernel(sink_ref, x_ref, q_ref, kp_ref, kc_ref, kn_ref, vp_ref, vc_ref, vn_ref,
                  qb_ref, kvp_ref, kvc_ref, kvn_ref, gate_ref, tbl_ref, mask_ref,
                  wua_ref, wub_ref, wo_ref, g2_ref, wr_ref, br_ref,
                  x1_ref, h2_ref, key_ref, wts_ref, cnt_ref,
                  kcat, vcat, kvcat, oa_s, ob_s, s_s, e_s, inv_s, *, rows, n_swa_blocks):
    j = pl.program_id(1)
    rows_per_tile = TM_MIX // GRID_W
    halo = (NA_ROWS // 2) * GRID_W
    band = NA_ROWS * GRID_W

    kcat[0:halo] = kp_ref[...]
    kcat[halo:halo + TM_MIX] = kc_ref[...]
    kcat[halo + TM_MIX:] = kn_ref[...]
    vcat[0:halo] = vp_ref[...]
    vcat[halo:halo + TM_MIX] = vc_ref[...]
    vcat[halo + TM_MIX:] = vn_ref[...]

    lane_q = lax.broadcasted_iota(jnp.int32, (GRID_W, LANES), 1)
    low_q = lane_q < NA_HEAD_DIM
    lane_o = lax.broadcasted_iota(jnp.int32, (GRID_W, LANES), 1) < NA_HEAD_DIM

    n_pairs = NA_HEADS // 2
    chain = 2 * GRID_W

    def band_start(i):
        r = j * rows_per_tile + i
        rs = jnp.clip(r - NA_ROWS // 2, 0, rows - NA_ROWS)
        start = pl.multiple_of((rs - (j * rows_per_tile - NA_ROWS // 2)) * GRID_W, GRID_W)
        return start, r - rs

    for half in range(rows_per_tile // NA_BATCH_ROWS):
        def na_scores(b, carry):
            i = half * NA_BATCH_ROWS + b
            start, d = band_start(i)
            q0 = pl.multiple_of(i * GRID_W, GRID_W)
            for p in range(n_pairs):
                cols = slice(p * LANES, (p + 1) * LANES)
                qpair = q_ref[pl.ds(q0, GRID_W), cols]
                zero = jnp.zeros_like(qpair)
                qs = jnp.concatenate([jnp.where(low_q, qpair, zero), jnp.where(low_q, zero, qpair)], axis=0)
                kb = kcat[pl.ds(start, band), cols]
                s = lax.dot_general(qs, kb, _NT_DIMS, preferred_element_type=_F32) + tbl_ref[d, p]
                s_s[pl.ds(pl.multiple_of((b * n_pairs + p) * chain, chain), chain), :] = s
            return carry

        def na_softmax(b, carry):
            rws = pl.ds(pl.multiple_of(b * n_pairs * chain, n_pairs * chain), n_pairs * chain)
            s = s_s[rws, :]
            e = jnp.exp2(s - jnp.max(s, axis=-1, keepdims=True))
            e_s[rws, :] = e.astype(_BF16)
            inv_s[rws, :] = jnp.broadcast_to(1.0 / jnp.sum(e, axis=-1, keepdims=True), (n_pairs * chain, LANES))
            return carry

        def na_values(b, carry):
            i = half * NA_BATCH_ROWS + b
            start, _ = band_start(i)
            q0 = pl.multiple_of(i * GRID_W, GRID_W)
            for p in range(n_pairs):
                cols = slice(p * LANES, (p + 1) * LANES)
                rws = pl.ds(pl.multiple_of((b * n_pairs + p) * chain, chain), chain)
                vb = vcat[pl.ds(start, band), cols]
                o = jnp.dot(e_s[rws, :], vb, preferred_element_type=_F32) * inv_s[rws, :]
                oa_s[pl.ds(q0, GRID_W), cols] = jnp.where(lane_o, o[:GRID_W], o[GRID_W:]).astype(_BF16)
            return carry

        lax.fori_loop(0, NA_BATCH_ROWS, na_scores, 0)
        lax.fori_loop(0, NA_BATCH_ROWS, na_softmax, 0)
        lax.fori_loop(0, NA_BATCH_ROWS, na_values, 0)

    kvcat[0:SWA_BLOCK] = kvp_ref[...]
    kvcat[SWA_BLOCK:SWA_BLOCK + TM_MIX] = kvc_ref[...]
    kvcat[SWA_BLOCK + TM_MIX:] = kvn_ref[...]

    n_slabs = SWA_Q_WIDTH // LANES
    stack = n_slabs * SWA_BLOCK
    lane_s = lax.broadcasted_iota(jnp.int32, (SWA_BLOCK, LANES), 1) < SWA_HEAD_DIM
    rowblk = lax.broadcasted_iota(jnp.int32, (stack, 1), 0) // SWA_BLOCK
    wband = 3 * SWA_BLOCK
    sinks = []
    for g in range(SWA_KV_HEADS):
        sink = jnp.zeros((stack, 1), _F32)
        for s_ in range(n_slabs):
            sink = jnp.where(rowblk == s_, sink_ref[SWA_KV_HEADS * s_ + g], sink)
        sinks.append(sink)

    def swa_block(n, carry):
        nb = j * (TM_MIX // SWA_BLOCK) + n
        t0 = pl.multiple_of(n * SWA_BLOCK, SWA_BLOCK)
        variant = jnp.where(nb == 0, 0, jnp.where(nb == n_swa_blocks - 1, 2, 1))
        mask = mask_ref[variant]
        mask = jnp.concatenate([mask] * n_slabs, axis=0)
        kband = kvcat[pl.ds(t0, wband), 0:LANES]
        vband = kvcat[pl.ds(t0, wband), LANES:2 * LANES]
        outs = []
        for g in range(SWA_KV_HEADS):
            parts = []
            for s_ in range(n_slabs):
                slab = qb_ref[pl.ds(t0, SWA_BLOCK), s_ * LANES:(s_ + 1) * LANES]
                zero = jnp.zeros_like(slab)
                parts.append(jnp.where(lane_s, slab, zero) if g == 0 else jnp.where(lane_s, zero, slab))
            qg = jnp.concatenate(parts, axis=0)
            s = lax.dot_general(qg, kband, _NT_DIMS, preferred_element_type=_F32) + mask
            m = jnp.maximum(jnp.max(s, axis=-1, keepdims=True), sinks[g])
            e = jnp.exp2(s - m)
            den = jnp.sum(e, axis=-1, keepdims=True) + jnp.exp2(sinks[g] - m)
            outs.append(jnp.dot(e.astype(_BF16), vband, preferred_element_type=_F32) * (1.0 / den))
        for s_ in range(n_slabs):
            rs_ = slice(s_ * SWA_BLOCK, (s_ + 1) * SWA_BLOCK)
            ob_s[pl.ds(t0, SWA_BLOCK), s_ * LANES:(s_ + 1) * LANES] = jnp.where(
                lane_s, outs[0][rs_], outs[1][rs_]).astype(_BF16)
        return carry

    lax.fori_loop(0, TM_MIX // SWA_BLOCK, swa_block, 0)

    ua = jnp.dot(oa_s[...], wua_ref[...], preferred_element_type=_F32)
    ub = jnp.dot(ob_s[...], wub_ref[...], preferred_element_type=_F32)
    merged = (gate_ref[:, 0:D_MODEL].astype(_F32) * ua
              + gate_ref[:, D_MODEL:].astype(_F32) * ub).astype(_BF16)
    x1 = x_ref[...] + jnp.dot(merged, wo_ref[...], preferred_element_type=_F32)
    x1_ref[...] = x1

    var = jnp.mean(x1 * x1, axis=-1, keepdims=True)
    h2 = x1 * lax.rsqrt(var + RMS_EPS) * g2_ref[...]
    h2_hi = h2.astype(_BF16)
    h2_ref[...] = _pack_rows(h2_hi[:, :HALF], h2_hi[:, HALF:])
    h2_lo = (h2 - h2_hi.astype(_F32)).astype(_BF16)
    both = lax.dot_general(wr_ref[...], h2_hi, _NT_DIMS, preferred_element_type=_F32)
    cross = lax.dot_general(wr_ref[0:N_EXPERTS], h2_lo, _NT_DIMS, preferred_element_type=_F32)
    logits = both[0:N_EXPERTS] + both[N_EXPERTS:] + cross + br_ref[...]
    eidx = lax.broadcasted_iota(jnp.int32, logits.shape, 0)
    vals, idxs, hots = [], [], []
    for _ in range(TOP_K):
        m = jnp.max(logits, axis=0, keepdims=True)
        idx = jnp.min(jnp.where(logits == m, eidx, N_EXPERTS), axis=0, keepdims=True)
        hot = eidx == idx
        logits = jnp.where(hot, -jnp.inf, logits)
        vals.append(m)
        idxs.append(idx)
        hots.append(hot)
    es = [jnp.exp(v - vals[0]) for v in vals]
    inv = 1.0 / (es[0] + es[1] + es[2] + es[3])
    sel = jnp.zeros(logits.shape, _F32)
    for hot in hots:
        sel = sel + jnp.where(hot, 1.0, 0.0)
    tile_idx = pl.program_id(0) * pl.num_programs(1) + j
    tok = tile_idx * TM_MIX + lax.broadcasted_iota(jnp.int32, (1, TM_MIX), 1)
    keys = [(idx << KEY_SHIFT) | (tok * TOP_K + k) for k, idx in enumerate(idxs)]
    key_ref[...] = jnp.concatenate(keys + [jnp.zeros((8 - TOP_K, TM_MIX), jnp.int32)], axis=0)
    wts_ref[...] = jnp.concatenate([e * inv for e in es] + [jnp.zeros((8 - TOP_K, TM_MIX), _F32)], axis=0)
    cnt_ref[...] = jnp.broadcast_to(jnp.sum(sel, axis=1, keepdims=True), (N_EXPERTS, LANES))


def _swa_mask_table():
    qpos = np.arange(SWA_BLOCK)[:, None]
    koff = np.arange(3 * SWA_BLOCK)[None, :] - SWA_BLOCK
    rel_ok = np.abs(koff - qpos) <= SWA_WINDOW
    has_prev = np.array([False, True, True])[:, None, None]
    has_next = np.array([True, True, False])[:, None, None]
    ok = rel_ok[None] & ((koff >= 0)[None] | has_prev) & ((koff < SWA_BLOCK)[None] | has_next)
    return jnp.asarray(np.where(ok, 0.0, NEG_BIG), _F32)


def _mixer(sinks_perm, x2, qkva, qb, kvb, gates, tbl, wua, wub, wo, g2, wr_t, br, batch, seq):
    t = x2.shape[0]
    tm = TM_MIX
    nj = seq // tm
    rows = seq // GRID_W
    hb = tm // ((NA_ROWS // 2) * GRID_W)
    sb = tm // SWA_BLOCK
    n_halo = seq // ((NA_ROWS // 2) * GRID_W)
    n_swa = seq // SWA_BLOCK
    halo = (NA_ROWS // 2) * GRID_W
    swa_mask = _swa_mask_table()
    score_rows = NA_BATCH_ROWS * NA_HEADS * GRID_W
    assert n_swa >= 2

    tile = lambda b, j, *_: (b * nj + j, 0)

    def na_spec(col, which):
        if which == 0:
            return pl.BlockSpec((tm, NA_WIDTH), lambda b, j, *_: (b * nj + j, col))
        if which < 0:
            return pl.BlockSpec((halo, NA_WIDTH),
                                lambda b, j, *_: (b * n_halo + jnp.maximum(j * hb - 1, 0), col))
        return pl.BlockSpec((halo, NA_WIDTH),
                            lambda b, j, *_: (b * n_halo + jnp.minimum(j * hb + hb, n_halo - 1), col))

    kv_prev = pl.BlockSpec((SWA_BLOCK, 2 * SWA_KV_WIDTH),
                           lambda b, j, *_: (b * n_swa + jnp.maximum(j * sb - 1, 0), 0))
    kv_next = pl.BlockSpec((SWA_BLOCK, 2 * SWA_KV_WIDTH),
                           lambda b, j, *_: (b * n_swa + jnp.minimum(j * sb + sb, n_swa - 1), 0))

    def cspec(shape):
        nd = len(shape)
        return pl.BlockSpec(shape, lambda *_: (0,) * nd, pipeline_mode=pl.Buffered(1))

    grid_spec = pltpu.PrefetchScalarGridSpec(
        num_scalar_prefetch=1,
        grid=(batch, nj),
        in_specs=[
            pl.BlockSpec((tm, D_MODEL), tile),
            na_spec(0, 0),
            na_spec(1, -1), na_spec(1, 0), na_spec(1, 1),
            na_spec(2, -1), na_spec(2, 0), na_spec(2, 1),
            pl.BlockSpec((tm, SWA_Q_WIDTH), tile),
            kv_prev, pl.BlockSpec((tm, 2 * SWA_KV_WIDTH), tile), kv_next,
            pl.BlockSpec((tm, 2 * D_MODEL), tile),
            cspec(tbl.shape), cspec(swa_mask.shape),
            cspec(wua.shape), cspec(wub.shape), cspec(wo.shape),
            cspec(g2.shape), cspec(wr_t.shape), cspec(br.shape),
        ],
        out_specs=[
            pl.BlockSpec((tm, D_MODEL), tile),
            pl.BlockSpec((tm, HALF), tile),
            pl.BlockSpec((8, tm), lambda b, j, *_: (0, b * nj + j)),
            pl.BlockSpec((8, tm), lambda b, j, *_: (0, b * nj + j)),
            pl.BlockSpec((N_EXPERTS, LANES), tile),
        ],
        scratch_shapes=[
            pltpu.VMEM((tm + 2 * halo, NA_WIDTH), _BF16),
            pltpu.VMEM((tm + 2 * halo, NA_WIDTH), _BF16),
            pltpu.VMEM((tm + 2 * SWA_BLOCK, 2 * SWA_KV_WIDTH), _BF16),
            pltpu.VMEM((tm, NA_WIDTH), _BF16),
            pltpu.VMEM((tm, SWA_Q_WIDTH), _BF16),
            pltpu.VMEM((score_rows, NA_ROWS * GRID_W), _F32),
            pltpu.VMEM((score_rows, NA_ROWS * GRID_W), _BF16),
            pltpu.VMEM((score_rows, LANES), _F32),
        ],
    )
    return pl.pallas_call(
        functools.partial(_mixer_kernel, rows=rows, n_swa_blocks=n_swa),
        grid_spec=grid_spec,
        out_shape=[
            jax.ShapeDtypeStruct((t, D_MODEL), _F32),
            jax.ShapeDtypeStruct((t, HALF), jnp.uint32),
            jax.ShapeDtypeStruct((8, t), jnp.int32),
            jax.ShapeDtypeStruct((8, t), _F32),
            jax.ShapeDtypeStruct((t // tm * N_EXPERTS, LANES), _F32),
        ],
        compiler_params=pltpu.CompilerParams(
            dimension_semantics=("parallel", "parallel"), vmem_limit_bytes=VMEM_LIMIT),
        name="mixer",
    )(sinks_perm, x2, qkva, qkva, qkva, qkva, qkva, qkva, qkva, qb, kvb, kvb, kvb, gates, tbl, swa_mask,
      wua, wub, wo, g2, wr_t, br)


SC_CORES = 2
SC_SUBCORES = 16
SC_WINDOW = 64


def _sc_dispatch(h2_rows, row_tok):
    n_rows = row_tok.shape[0]
    n_workers = SC_CORES * SC_SUBCORES
    per_w = n_rows // n_workers
    assert per_w * n_workers == n_rows and per_w % SC_WINDOW == 0
    mesh = plsc.VectorSubcoreMesh(core_axis_name="c", subcore_axis_name="s")

    n_win = per_w // SC_WINDOW
    assert n_win % 2 == 0

    @functools.partial(
        pl.kernel, mesh=mesh,
        out_type=jax.ShapeDtypeStruct((n_rows, HALF), jnp.uint32),
        scratch_types=[pltpu.VMEM((per_w,), jnp.int32),
                       pltpu.VMEM((SC_WINDOW, HALF), jnp.uint32),
                       pltpu.VMEM((SC_WINDOW, HALF), jnp.uint32),
                       pltpu.SemaphoreType.DMA, pltpu.SemaphoreType.DMA],
    )
    def dispatch(h2_hbm, idx_hbm, out_hbm, idx_v, rows0, rows1, sem0, sem1):
        wid = lax.axis_index("s") * SC_CORES + lax.axis_index("c")
        base = wid * per_w
        pltpu.sync_copy(idx_hbm.at[pl.ds(base, per_w)], idx_v)
        bufs, sems = (rows0, rows1), (sem0, sem1)

        def gather(w, b):
            off = pl.multiple_of(w * SC_WINDOW, SC_WINDOW)
            return pltpu.make_async_copy(h2_hbm.at[idx_v.at[pl.ds(off, SC_WINDOW)]], bufs[b], sems[b])

        gather(0, 0).start()

        @pl.loop(0, n_win, step=2)
        def _(w):
            for b in range(2):
                gather(w + b, b).wait()

                @pl.when(w + b + 1 < n_win)
                def _():
                    gather(w + b + 1, 1 - b).start()

                off = pl.multiple_of((w + b) * SC_WINDOW, SC_WINDOW)
                pltpu.sync_copy(bufs[b], out_hbm.at[pl.ds(base + off, SC_WINDOW)])

    return dispatch(h2_rows, row_tok)


def _expert_kernel(*refs, n_blocks, n_real_rows, gather):
    if gather:
        bexp_ref, nact_ref, gidx_hbm, sidx_hbm, h2_hbm, w1_ref, b1_ref, w2_ref, b2_ref, y_hbm = refs[:10]
    else:
        bexp_ref, nact_ref, sidx_hbm, x_ref, _, w1_ref, b1_ref, w2_ref, b2_ref, y_hbm = refs[:10]
    gs0, gs1, ss0, ss1, xbuf0, xbuf1, ybuf0, ybuf1, w1b, w2b, isem, gsem, ssem = refs[10:]
    i = pl.program_id(0)
    nact = nact_ref[0]
    eb = EXPERT_BLOCK
    gs, ss, xbuf, ybuf = (gs0, gs1), (ss0, ss1), (xbuf0, xbuf1), (ybuf0, ybuf1)
    rows = pl.ds(0, eb)

    def gidx_copy(blk, slot):
        return pltpu.make_async_copy(gidx_hbm.at[blk], gs[slot], isem.at[0, slot])

    def sidx_copy(entry, slot):
        return pltpu.make_async_copy(sidx_hbm.at[entry], ss[slot], isem.at[1, slot])

    def gather_row(slot, r, priority=0):
        pltpu.make_async_copy(h2_hbm.at[pl.ds(gs[slot][r], 1), :], xbuf[slot].at[pl.ds(r, 1), :],
                              gsem.at[slot]).start(priority=priority)

    def wait_gather(slot):
        pltpu.make_async_copy(h2_hbm.at[rows, :], xbuf[slot].at[rows, :], gsem.at[slot]).wait()

    def scatter_row(slot, r, priority=0):
        pltpu.make_async_copy(ybuf[slot].at[pl.ds(r, 1), :], y_hbm.at[pl.ds(ss[slot][r], 1), :],
                              ssem.at[slot]).start(priority=priority)

    def wait_scatter(slot):
        pltpu.make_async_copy(ybuf[slot].at[rows, :], y_hbm.at[rows, :], ssem.at[slot]).wait()

    def rolled(fn, slot):
        def body(r, c):
            fn(slot, r)
            return c
        lax.fori_loop(0, eb, body, 0)

    @pl.when(jnp.logical_and(i == 0, nact > 0))
    def _():
        if gather:
            gidx_copy(0, 0).start()
            gidx_copy(0, 0).wait()
            rolled(gather_row, 0)
            gidx_copy(jnp.minimum(1, n_blocks - 1), 1).start()
        sidx_copy(0, 1).start()
        ybuf1[...] = jnp.zeros(ybuf1.shape, jnp.uint32)
        init = pltpu.make_async_copy(ybuf1.at[rows, :], y_hbm.at[pl.ds(n_real_rows, eb), :], ssem.at[0])
        init.start()
        init.wait()

    def step(cur):
        nxt = 1 - cur
        sidx_copy(i + 1, cur).start()
        sidx_copy(0, nxt).wait()
        if gather:
            gidx_copy(jnp.minimum(i + 2, n_blocks - 1), cur).start()
            gidx_copy(0, nxt).wait()
            wait_gather(cur)
            x_packed = xbuf[cur][rows, :]
        else:
            x_packed = x_ref[...]
        x_lo, x_hi = (half.astype(_BF16) for half in _unpack_rows(x_packed))

        def x_dot(c0):
            return (jnp.dot(x_lo, w1b[0:HALF, c0:c0 + FF_CHUNK], preferred_element_type=_F32)
                    + jnp.dot(x_hi, w1b[HALF:, c0:c0 + FF_CHUNK], preferred_element_type=_F32))

        n_chunks = D_FF // FF_CHUNK
        spare = pl.ds(pl.multiple_of(eb + jnp.minimum(i, 0) * SUBLANES, SUBLANES), SUBLANES)
        y = None
        for jc in range(n_chunks):
            if gather:
                issue = gather_row if jc < n_chunks // 2 else scatter_row
                group = 2 * eb // n_chunks
                first = (jc % (n_chunks // 2)) * group
            else:
                issue, group, first = scatter_row, eb // n_chunks, jc * (eb // n_chunks)
            for r in range(first, first + group):
                issue(nxt, r, r % 2)
            anchor_bufs = (xbuf[nxt], ybuf[nxt]) if gather else (ybuf[nxt],)
            bits = None
            for buf in anchor_bufs:
                buf[spare, 0:FF_CHUNK] = jnp.zeros((SUBLANES, FF_CHUNK), jnp.uint32)
            for buf in anchor_bufs:
                bits = buf[spare, 0:FF_CHUNK] if bits is None else bits | buf[spare, 0:FF_CHUNK]
            anchor = lax.bitcast_convert_type(bits[0:1], _F32)
            c0 = jc * FF_CHUNK
            g = x_dot(c0) + (b1_ref[:, c0:c0 + FF_CHUNK] + anchor)
            u = x_dot(D_FF + c0) + b1_ref[:, D_FF + c0:D_FF + c0 + FF_CHUNK]
            gate = jnp.minimum(g, SWIGLU_LIMIT)
            up = jnp.clip(u, -SWIGLU_LIMIT, SWIGLU_LIMIT)
            act = ((up + 1.0) * (gate * jax.nn.sigmoid(gate * SWIGLU_ALPHA))).astype(_BF16)
            part = jnp.dot(act, w2b[c0:c0 + FF_CHUNK, :], preferred_element_type=_F32)
            y = part + b2_ref[...] if y is None else y + part

        @pl.when(i >= 1)
        def _():
            wait_scatter(cur)

        ybuf[cur][rows, :] = _pack_rows(y[:, :HALF].astype(_BF16), y[:, HALF:].astype(_BF16))

        @pl.when(i == nact - 1)
        def _():
            sidx_copy(0, cur).wait()
            rolled(scatter_row, cur)
            wait_scatter(nxt)
            wait_scatter(cur)
            if gather:
                wait_gather(nxt)
                gidx_copy(0, cur).wait()

    new_expert = jnp.logical_or(i == 0, bexp_ref[i] != bexp_ref[jnp.maximum(i - 1, 0)])

    @pl.when(jnp.logical_and(i < nact, new_expert))
    def _():
        w1b[...] = w1_ref[...].astype(_BF16)
        w2b[...] = w2_ref[...].astype(_BF16)

    for parity in range(2):
        @pl.when(jnp.logical_and(i < nact, i % 2 == parity))
        def _():
            step(parity)


def _experts(blk_exp, n_active, sidx, w1, b1, w2, b2, n_tokens, *, gidx=None, h2=None, xs=None, y_in=None):
    gather = xs is None
    n_blk = blk_exp.shape[0]
    eb = EXPERT_BLOCK
    n_real = TOP_K * n_tokens
    wmap = lambda i, be, na: (be[i], 0, 0)
    hbm = pl.BlockSpec(memory_space=pl.ANY)
    if gather:
        row_args, row_specs = (gidx, sidx, h2), [hbm, hbm, hbm]
    else:
        row_args, row_specs = (sidx, xs, y_in), [hbm, pl.BlockSpec((eb, HALF), lambda i, be, na: (i, 0)), hbm]
    grid_spec = pltpu.PrefetchScalarGridSpec(
        num_scalar_prefetch=2,
        grid=(n_blk,),
        in_specs=row_specs + [
            pl.BlockSpec((None, D_MODEL, 2 * D_FF), wmap),
            pl.BlockSpec((None, 1, 2 * D_FF), wmap),
            pl.BlockSpec((None, D_FF, D_MODEL), wmap),
            pl.BlockSpec((None, 1, D_MODEL), wmap),
        ],
        out_specs=pl.BlockSpec(memory_space=pl.ANY),
        scratch_shapes=[
            pltpu.SMEM((eb,), jnp.int32), pltpu.SMEM((eb,), jnp.int32),
            pltpu.SMEM((eb,), jnp.int32), pltpu.SMEM((eb,), jnp.int32),
            pltpu.VMEM((eb + SUBLANES, HALF), jnp.uint32),
            pltpu.VMEM((eb + SUBLANES, HALF), jnp.uint32),
            pltpu.VMEM((eb + SUBLANES, HALF), jnp.uint32),
            pltpu.VMEM((eb + SUBLANES, HALF), jnp.uint32),
            pltpu.VMEM((D_MODEL, 2 * D_FF), _BF16),
            pltpu.VMEM((D_FF, D_MODEL), _BF16),
            pltpu.SemaphoreType.DMA((2, 2)),
            pltpu.SemaphoreType.DMA((2,)),
            pltpu.SemaphoreType.DMA((2,)),
        ],
    )
    return pl.pallas_call(
        functools.partial(_expert_kernel, n_blocks=n_blk, n_real_rows=n_real, gather=gather),
        grid_spec=grid_spec,
        out_shape=jax.ShapeDtypeStruct((n_real + 2 * eb, HALF), jnp.uint32),
        input_output_aliases={} if gather else {4: 0},
        compiler_params=pltpu.CompilerParams(
            dimension_semantics=("arbitrary",), vmem_limit_bytes=VMEM_LIMIT),
        name="experts_gather" if gather else "experts_dense",
    )(blk_exp, n_active, *row_args, w1, b1, w2, b2)


def _combine_kernel(x1_ref, y0_ref, y1_ref, y2_ref, y3_ref, wts_ref, g_ref, o_ref):
    wt = wts_ref[...].T
    y = x1_ref[...]
    for k, y_ref in enumerate((y0_ref, y1_ref, y2_ref, y3_ref)):
        y = y + wt[:, k:k + 1] * jnp.concatenate(_unpack_rows(y_ref[...]), axis=1)
    var = jnp.mean(y * y, axis=-1, keepdims=True)
    o_ref[...] = y * lax.rsqrt(var + RMS_EPS) * g_ref[...]


def _combine(x1, y, wts, gf):
    t = x1.shape[0]
    tm = TM_OUT
    nt = t // tm
    yspec = lambda k: pl.BlockSpec((tm, HALF), lambda i: (k * nt + i, 0))
    return pl.pallas_call(
        _combine_kernel,
        grid=(nt,),
        in_specs=[
            pl.BlockSpec((tm, D_MODEL), lambda i: (i, 0)),
            yspec(0), yspec(1), yspec(2), yspec(3),
            pl.BlockSpec((8, tm), lambda i: (0, i)),
            _const_spec((1, D_MODEL)),
        ],
        out_specs=pl.BlockSpec((tm, D_MODEL), lambda i: (i, 0)),
        out_shape=jax.ShapeDtypeStruct((t, D_MODEL), _F32),
        compiler_params=pltpu.CompilerParams(
            dimension_semantics=("parallel",), vmem_limit_bytes=VMEM_LIMIT),
        name="combine",
    )(x1, y, y, y, y, wts, gf)


def _split_bf16(w):
    hi = w.astype(_BF16)
    lo = (w - hi.astype(_F32)).astype(_BF16)
    return jnp.concatenate([hi, lo], axis=0)


def _rope_tables(seq):
    half = SWA_HEAD_DIM // 2
    inv_freq = ROPE_THETA ** (-jnp.arange(half, dtype=_F32) / half)
    ang = jnp.arange(seq, dtype=_F32)[:, None] * inv_freq[None, :]
    cos, sin = jnp.cos(ang), jnp.sin(ang)
    cos_t = jnp.tile(jnp.concatenate([cos, cos], axis=1), (1, LANES // SWA_HEAD_DIM))
    sin_t = jnp.tile(jnp.concatenate([-sin, sin], axis=1), (1, LANES // SWA_HEAD_DIM))
    return cos_t, sin_t


def _na_bias_table(rpb):
    col = np.arange(GRID_W)
    cstart = np.clip(col - NA_COLS // 2, 0, GRID_W - NA_COLS)
    kc = np.arange(GRID_W)
    valid = (kc[None, :] >= cstart[:, None]) & (kc[None, :] < cstart[:, None] + NA_COLS)
    off = np.clip(kc[None, :] - col[:, None] + NA_COLS - 1, 0, 2 * NA_COLS - 2)
    ext = jnp.where(valid[None, None], rpb[:, :, off], NEG_BIG)
    u = np.arange(NA_ROWS)[None, :] - np.arange(NA_ROWS)[:, None] + NA_ROWS - 1
    tbl = ext[:, u]
    tbl = tbl.transpose(1, 0, 3, 2, 4).reshape(NA_ROWS, NA_HEADS // 2, 2 * GRID_W, NA_ROWS * GRID_W)
    return (tbl * LOG2E).astype(_F32)


def kernel(x, norm1_g, w_in, b_in, na_rpb, swa_sinks, w_up_a, w_up_b, w_out, norm2_g, w_router,
           b_router, w1, b1, w2, b2, final_g):
    batch, seq, d = x.shape
    depth = w_in.shape[0]
    t = batch * seq
    assert depth == 1, "the final norm is fused into the single layer's combine step"
    assert d == D_MODEL and seq % TM_MIX == 0 and seq // GRID_W >= 2 * NA_ROWS and t % TM_OUT == 0

    group = SWA_Q_HEADS // SWA_KV_HEADS
    head_order = np.arange(SWA_Q_HEADS).reshape(SWA_KV_HEADS, group).T.reshape(-1)

    def reorder_heads(a, axis, start):
        take = lambda lo, hi: lax.slice_in_dim(a, lo, hi, axis=axis)
        heads = [take(start + h * SWA_HEAD_DIM, start + (h + 1) * SWA_HEAD_DIM) for h in head_order]
        return jnp.concatenate([take(0, start)] + heads + [take(start + SWA_Q_WIDTH, a.shape[axis])], axis=axis)

    cos_t, sin_t = _rope_tables(seq)

    n_assign = t * TOP_K
    assert n_assign <= PAD_FLAG
    n_rows = n_assign + N_EXPERTS * EXPERT_BLOCK
    n_blk = n_rows // EXPERT_BLOCK
    n_tiles = t // TM_MIX

    x2 = x.reshape(t, d)
    for l in range(depth):
        w_in_l = reorder_heads(w_in[l], 1, C_QB).astype(_BF16)
        b_in_l = reorder_heads(b_in[l], 0, C_QB).reshape(1, D_IN)
        qkva, qb, kvb, gates = _inproj(x2, norm1_g[l].reshape(1, d), w_in_l, b_in_l, cos_t, sin_t, seq)

        x1, h2, keys, wts, cnt = _mixer(
            swa_sinks[l][head_order].astype(_F32) * LOG2E, x2, qkva, qb, kvb, gates, _na_bias_table(na_rpb[l]),
            w_up_a[l].astype(_BF16), reorder_heads(w_up_b[l], 0, 0).astype(_BF16), w_out[l].astype(_BF16),
            norm2_g[l].reshape(1, d), _split_bf16(w_router[l].T), b_router[l].reshape(N_EXPERTS, 1),
            batch, seq)

        cnt = cnt.reshape(n_tiles, N_EXPERTS, LANES)[:, :, 0].astype(jnp.int32)
        counts = jnp.sum(cnt, axis=0)
        padded = (counts + EXPERT_BLOCK - 1) // EXPERT_BLOCK * EXPERT_BLOCK
        pend = jnp.cumsum(padded)
        pad_i = jnp.arange(EXPERT_BLOCK, dtype=jnp.int32)[None, :]
        pad_keys = jnp.where(pad_i < (padded - counts)[:, None],
                             (jnp.arange(N_EXPERTS, dtype=jnp.int32)[:, None] << KEY_SHIFT) | PAD_FLAG | pad_i,
                             jnp.iinfo(jnp.int32).max)
        sorted_keys = jnp.sort(jnp.concatenate([keys[:TOP_K].reshape(-1), pad_keys.reshape(-1)]))
        row_a = jnp.where((sorted_keys & PAD_FLAG) == 0, sorted_keys & (PAD_FLAG - 1), -1)
        row_a = row_a.reshape(n_blk, EXPERT_BLOCK)
        blk_start = jnp.arange(n_blk, dtype=jnp.int32) * EXPERT_BLOCK
        blk_exp = jnp.minimum(jnp.sum((pend[None, :] <= blk_start[:, None]).astype(jnp.int32), axis=1),
                              N_EXPERTS - 1)
        n_active = (pend[-1:] // EXPERT_BLOCK).astype(jnp.int32)
        r_in_blk = jnp.arange(EXPERT_BLOCK, dtype=jnp.int32)[None, :]
        parity = (jnp.arange(-1, n_blk, dtype=jnp.int32) % 2)[:, None]
        trash = n_assign + parity * EXPERT_BLOCK + r_in_blk
        gidx = jnp.where(row_a >= 0, row_a >> 2, 0)
        sidx = jnp.where(row_a >= 0, (row_a & (TOP_K - 1)) * t + (row_a >> 2), trash[1:])

        n_a = n_blk // 2
        assert n_a % 2 == 0 and (n_blk - n_a) % 2 == 0
        weights = (w1[l], b1[l].reshape(N_EXPERTS, 1, 2 * D_FF), w2[l], b2[l].reshape(N_EXPERTS, 1, D_MODEL))
        xs_b = _sc_dispatch(h2, gidx[n_a:].reshape(-1))
        y = _experts(blk_exp[:n_a], jnp.minimum(n_active, n_a), jnp.concatenate([trash[:1], sidx[:n_a]], axis=0),
                     *weights, t, gidx=gidx[:n_a], h2=h2)
        y = _experts(blk_exp[n_a:], jnp.maximum(n_active - n_a, 0), jnp.concatenate([trash[:1], sidx[n_a:]], axis=0),
                     *weights, t, xs=xs_b, y_in=y)
        x2 = _combine(x1, y, wts, final_g.reshape(1, d))
    return x2.reshape(batch, seq, d)
```

```python
import functools

import jax
import jax.numpy as jnp
import numpy as np
from jax import lax
from jax.experimental import pallas as pl
from jax.experimental.pallas import tpu as pltpu

D_MODEL = 1024
GRID_W = 64
NA_HEADS = 8
NA_HEAD_DIM = 64
NA_ROWS = 8
NA_COLS = 16
SWA_Q_HEADS = 8
SWA_KV_HEADS = 2
SWA_HEAD_DIM = 64
SWA_WINDOW = 128
SWA_BLOCK = 128
ROPE_THETA = 10000.0
N_EXPERTS = 32
TOP_K = 4
D_FF = 1024
SWIGLU_LIMIT = 7.0
SWIGLU_ALPHA = 1.702
EXPERT_BLOCK = 512
RMS_EPS = 1e-5

NA_WIDTH = NA_HEADS * NA_HEAD_DIM
SWA_Q_WIDTH = SWA_Q_HEADS * SWA_HEAD_DIM
SWA_KV_WIDTH = SWA_KV_HEADS * SWA_HEAD_DIM
LANES = 128
SUBLANES = 8
NEG_BIG = -1e30

C_QA, C_KA, C_VA = 0, NA_WIDTH, 2 * NA_WIDTH
C_QB = 3 * NA_WIDTH
C_KB = C_QB + SWA_Q_WIDTH
C_VB = C_KB + SWA_KV_WIDTH
C_GA = C_VB + SWA_KV_WIDTH
C_GB = C_GA + D_MODEL
D_IN = C_GB + D_MODEL

TM_PROJ = 512
TM_MIX = 512
TM_OUT = 512
FF_CHUNK = 256
NA_BATCH_ROWS = 4
LOG2E = 1.4426950408889634
KEY_SHIFT = 18
PAD_FLAG = 1 << (KEY_SHIFT - 1)
VMEM_LIMIT = 56 * 1024 * 1024

_BF16 = jnp.bfloat16
_F32 = jnp.float32


def _const_spec(shape):
    nd = len(shape)
    return pl.BlockSpec(shape, lambda *_: (0,) * nd, pipeline_mode=pl.Buffered(1))


HALF = D_MODEL // 2
_HI_MASK = 0xFFFF0000


def _pack_rows(lo_bf16, hi_bf16):
    lo = lax.bitcast_convert_type(lo_bf16.astype(_F32), jnp.uint32) >> 16
    hi = lax.bitcast_convert_type(hi_bf16.astype(_F32), jnp.uint32) & jnp.uint32(_HI_MASK)
    return lo | hi


def _unpack_rows(u):
    lo = lax.bitcast_convert_type(u << 16, _F32)
    hi = lax.bitcast_convert_type(u & jnp.uint32(_HI_MASK), _F32)
    return lo, hi


def _rope_slab(y, cos, sin_signed):
    lane = lax.broadcasted_iota(jnp.int32, y.shape, 1)
    first_half = (lane & (SWA_HEAD_DIM - 1)) < (SWA_HEAD_DIM // 2)
    rot = jnp.where(first_half, pltpu.roll(y, LANES - SWA_HEAD_DIM // 2, axis=1),
                    pltpu.roll(y, SWA_HEAD_DIM // 2, axis=1))
    return y * cos + rot * sin_signed


def _inproj_kernel(x_ref, g_ref, w_ref, b_ref, cos_ref, sin_ref,
                   qkva_ref, qb_ref, kvb_ref, gate_ref):
    x = x_ref[...]
    var = jnp.mean(x * x, axis=-1, keepdims=True)
    h = (x * lax.rsqrt(var + RMS_EPS) * g_ref[...]).astype(_BF16)

    def proj(c0, c1):
        return jnp.dot(h, w_ref[:, c0:c1], preferred_element_type=_F32) + b_ref[:, c0:c1]

    scale = NA_HEAD_DIM ** -0.5 * LOG2E
    qkva_ref[:, C_QA:C_KA] = (proj(C_QA, C_KA) * scale).astype(_BF16)
    qkva_ref[:, C_KA:C_VA] = proj(C_KA, C_VA).astype(_BF16)
    qkva_ref[:, C_VA:C_QB] = proj(C_VA, C_QB).astype(_BF16)

    cos = cos_ref[...]
    sin = sin_ref[...]
    qb = proj(C_QB, C_KB)
    for s in range(SWA_Q_WIDTH // LANES):
        slab = _rope_slab(qb[:, s * LANES:(s + 1) * LANES], cos, sin)
        qb_ref[:, s * LANES:(s + 1) * LANES] = (slab * (SWA_HEAD_DIM ** -0.5 * LOG2E)).astype(_BF16)
    kvb = proj(C_KB, C_GA)
    kvb_ref[:, 0:LANES] = _rope_slab(kvb[:, 0:LANES], cos, sin).astype(_BF16)
    kvb_ref[:, LANES:2 * LANES] = kvb[:, LANES:2 * LANES].astype(_BF16)

    for c0 in range(C_GA, D_IN, 512):
        gate_ref[:, c0 - C_GA:c0 - C_GA + 512] = jax.nn.sigmoid(proj(c0, c0 + 512)).astype(_BF16)


def _inproj(x2, g1, w_in, b_in, cos_t, sin_t, seq):
    t = x2.shape[0]
    tm = TM_PROJ
    nseq = seq // tm
    row = lambda i: (i, 0)
    return pl.pallas_call(
        _inproj_kernel,
        grid=(t // tm,),
        in_specs=[
            pl.BlockSpec((tm, D_MODEL), row),
            _const_spec((1, D_MODEL)),
            _const_spec((D_MODEL, D_IN)),
            _const_spec((1, D_IN)),
            pl.BlockSpec((tm, LANES), lambda i: (i % nseq, 0)),
            pl.BlockSpec((tm, LANES), lambda i: (i % nseq, 0)),
        ],
        out_specs=[
            pl.BlockSpec((tm, 3 * NA_WIDTH), row),
            pl.BlockSpec((tm, SWA_Q_WIDTH), row),
            pl.BlockSpec((tm, 2 * SWA_KV_WIDTH), row),
            pl.BlockSpec((tm, 2 * D_MODEL), row),
        ],
        out_shape=[
            jax.ShapeDtypeStruct((t, 3 * NA_WIDTH), _BF16),
            jax.ShapeDtypeStruct((t, SWA_Q_WIDTH), _BF16),
            jax.ShapeDtypeStruct((t, 2 * SWA_KV_WIDTH), _BF16),
            jax.ShapeDtypeStruct((t, 2 * D_MODEL), _BF16),
        ],
        compiler_params=pltpu.CompilerParams(
            dimension_semantics=("parallel",), vmem_limit_bytes=VMEM_LIMIT),
        name="inproj",
    )(x2, g1, w_in, b_in, cos_t, sin_t)


_NT_DIMS = (((1,), (1,)), ((), ()))


def _mixer_kernel(sink_ref, x_ref, q_ref, kp_ref, kc_ref, kn_ref, vp_ref, vc_ref, vn_ref,
                  qb_ref, kvp_ref, kvc_ref, kvn_ref, gate_ref, tbl_ref, mask_ref,
                  wua_ref, wub_ref, wo_ref, g2_ref, wr_ref, br_ref,
                  x1_ref, h2_ref, key_ref, wts_ref, cnt_ref,
                  kcat, vcat, kvcat, oa_s, ob_s, s_s, e_s, inv_s, *, rows, n_swa_blocks):
    j = pl.program_id(1)
    rows_per_tile = TM_MIX // GRID_W
    halo = (NA_ROWS // 2) * GRID_W
    band = NA_ROWS * GRID_W

    kcat[0:halo] = kp_ref[...]
    kcat[halo:halo + TM_MIX] = kc_ref[...]
    kcat[halo + TM_MIX:] = kn_ref[...]
    vcat[0:halo] = vp_ref[...]
    vcat[halo:halo + TM_MIX] = vc_ref[...]
    vcat[halo + TM_MIX:] = vn_ref[...]

    lane_q = lax.broadcasted_iota(jnp.int32, (GRID_W, LANES), 1)
    low_q = lane_q < NA_HEAD_DIM
    lane_o = lax.broadcasted_iota(jnp.int32, (GRID_W, LANES), 1) < NA_HEAD_DIM

    n_pairs = NA_HEADS // 2
    chain = 2 * GRID_W

    def band_start(i):
        r = j * rows_per_tile + i
        rs = jnp.clip(r - NA_ROWS // 2, 0, rows - NA_ROWS)
        start = pl.multiple_of((rs - (j * rows_per_tile - NA_ROWS // 2)) * GRID_W, GRID_W)
        return start, r - rs

    for half in range(rows_per_tile // NA_BATCH_ROWS):
        def na_scores(b, carry):
            i = half * NA_BATCH_ROWS + b
            start, d = band_start(i)
            q0 = pl.multiple_of(i * GRID_W, GRID_W)
            for p in range(n_pairs):
                cols = slice(p * LANES, (p + 1) * LANES)
                qpair = q_ref[pl.ds(q0, GRID_W), cols]
                zero = jnp.zeros_like(qpair)
                qs = jnp.concatenate([jnp.where(low_q, qpair, zero), jnp.where(low_q, zero, qpair)], axis=0)
                kb = kcat[pl.ds(start, band), cols]
                s = lax.dot_general(qs, kb, _NT_DIMS, preferred_element_type=_F32) + tbl_ref[d, p]
                s_s[pl.ds(pl.multiple_of((b * n_pairs + p) * chain, chain), chain), :] = s
            return carry

        def na_softmax(b, carry):
            rws = pl.ds(pl.multiple_of(b * n_pairs * chain, n_pairs * chain), n_pairs * chain)
            s = s_s[rws, :]
            e = jnp.exp2(s - jnp.max(s, axis=-1, keepdims=True))
            e_s[rws, :] = e.astype(_BF16)
            inv_s[rws, :] = jnp.broadcast_to(1.0 / jnp.sum(e, axis=-1, keepdims=True), (n_pairs * chain, LANES))
            return carry

        def na_values(b, carry):
            i = half * NA_BATCH_ROWS + b
            start, _ = band_start(i)
            q0 = pl.multiple_of(i * GRID_W, GRID_W)
            for p in range(n_pairs):
                cols = slice(p * LANES, (p + 1) * LANES)
                rws = pl.ds(pl.multiple_of((b * n_pairs + p) * chain, chain), chain)
                vb = vcat[pl.ds(start, band), cols]
                o = jnp.dot(e_s[rws, :], vb, preferred_element_type=_F32) * inv_s[rws, :]
                oa_s[pl.ds(q0, GRID_W), cols] = jnp.where(lane_o, o[:GRID_W], o[GRID_W:]).astype(_BF16)
            return carry

        lax.fori_loop(0, NA_BATCH_ROWS, na_scores, 0)
        lax.fori_loop(0, NA_BATCH_ROWS, na_softmax, 0)
        lax.fori_loop(0, NA_BATCH_ROWS, na_values, 0)

    kvcat[0:SWA_BLOCK] = kvp_ref[...]
    kvcat[SWA_BLOCK:SWA_BLOCK + TM_MIX] = kvc_ref[...]
    kvcat[SWA_BLOCK + TM_MIX:] = kvn_ref[...]

    n_slabs = SWA_Q_WIDTH // LANES
    stack = n_slabs * SWA_BLOCK
    lane_s = lax.broadcasted_iota(jnp.int32, (SWA_BLOCK, LANES), 1) < SWA_HEAD_DIM
    rowblk = lax.broadcasted_iota(jnp.int32, (stack, 1), 0) // SWA_BLOCK
    wband = 3 * SWA_BLOCK
    sinks = []
    for g in range(SWA_KV_HEADS):
        sink = jnp.zeros((stack, 1), _F32)
        for s_ in range(n_slabs):
            sink = jnp.where(rowblk == s_, sink_ref[SWA_KV_HEADS * s_ + g], sink)
        sinks.append(sink)

    def swa_block(n, carry):
        nb = j * (TM_MIX // SWA_BLOCK) + n
        t0 = pl.multiple_of(n * SWA_BLOCK, SWA_BLOCK)
        variant = jnp.where(nb == 0, 0, jnp.where(nb == n_swa_blocks - 1, 2, 1))
        mask = mask_ref[variant]
        mask = jnp.concatenate([mask] * n_slabs, axis=0)
        kband = kvcat[pl.ds(t0, wband), 0:LANES]
        vband = kvcat[pl.ds(t0, wband), LANES:2 * LANES]
        outs = []
        for g in range(SWA_KV_HEADS):
            parts = []
            for s_ in range(n_slabs):
                slab = qb_ref[pl.ds(t0, SWA_BLOCK), s_ * LANES:(s_ + 1) * LANES]
                zero = jnp.zeros_like(slab)
                parts.append(jnp.where(lane_s, slab, zero) if g == 0 else jnp.where(lane_s, zero, slab))
            qg = jnp.concatenate(parts, axis=0)
            s = lax.dot_general(qg, kband, _NT_DIMS, preferred_element_type=_F32) + mask
            m = jnp.maximum(jnp.max(s, axis=-1, keepdims=True), sinks[g])
            e = jnp.exp2(s - m)
            den = jnp.sum(e, axis=-1, keepdims=True) + jnp.exp2(sinks[g] - m)
            outs.append(jnp.dot(e.astype(_BF16), vband, preferred_element_type=_F32) * (1.0 / den))
        for s_ in range(n_slabs):
            rs_ = slice(s_ * SWA_BLOCK, (s_ + 1) * SWA_BLOCK)
            ob_s[pl.ds(t0, SWA_BLOCK), s_ * LANES:(s_ + 1) * LANES] = jnp.where(
                lane_s, outs[0][rs_], outs[1][rs_]).astype(_BF16)
        return carry

    lax.fori_loop(0, TM_MIX // SWA_BLOCK, swa_block, 0)

    ua = jnp.dot(oa_s[...], wua_ref[...], preferred_element_type=_F32)
    ub = jnp.dot(ob_s[...], wub_ref[...], preferred_element_type=_F32)
    merged = (gate_ref[:, 0:D_MODEL].astype(_F32) * ua
              + gate_ref[:, D_MODEL:].astype(_F32) * ub).astype(_BF16)
    x1 = x_ref[...] + jnp.dot(merged, wo_ref[...], preferred_element_type=_F32)
    x1_ref[...] = x1

    var = jnp.mean(x1 * x1, axis=-1, keepdims=True)
    h2 = x1 * lax.rsqrt(var + RMS_EPS) * g2_ref[...]
    h2_hi = h2.astype(_BF16)
    h2_ref[...] = _pack_rows(h2_hi[:, :HALF], h2_hi[:, HALF:])
    h2_lo = (h2 - h2_hi.astype(_F32)).astype(_BF16)
    both = lax.dot_general(wr_ref[...], h2_hi, _NT_DIMS, preferred_element_type=_F32)
    cross = lax.dot_general(wr_ref[0:N_EXPERTS], h2_lo, _NT_DIMS, preferred_element_type=_F32)
    logits = both[0:N_EXPERTS] + both[N_EXPERTS:] + cross + br_ref[...]
    eidx = lax.broadcasted_iota(jnp.int32, logits.shape, 0)
    vals, idxs, hots = [], [], []
    for _ in range(TOP_K):
        m = jnp.max(logits, axis=0, keepdims=True)
        idx = jnp.min(jnp.where(logits == m, eidx, N_EXPERTS), axis=0, keepdims=True)
        hot = eidx == idx
        logits = jnp.where(hot, -jnp.inf, logits)
        vals.append(m)
        idxs.append(idx)
        hots.append(hot)
    es = [jnp.exp(v - vals[0]) for v in vals]
    inv = 1.0 / (es[0] + es[1] + es[2] + es[3])
    sel = jnp.zeros(logits.shape, _F32)
    for hot in hots:
        sel = sel + jnp.where(hot, 1.0, 0.0)
    tile_idx = pl.program_id(0) * pl.num_programs(1) + j
    tok = tile_idx * TM_MIX + lax.broadcasted_iota(jnp.int32, (1, TM_MIX), 1)
    keys = [(idx << KEY_SHIFT) | (tok * TOP_K + k) for k, idx in enumerate(idxs)]
    key_ref[...] = jnp.concatenate(keys + [jnp.zeros((8 - TOP_K, TM_MIX), jnp.int32)], axis=0)
    wts_ref[...] = jnp.concatenate([e * inv for e in es] + [jnp.zeros((8 - TOP_K, TM_MIX), _F32)], axis=0)
    cnt_ref[...] = jnp.broadcast_to(jnp.sum(sel, axis=1, keepdims=True), (N_EXPERTS, LANES))


def _swa_mask_table():
    qpos = np.arange(SWA_BLOCK)[:, None]
    koff = np.arange(3 * SWA_BLOCK)[None, :] - SWA_BLOCK
    rel_ok = np.abs(koff - qpos) <= SWA_WINDOW
    has_prev = np.array([False, True, True])[:, None, None]
    has_next = np.array([True, True, False])[:, None, None]
    ok = rel_ok[None] & ((koff >= 0)[None] | has_prev) & ((koff < SWA_BLOCK)[None] | has_next)
    return jnp.asarray(np.where(ok, 0.0, NEG_BIG), _F32)


def _mixer(sinks_perm, x2, qkva, qb, kvb, gates, tbl, wua, wub, wo, g2, wr_t, br, batch, seq):
    t = x2.shape[0]
    tm = TM_MIX
    nj = seq // tm
    rows = seq // GRID_W
    hb = tm // ((NA_ROWS // 2) * GRID_W)
    sb = tm // SWA_BLOCK
    n_halo = seq // ((NA_ROWS // 2) * GRID_W)
    n_swa = seq // SWA_BLOCK
    halo = (NA_ROWS // 2) * GRID_W
    swa_mask = _swa_mask_table()
    score_rows = NA_BATCH_ROWS * NA_HEADS * GRID_W
    assert n_swa >= 2

    tile = lambda b, j, *_: (b * nj + j, 0)

    def na_spec(col, which):
        if which == 0:
            return pl.BlockSpec((tm, NA_WIDTH), lambda b, j, *_: (b * nj + j, col))
        if which < 0:
            return pl.BlockSpec((halo, NA_WIDTH),
                                lambda b, j, *_: (b * n_halo + jnp.maximum(j * hb - 1, 0), col))
        return pl.BlockSpec((halo, NA_WIDTH),
                            lambda b, j, *_: (b * n_halo + jnp.minimum(j * hb + hb, n_halo - 1), col))

    kv_prev = pl.BlockSpec((SWA_BLOCK, 2 * SWA_KV_WIDTH),
                           lambda b, j, *_: (b * n_swa + jnp.maximum(j * sb - 1, 0), 0))
    kv_next = pl.BlockSpec((SWA_BLOCK, 2 * SWA_KV_WIDTH),
                           lambda b, j, *_: (b * n_swa + jnp.minimum(j * sb + sb, n_swa - 1), 0))

    def cspec(shape):
        nd = len(shape)
        return pl.BlockSpec(shape, lambda *_: (0,) * nd, pipeline_mode=pl.Buffered(1))

    grid_spec = pltpu.PrefetchScalarGridSpec(
        num_scalar_prefetch=1,
        grid=(batch, nj),
        in_specs=[
            pl.BlockSpec((tm, D_MODEL), tile),
            na_spec(0, 0),
            na_spec(1, -1), na_spec(1, 0), na_spec(1, 1),
            na_spec(2, -1), na_spec(2, 0), na_spec(2, 1),
            pl.BlockSpec((tm, SWA_Q_WIDTH), tile),
            kv_prev, pl.BlockSpec((tm, 2 * SWA_KV_WIDTH), tile), kv_next,
            pl.BlockSpec((tm, 2 * D_MODEL), tile),
            cspec(tbl.shape), cspec(swa_mask.shape),
            cspec(wua.shape), cspec(wub.shape), cspec(wo.shape),
            cspec(g2.shape), cspec(wr_t.shape), cspec(br.shape),
        ],
        out_specs=[
            pl.BlockSpec((tm, D_MODEL), tile),
            pl.BlockSpec((tm, HALF), tile),
            pl.BlockSpec((8, tm), lambda b, j, *_: (0, b * nj + j)),
            pl.BlockSpec((8, tm), lambda b, j, *_: (0, b * nj + j)),
            pl.BlockSpec((N_EXPERTS, LANES), tile),
        ],
        scratch_shapes=[
            pltpu.VMEM((tm + 2 * halo, NA_WIDTH), _BF16),
            pltpu.VMEM((tm + 2 * halo, NA_WIDTH), _BF16),
            pltpu.VMEM((tm + 2 * SWA_BLOCK, 2 * SWA_KV_WIDTH), _BF16),
            pltpu.VMEM((tm, NA_WIDTH), _BF16),
            pltpu.VMEM((tm, SWA_Q_WIDTH), _BF16),
            pltpu.VMEM((score_rows, NA_ROWS * GRID_W), _F32),
            pltpu.VMEM((score_rows, NA_ROWS * GRID_W), _BF16),
            pltpu.VMEM((score_rows, LANES), _F32),
        ],
    )
    return pl.pallas_call(
        functools.partial(_mixer_kernel, rows=rows, n_swa_blocks=n_swa),
        grid_spec=grid_spec,
        out_shape=[
            jax.ShapeDtypeStruct((t, D_MODEL), _F32),
            jax.ShapeDtypeStruct((t, HALF), jnp.uint32),
            jax.ShapeDtypeStruct((8, t), jnp.int32),
            jax.ShapeDtypeStruct((8, t), _F32),
            jax.ShapeDtypeStruct((t // tm * N_EXPERTS, LANES), _F32),
        ],
        compiler_params=pltpu.CompilerParams(
            dimension_semantics=("parallel", "parallel"), vmem_limit_bytes=VMEM_LIMIT),
        name="mixer",
    )(sinks_perm, x2, qkva, qkva, qkva, qkva, qkva, qkva, qkva, qb, kvb, kvb, kvb, gates, tbl, swa_mask,
      wua, wub, wo, g2, wr_t, br)


def _expert_kernel(bexp_ref, nact_ref, gidx_hbm, sidx_hbm, h2_hbm, w1_ref, b1_ref, w2_ref, b2_ref,
                   y_hbm, gs0, gs1, ss0, ss1, xbuf0, xbuf1, ybuf0, ybuf1, w1b, w2b, isem, gsem, ssem,
                   *, n_blocks, n_real_rows):
    i = pl.program_id(0)
    nact = nact_ref[0]
    eb = EXPERT_BLOCK
    gs, ss, xbuf, ybuf = (gs0, gs1), (ss0, ss1), (xbuf0, xbuf1), (ybuf0, ybuf1)
    rows = pl.ds(0, eb)

    def gidx_copy(blk, slot):
        return pltpu.make_async_copy(gidx_hbm.at[blk], gs[slot], isem.at[0, slot])

    def sidx_copy(entry, slot):
        return pltpu.make_async_copy(sidx_hbm.at[entry], ss[slot], isem.at[1, slot])

    def gather_row(slot, r, priority=0):
        pltpu.make_async_copy(h2_hbm.at[pl.ds(gs[slot][r], 1), :], xbuf[slot].at[pl.ds(r, 1), :],
                              gsem.at[slot]).start(priority=priority)

    def wait_gather(slot):
        pltpu.make_async_copy(h2_hbm.at[rows, :], xbuf[slot].at[rows, :], gsem.at[slot]).wait()

    def scatter_row(slot, r, priority=0):
        pltpu.make_async_copy(ybuf[slot].at[pl.ds(r, 1), :], y_hbm.at[pl.ds(ss[slot][r], 1), :],
                              ssem.at[slot]).start(priority=priority)

    def wait_scatter(slot):
        pltpu.make_async_copy(ybuf[slot].at[rows, :], y_hbm.at[rows, :], ssem.at[slot]).wait()

    def rolled(fn, slot):
        def body(r, c):
            fn(slot, r)
            return c
        lax.fori_loop(0, eb, body, 0)

    @pl.when(i == 0)
    def _():
        gidx_copy(0, 0).start()
        gidx_copy(0, 0).wait()
        rolled(gather_row, 0)
        gidx_copy(jnp.minimum(1, n_blocks - 1), 1).start()
        sidx_copy(0, 1).start()
        ybuf1[...] = jnp.zeros(ybuf1.shape, jnp.uint32)
        init = pltpu.make_async_copy(ybuf1.at[rows, :], y_hbm.at[pl.ds(n_real_rows, eb), :], ssem.at[0])
        init.start()
        init.wait()

    def step(cur):
        nxt = 1 - cur
        gidx_copy(jnp.minimum(i + 2, n_blocks - 1), cur).start()
        sidx_copy(i + 1, cur).start()
        gidx_copy(0, nxt).wait()
        sidx_copy(0, nxt).wait()
        wait_gather(cur)

        x_lo, x_hi = (half.astype(_BF16) for half in _unpack_rows(xbuf[cur][rows, :]))

        def x_dot(c0):
            return (jnp.dot(x_lo, w1b[0:HALF, c0:c0 + FF_CHUNK], preferred_element_type=_F32)
                    + jnp.dot(x_hi, w1b[HALF:, c0:c0 + FF_CHUNK], preferred_element_type=_F32))

        n_chunks = D_FF // FF_CHUNK
        rows_per_group = 2 * eb // n_chunks
        spare = pl.ds(pl.multiple_of(eb + jnp.minimum(i, 0) * SUBLANES, SUBLANES), SUBLANES)
        y = None
        for jc in range(n_chunks):
            issue = gather_row if jc < n_chunks // 2 else scatter_row
            first = (jc % (n_chunks // 2)) * rows_per_group
            for r in range(first, first + rows_per_group):
                issue(nxt, r, r % 2)
            xbuf[nxt][spare, 0:FF_CHUNK] = jnp.zeros((SUBLANES, FF_CHUNK), jnp.uint32)
            ybuf[nxt][spare, 0:FF_CHUNK] = jnp.zeros((SUBLANES, FF_CHUNK), jnp.uint32)
            anchor = lax.bitcast_convert_type(
                (xbuf[nxt][spare, 0:FF_CHUNK] | ybuf[nxt][spare, 0:FF_CHUNK])[0:1], _F32)
            c0 = jc * FF_CHUNK
            g = x_dot(c0) + (b1_ref[:, c0:c0 + FF_CHUNK] + anchor)
            u = x_dot(D_FF + c0) + b1_ref[:, D_FF + c0:D_FF + c0 + FF_CHUNK]
            gate = jnp.minimum(g, SWIGLU_LIMIT)
            up = jnp.clip(u, -SWIGLU_LIMIT, SWIGLU_LIMIT)
            act = ((up + 1.0) * (gate * jax.nn.sigmoid(gate * SWIGLU_ALPHA))).astype(_BF16)
            part = jnp.dot(act, w2b[c0:c0 + FF_CHUNK, :], preferred_element_type=_F32)
            y = part + b2_ref[...] if y is None else y + part

        @pl.when(i >= 1)
        def _():
            wait_scatter(cur)

        ybuf[cur][rows, :] = _pack_rows(y[:, :HALF].astype(_BF16), y[:, HALF:].astype(_BF16))

        @pl.when(i == nact - 1)
        def _():
            sidx_copy(0, cur).wait()
            rolled(scatter_row, cur)
            wait_scatter(nxt)
            wait_scatter(cur)
            wait_gather(nxt)
            gidx_copy(0, cur).wait()

    new_expert = jnp.logical_or(i == 0, bexp_ref[i] != bexp_ref[jnp.maximum(i - 1, 0)])

    @pl.when(jnp.logical_and(i < nact, new_expert))
    def _():
        w1b[...] = w1_ref[...].astype(_BF16)
        w2b[...] = w2_ref[...].astype(_BF16)

    for parity in range(2):
        @pl.when(jnp.logical_and(i < nact, i % 2 == parity))
        def _():
            step(parity)


def _experts(blk_exp, n_active, gidx, sidx, h2, w1, b1, w2, b2, n_tokens):
    n_blk = gidx.shape[0]
    eb = EXPERT_BLOCK
    n_real = TOP_K * n_tokens
    wmap = lambda i, be, na: (be[i], 0, 0)
    grid_spec = pltpu.PrefetchScalarGridSpec(
        num_scalar_prefetch=2,
        grid=(n_blk,),
        in_specs=[
            pl.BlockSpec(memory_space=pl.ANY),
            pl.BlockSpec(memory_space=pl.ANY),
            pl.BlockSpec(memory_space=pl.ANY),
            pl.BlockSpec((None, D_MODEL, 2 * D_FF), wmap),
            pl.BlockSpec((None, 1, 2 * D_FF), wmap),
            pl.BlockSpec((None, D_FF, D_MODEL), wmap),
            pl.BlockSpec((None, 1, D_MODEL), wmap),
        ],
        out_specs=pl.BlockSpec(memory_space=pl.ANY),
        scratch_shapes=[
            pltpu.SMEM((eb,), jnp.int32), pltpu.SMEM((eb,), jnp.int32),
            pltpu.SMEM((eb,), jnp.int32), pltpu.SMEM((eb,), jnp.int32),
            pltpu.VMEM((eb + SUBLANES, HALF), jnp.uint32),
            pltpu.VMEM((eb + SUBLANES, HALF), jnp.uint32),
            pltpu.VMEM((eb + SUBLANES, HALF), jnp.uint32),
            pltpu.VMEM((eb + SUBLANES, HALF), jnp.uint32),
            pltpu.VMEM((D_MODEL, 2 * D_FF), _BF16),
            pltpu.VMEM((D_FF, D_MODEL), _BF16),
            pltpu.SemaphoreType.DMA((2, 2)),
            pltpu.SemaphoreType.DMA((2,)),
            pltpu.SemaphoreType.DMA((2,)),
        ],
    )
    return pl.pallas_call(
        functools.partial(_expert_kernel, n_blocks=n_blk, n_real_rows=n_real),
        grid_spec=grid_spec,
        out_shape=jax.ShapeDtypeStruct((n_real + 2 * eb, HALF), jnp.uint32),
        compiler_params=pltpu.CompilerParams(
            dimension_semantics=("arbitrary",), vmem_limit_bytes=VMEM_LIMIT),
        name="experts",
    )(blk_exp, n_active, gidx, sidx, h2, w1, b1, w2, b2)


def _combine_kernel(x1_ref, y0_ref, y1_ref, y2_ref, y3_ref, wts_ref, g_ref, o_ref):
    wt = wts_ref[...].T
    y = x1_ref[...]
    for k, y_ref in enumerate((y0_ref, y1_ref, y2_ref, y3_ref)):
        y = y + wt[:, k:k + 1] * jnp.concatenate(_unpack_rows(y_ref[...]), axis=1)
    var = jnp.mean(y * y, axis=-1, keepdims=True)
    o_ref[...] = y * lax.rsqrt(var + RMS_EPS) * g_ref[...]


def _combine(x1, y, wts, gf):
    t = x1.shape[0]
    tm = TM_OUT
    nt = t // tm
    yspec = lambda k: pl.BlockSpec((tm, HALF), lambda i: (k * nt + i, 0))
    return pl.pallas_call(
        _combine_kernel,
        grid=(nt,),
        in_specs=[
            pl.BlockSpec((tm, D_MODEL), lambda i: (i, 0)),
            yspec(0), yspec(1), yspec(2), yspec(3),
            pl.BlockSpec((8, tm), lambda i: (0, i)),
            _const_spec((1, D_MODEL)),
        ],
        out_specs=pl.BlockSpec((tm, D_MODEL), lambda i: (i, 0)),
        out_shape=jax.ShapeDtypeStruct((t, D_MODEL), _F32),
        compiler_params=pltpu.CompilerParams(
            dimension_semantics=("parallel",), vmem_limit_bytes=VMEM_LIMIT),
        name="combine",
    )(x1, y, y, y, y, wts, gf)


def _split_bf16(w):
    hi = w.astype(_BF16)
    lo = (w - hi.astype(_F32)).astype(_BF16)
    return jnp.concatenate([hi, lo], axis=0)


def _rope_tables(seq):
    half = SWA_HEAD_DIM // 2
    inv_freq = ROPE_THETA ** (-jnp.arange(half, dtype=_F32) / half)
    ang = jnp.arange(seq, dtype=_F32)[:, None] * inv_freq[None, :]
    cos, sin = jnp.cos(ang), jnp.sin(ang)
    cos_t = jnp.tile(jnp.concatenate([cos, cos], axis=1), (1, LANES // SWA_HEAD_DIM))
    sin_t = jnp.tile(jnp.concatenate([-sin, sin], axis=1), (1, LANES // SWA_HEAD_DIM))
    return cos_t, sin_t


def _na_bias_table(rpb):
    col = np.arange(GRID_W)
    cstart = np.clip(col - NA_COLS // 2, 0, GRID_W - NA_COLS)
    kc = np.arange(GRID_W)
    valid = (kc[None, :] >= cstart[:, None]) & (kc[None, :] < cstart[:, None] + NA_COLS)
    off = np.clip(kc[None, :] - col[:, None] + NA_COLS - 1, 0, 2 * NA_COLS - 2)
    ext = jnp.where(valid[None, None], rpb[:, :, off], NEG_BIG)
    u = np.arange(NA_ROWS)[None, :] - np.arange(NA_ROWS)[:, None] + NA_ROWS - 1
    tbl = ext[:, u]
    tbl = tbl.transpose(1, 0, 3, 2, 4).reshape(NA_ROWS, NA_HEADS // 2, 2 * GRID_W, NA_ROWS * GRID_W)
    return (tbl * LOG2E).astype(_F32)


def kernel(x, norm1_g, w_in, b_in, na_rpb, swa_sinks, w_up_a, w_up_b, w_out, norm2_g, w_router,
           b_router, w1, b1, w2, b2, final_g):
    batch, seq, d = x.shape
    depth = w_in.shape[0]
    t = batch * seq
    assert depth == 1, "the final norm is fused into the single layer's combine step"
    assert d == D_MODEL and seq % TM_MIX == 0 and seq // GRID_W >= 2 * NA_ROWS and t % TM_OUT == 0

    group = SWA_Q_HEADS // SWA_KV_HEADS
    head_order = np.arange(SWA_Q_HEADS).reshape(SWA_KV_HEADS, group).T.reshape(-1)

    def reorder_heads(a, axis, start):
        take = lambda lo, hi: lax.slice_in_dim(a, lo, hi, axis=axis)
        heads = [take(start + h * SWA_HEAD_DIM, start + (h + 1) * SWA_HEAD_DIM) for h in head_order]
        return jnp.concatenate([take(0, start)] + heads + [take(start + SWA_Q_WIDTH, a.shape[axis])], axis=axis)

    cos_t, sin_t = _rope_tables(seq)

    n_assign = t * TOP_K
    assert n_assign <= PAD_FLAG
    n_rows = n_assign + N_EXPERTS * EXPERT_BLOCK
    n_blk = n_rows // EXPERT_BLOCK
    n_tiles = t // TM_MIX

    x2 = x.reshape(t, d)
    for l in range(depth):
        w_in_l = reorder_heads(w_in[l], 1, C_QB).astype(_BF16)
        b_in_l = reorder_heads(b_in[l], 0, C_QB).reshape(1, D_IN)
        qkva, qb, kvb, gates = _inproj(x2, norm1_g[l].reshape(1, d), w_in_l, b_in_l, cos_t, sin_t, seq)

        x1, h2, keys, wts, cnt = _mixer(
            swa_sinks[l][head_order].astype(_F32) * LOG2E, x2, qkva, qb, kvb, gates, _na_bias_table(na_rpb[l]),
            w_up_a[l].astype(_BF16), reorder_heads(w_up_b[l], 0, 0).astype(_BF16), w_out[l].astype(_BF16),
            norm2_g[l].reshape(1, d), _split_bf16(w_router[l].T), b_router[l].reshape(N_EXPERTS, 1),
            batch, seq)

        cnt = cnt.reshape(n_tiles, N_EXPERTS, LANES)[:, :, 0].astype(jnp.int32)
        counts = jnp.sum(cnt, axis=0)
        padded = (counts + EXPERT_BLOCK - 1) // EXPERT_BLOCK * EXPERT_BLOCK
        pend = jnp.cumsum(padded)
        pad_i = jnp.arange(EXPERT_BLOCK, dtype=jnp.int32)[None, :]
        pad_keys = jnp.where(pad_i < (padded - counts)[:, None],
                             (jnp.arange(N_EXPERTS, dtype=jnp.int32)[:, None] << KEY_SHIFT) | PAD_FLAG | pad_i,
                             jnp.iinfo(jnp.int32).max)
        sorted_keys = jnp.sort(jnp.concatenate([keys[:TOP_K].reshape(-1), pad_keys.reshape(-1)]))
        row_a = jnp.where((sorted_keys & PAD_FLAG) == 0, sorted_keys & (PAD_FLAG - 1), -1)
        row_a = row_a.reshape(n_blk, EXPERT_BLOCK)
        blk_start = jnp.arange(n_blk, dtype=jnp.int32) * EXPERT_BLOCK
        blk_exp = jnp.minimum(jnp.sum((pend[None, :] <= blk_start[:, None]).astype(jnp.int32), axis=1),
                              N_EXPERTS - 1)
        n_active = (pend[-1:] // EXPERT_BLOCK).astype(jnp.int32)
        r_in_blk = jnp.arange(EXPERT_BLOCK, dtype=jnp.int32)[None, :]
        parity = (jnp.arange(-1, n_blk, dtype=jnp.int32) % 2)[:, None]
        trash = n_assign + parity * EXPERT_BLOCK + r_in_blk
        gidx = jnp.where(row_a >= 0, row_a >> 2, 0)
        sidx = jnp.where(row_a >= 0, (row_a & (TOP_K - 1)) * t + (row_a >> 2), trash[1:])
        sidx = jnp.concatenate([trash[:1], sidx], axis=0)

        y = _experts(blk_exp, n_active, gidx, sidx, h2,
                     w1[l], b1[l].reshape(N_EXPERTS, 1, 2 * D_FF),
                     w2[l], b2[l].reshape(N_EXPERTS, 1, D_MODEL), t)
        x2 = _combine(x1, y, wts, final_g.reshape(1, d))
    return x2.reshape(batch, seq, d)
```

```python
import functools

import jax
import jax.numpy as jnp
import numpy as np
from jax import lax
from jax.experimental import pallas as pl
from jax.experimental.pallas import tpu as pltpu

D_MODEL = 1024
GRID_W = 64
NA_HEADS = 8
NA_HEAD_DIM = 64
NA_ROWS = 8
NA_COLS = 16
SWA_Q_HEADS = 8
SWA_KV_HEADS = 2
SWA_HEAD_DIM = 64
SWA_WINDOW = 128
SWA_BLOCK = 128
ROPE_THETA = 10000.0
N_EXPERTS = 32
TOP_K = 4
D_FF = 1024
SWIGLU_LIMIT = 7.0
SWIGLU_ALPHA = 1.702
EXPERT_BLOCK = 512
RMS_EPS = 1e-5

NA_WIDTH = NA_HEADS * NA_HEAD_DIM
SWA_Q_WIDTH = SWA_Q_HEADS * SWA_HEAD_DIM
SWA_KV_WIDTH = SWA_KV_HEADS * SWA_HEAD_DIM
LANES = 128
SUBLANES = 8
NEG_BIG = -1e30

C_QA, C_KA, C_VA = 0, NA_WIDTH, 2 * NA_WIDTH
C_QB = 3 * NA_WIDTH
C_KB = C_QB + SWA_Q_WIDTH
C_VB = C_KB + SWA_KV_WIDTH
C_GA = C_VB + SWA_KV_WIDTH
C_GB = C_GA + D_MODEL
D_IN = C_GB + D_MODEL

TM_PROJ = 512
TM_MIX = 512
TM_OUT = 512
FF_CHUNK = 256
NA_BATCH_ROWS = 4
LOG2E = 1.4426950408889634
KEY_SHIFT = 18
PAD_FLAG = 1 << (KEY_SHIFT - 1)
VMEM_LIMIT = 56 * 1024 * 1024

_BF16 = jnp.bfloat16
_F32 = jnp.float32


def _const_spec(shape):
    nd = len(shape)
    return pl.BlockSpec(shape, lambda *_: (0,) * nd, pipeline_mode=pl.Buffered(1))


HALF = D_MODEL // 2
_HI_MASK = 0xFFFF0000


def _pack_rows(lo_bf16, hi_bf16):
    lo = lax.bitcast_convert_type(lo_bf16.astype(_F32), jnp.uint32) >> 16
    hi = lax.bitcast_convert_type(hi_bf16.astype(_F32), jnp.uint32) & jnp.uint32(_HI_MASK)
    return lo | hi


def _unpack_rows(u):
    lo = lax.bitcast_convert_type(u << 16, _F32)
    hi = lax.bitcast_convert_type(u & jnp.uint32(_HI_MASK), _F32)
    return lo, hi


def _rope_slab(y, cos, sin_signed):
    lane = lax.broadcasted_iota(jnp.int32, y.shape, 1)
    first_half = (lane & (SWA_HEAD_DIM - 1)) < (SWA_HEAD_DIM // 2)
    rot = jnp.where(first_half, pltpu.roll(y, LANES - SWA_HEAD_DIM // 2, axis=1),
                    pltpu.roll(y, SWA_HEAD_DIM // 2, axis=1))
    return y * cos + rot * sin_signed


def _inproj_kernel(x_ref, g_ref, w_ref, b_ref, cos_ref, sin_ref,
                   qkva_ref, qb_ref, kvb_ref, gate_ref):
    x = x_ref[...]
    var = jnp.mean(x * x, axis=-1, keepdims=True)
    h = (x * lax.rsqrt(var + RMS_EPS) * g_ref[...]).astype(_BF16)

    def proj(c0, c1):
        return jnp.dot(h, w_ref[:, c0:c1], preferred_element_type=_F32) + b_ref[:, c0:c1]

    scale = NA_HEAD_DIM ** -0.5 * LOG2E
    qkva_ref[:, C_QA:C_KA] = (proj(C_QA, C_KA) * scale).astype(_BF16)
    qkva_ref[:, C_KA:C_VA] = proj(C_KA, C_VA).astype(_BF16)
    qkva_ref[:, C_VA:C_QB] = proj(C_VA, C_QB).astype(_BF16)

    cos = cos_ref[...]
    sin = sin_ref[...]
    qb = proj(C_QB, C_KB)
    for s in range(SWA_Q_WIDTH // LANES):
        slab = _rope_slab(qb[:, s * LANES:(s + 1) * LANES], cos, sin)
        qb_ref[:, s * LANES:(s + 1) * LANES] = (slab * (SWA_HEAD_DIM ** -0.5 * LOG2E)).astype(_BF16)
    kvb = proj(C_KB, C_GA)
    kvb_ref[:, 0:LANES] = _rope_slab(kvb[:, 0:LANES], cos, sin).astype(_BF16)
    kvb_ref[:, LANES:2 * LANES] = kvb[:, LANES:2 * LANES].astype(_BF16)

    for c0 in range(C_GA, D_IN, 512):
        gate_ref[:, c0 - C_GA:c0 - C_GA + 512] = jax.nn.sigmoid(proj(c0, c0 + 512)).astype(_BF16)


def _inproj(x2, g1, w_in, b_in, cos_t, sin_t, seq):
    t = x2.shape[0]
    tm = TM_PROJ
    nseq = seq // tm
    row = lambda i: (i, 0)
    return pl.pallas_call(
        _inproj_kernel,
        grid=(t // tm,),
        in_specs=[
            pl.BlockSpec((tm, D_MODEL), row),
            _const_spec((1, D_MODEL)),
            _const_spec((D_MODEL, D_IN)),
            _const_spec((1, D_IN)),
            pl.BlockSpec((tm, LANES), lambda i: (i % nseq, 0)),
            pl.BlockSpec((tm, LANES), lambda i: (i % nseq, 0)),
        ],
        out_specs=[
            pl.BlockSpec((tm, 3 * NA_WIDTH), row),
            pl.BlockSpec((tm, SWA_Q_WIDTH), row),
            pl.BlockSpec((tm, 2 * SWA_KV_WIDTH), row),
            pl.BlockSpec((tm, 2 * D_MODEL), row),
        ],
        out_shape=[
            jax.ShapeDtypeStruct((t, 3 * NA_WIDTH), _BF16),
            jax.ShapeDtypeStruct((t, SWA_Q_WIDTH), _BF16),
            jax.ShapeDtypeStruct((t, 2 * SWA_KV_WIDTH), _BF16),
            jax.ShapeDtypeStruct((t, 2 * D_MODEL), _BF16),
        ],
        compiler_params=pltpu.CompilerParams(
            dimension_semantics=("parallel",), vmem_limit_bytes=VMEM_LIMIT),
        name="inproj",
    )(x2, g1, w_in, b_in, cos_t, sin_t)


_NT_DIMS = (((1,), (1,)), ((), ()))


def _mixer_kernel(sink_ref, x_ref, q_ref, kp_ref, kc_ref, kn_ref, vp_ref, vc_ref, vn_ref,
                  qb_ref, kvp_ref, kvc_ref, kvn_ref, gate_ref, tbl_ref, mask_ref,
                  wua_ref, wub_ref, wo_ref, g2_ref, wr_ref, br_ref,
                  x1_ref, h2_ref, key_ref, wts_ref, cnt_ref,
                  kcat, vcat, kvcat, oa_s, ob_s, s_s, e_s, inv_s, *, rows, n_swa_blocks):
    j = pl.program_id(1)
    rows_per_tile = TM_MIX // GRID_W
    halo = (NA_ROWS // 2) * GRID_W
    band = NA_ROWS * GRID_W

    kcat[0:halo] = kp_ref[...]
    kcat[halo:halo + TM_MIX] = kc_ref[...]
    kcat[halo + TM_MIX:] = kn_ref[...]
    vcat[0:halo] = vp_ref[...]
    vcat[halo:halo + TM_MIX] = vc_ref[...]
    vcat[halo + TM_MIX:] = vn_ref[...]

    lane_q = lax.broadcasted_iota(jnp.int32, (GRID_W, LANES), 1)
    low_q = lane_q < NA_HEAD_DIM
    lane_o = lax.broadcasted_iota(jnp.int32, (GRID_W, LANES), 1) < NA_HEAD_DIM

    n_pairs = NA_HEADS // 2
    chain = 2 * GRID_W

    def band_start(i):
        r = j * rows_per_tile + i
        rs = jnp.clip(r - NA_ROWS // 2, 0, rows - NA_ROWS)
        start = pl.multiple_of((rs - (j * rows_per_tile - NA_ROWS // 2)) * GRID_W, GRID_W)
        return start, r - rs

    for half in range(rows_per_tile // NA_BATCH_ROWS):
        def na_scores(b, carry):
            i = half * NA_BATCH_ROWS + b
            start, d = band_start(i)
            q0 = pl.multiple_of(i * GRID_W, GRID_W)
            for p in range(n_pairs):
                cols = slice(p * LANES, (p + 1) * LANES)
                qpair = q_ref[pl.ds(q0, GRID_W), cols]
                zero = jnp.zeros_like(qpair)
                qs = jnp.concatenate([jnp.where(low_q, qpair, zero), jnp.where(low_q, zero, qpair)], axis=0)
                kb = kcat[pl.ds(start, band), cols]
                s = lax.dot_general(qs, kb, _NT_DIMS, preferred_element_type=_F32) + tbl_ref[d, p]
                s_s[pl.ds(pl.multiple_of((b * n_pairs + p) * chain, chain), chain), :] = s
            return carry

        def na_softmax(b, carry):
            rws = pl.ds(pl.multiple_of(b * n_pairs * chain, n_pairs * chain), n_pairs * chain)
            s = s_s[rws, :]
            e = jnp.exp2(s - jnp.max(s, axis=-1, keepdims=True))
            e_s[rws, :] = e.astype(_BF16)
            inv_s[rws, :] = jnp.broadcast_to(1.0 / jnp.sum(e, axis=-1, keepdims=True), (n_pairs * chain, LANES))
            return carry

        def na_values(b, carry):
            i = half * NA_BATCH_ROWS + b
            start, _ = band_start(i)
            q0 = pl.multiple_of(i * GRID_W, GRID_W)
            for p in range(n_pairs):
                cols = slice(p * LANES, (p + 1) * LANES)
                rws = pl.ds(pl.multiple_of((b * n_pairs + p) * chain, chain), chain)
                vb = vcat[pl.ds(start, band), cols]
                o = jnp.dot(e_s[rws, :], vb, preferred_element_type=_F32) * inv_s[rws, :]
                oa_s[pl.ds(q0, GRID_W), cols] = jnp.where(lane_o, o[:GRID_W], o[GRID_W:]).astype(_BF16)
            return carry

        lax.fori_loop(0, NA_BATCH_ROWS, na_scores, 0)
        lax.fori_loop(0, NA_BATCH_ROWS, na_softmax, 0)
        lax.fori_loop(0, NA_BATCH_ROWS, na_values, 0)

    kvcat[0:SWA_BLOCK] = kvp_ref[...]
    kvcat[SWA_BLOCK:SWA_BLOCK + TM_MIX] = kvc_ref[...]
    kvcat[SWA_BLOCK + TM_MIX:] = kvn_ref[...]

    n_slabs = SWA_Q_WIDTH // LANES
    stack = n_slabs * SWA_BLOCK
    lane_s = lax.broadcasted_iota(jnp.int32, (SWA_BLOCK, LANES), 1) < SWA_HEAD_DIM
    rowblk = lax.broadcasted_iota(jnp.int32, (stack, 1), 0) // SWA_BLOCK
    wband = 3 * SWA_BLOCK
    sinks = []
    for g in range(SWA_KV_HEADS):
        sink = jnp.zeros((stack, 1), _F32)
        for s_ in range(n_slabs):
            sink = jnp.where(rowblk == s_, sink_ref[SWA_KV_HEADS * s_ + g], sink)
        sinks.append(sink)

    def swa_block(n, carry):
        nb = j * (TM_MIX // SWA_BLOCK) + n
        t0 = pl.multiple_of(n * SWA_BLOCK, SWA_BLOCK)
        variant = jnp.where(nb == 0, 0, jnp.where(nb == n_swa_blocks - 1, 2, 1))
        mask = mask_ref[variant]
        mask = jnp.concatenate([mask] * n_slabs, axis=0)
        kband = kvcat[pl.ds(t0, wband), 0:LANES]
        vband = kvcat[pl.ds(t0, wband), LANES:2 * LANES]
        outs = []
        for g in range(SWA_KV_HEADS):
            parts = []
            for s_ in range(n_slabs):
                slab = qb_ref[pl.ds(t0, SWA_BLOCK), s_ * LANES:(s_ + 1) * LANES]
                zero = jnp.zeros_like(slab)
                parts.append(jnp.where(lane_s, slab, zero) if g == 0 else jnp.where(lane_s, zero, slab))
            qg = jnp.concatenate(parts, axis=0)
            s = lax.dot_general(qg, kband, _NT_DIMS, preferred_element_type=_F32) + mask
            m = jnp.maximum(jnp.max(s, axis=-1, keepdims=True), sinks[g])
            e = jnp.exp2(s - m)
            den = jnp.sum(e, axis=-1, keepdims=True) + jnp.exp2(sinks[g] - m)
            outs.append(jnp.dot(e.astype(_BF16), vband, preferred_element_type=_F32) * (1.0 / den))
        for s_ in range(n_slabs):
            rs_ = slice(s_ * SWA_BLOCK, (s_ + 1) * SWA_BLOCK)
            ob_s[pl.ds(t0, SWA_BLOCK), s_ * LANES:(s_ + 1) * LANES] = jnp.where(
                lane_s, outs[0][rs_], outs[1][rs_]).astype(_BF16)
        return carry

    lax.fori_loop(0, TM_MIX // SWA_BLOCK, swa_block, 0)

    ua = jnp.dot(oa_s[...], wua_ref[...], preferred_element_type=_F32)
    ub = jnp.dot(ob_s[...], wub_ref[...], preferred_element_type=_F32)
    merged = (gate_ref[:, 0:D_MODEL].astype(_F32) * ua
              + gate_ref[:, D_MODEL:].astype(_F32) * ub).astype(_BF16)
    x1 = x_ref[...] + jnp.dot(merged, wo_ref[...], preferred_element_type=_F32)
    x1_ref[...] = x1

    var = jnp.mean(x1 * x1, axis=-1, keepdims=True)
    h2 = x1 * lax.rsqrt(var + RMS_EPS) * g2_ref[...]
    h2_hi = h2.astype(_BF16)
    h2_ref[...] = _pack_rows(h2_hi[:, :HALF], h2_hi[:, HALF:])
    h2_lo = (h2 - h2_hi.astype(_F32)).astype(_BF16)
    both = lax.dot_general(wr_ref[...], h2_hi, _NT_DIMS, preferred_element_type=_F32)
    cross = lax.dot_general(wr_ref[0:N_EXPERTS], h2_lo, _NT_DIMS, preferred_element_type=_F32)
    logits = both[0:N_EXPERTS] + both[N_EXPERTS:] + cross + br_ref[...]
    eidx = lax.broadcasted_iota(jnp.int32, logits.shape, 0)
    vals, idxs, hots = [], [], []
    for _ in range(TOP_K):
        m = jnp.max(logits, axis=0, keepdims=True)
        idx = jnp.min(jnp.where(logits == m, eidx, N_EXPERTS), axis=0, keepdims=True)
        hot = eidx == idx
        logits = jnp.where(hot, -jnp.inf, logits)
        vals.append(m)
        idxs.append(idx)
        hots.append(hot)
    es = [jnp.exp(v - vals[0]) for v in vals]
    inv = 1.0 / (es[0] + es[1] + es[2] + es[3])
    sel = jnp.zeros(logits.shape, _F32)
    for hot in hots:
        sel = sel + jnp.where(hot, 1.0, 0.0)
    tile_idx = pl.program_id(0) * pl.num_programs(1) + j
    tok = tile_idx * TM_MIX + lax.broadcasted_iota(jnp.int32, (1, TM_MIX), 1)
    keys = [(idx << KEY_SHIFT) | (tok * TOP_K + k) for k, idx in enumerate(idxs)]
    key_ref[...] = jnp.concatenate(keys + [jnp.zeros((8 - TOP_K, TM_MIX), jnp.int32)], axis=0)
    wts_ref[...] = jnp.concatenate([e * inv for e in es] + [jnp.zeros((8 - TOP_K, TM_MIX), _F32)], axis=0)
    cnt_ref[...] = jnp.broadcast_to(jnp.sum(sel, axis=1, keepdims=True), (N_EXPERTS, LANES))


def _swa_mask_table():
    qpos = np.arange(SWA_BLOCK)[:, None]
    koff = np.arange(3 * SWA_BLOCK)[None, :] - SWA_BLOCK
    rel_ok = np.abs(koff - qpos) <= SWA_WINDOW
    has_prev = np.array([False, True, True])[:, None, None]
    has_next = np.array([True, True, False])[:, None, None]
    ok = rel_ok[None] & ((koff >= 0)[None] | has_prev) & ((koff < SWA_BLOCK)[None] | has_next)
    return jnp.asarray(np.where(ok, 0.0, NEG_BIG), _F32)


def _mixer(sinks_perm, x2, qkva, qb, kvb, gates, tbl, wua, wub, wo, g2, wr_t, br, batch, seq):
    t = x2.shape[0]
    tm = TM_MIX
    nj = seq // tm
    rows = seq // GRID_W
    hb = tm // ((NA_ROWS // 2) * GRID_W)
    sb = tm // SWA_BLOCK
    n_halo = seq // ((NA_ROWS // 2) * GRID_W)
    n_swa = seq // SWA_BLOCK
    halo = (NA_ROWS // 2) * GRID_W
    swa_mask = _swa_mask_table()
    score_rows = NA_BATCH_ROWS * NA_HEADS * GRID_W
    assert n_swa >= 2

    tile = lambda b, j, *_: (b * nj + j, 0)

    def na_spec(col, which):
        if which == 0:
            return pl.BlockSpec((tm, NA_WIDTH), lambda b, j, *_: (b * nj + j, col))
        if which < 0:
            return pl.BlockSpec((halo, NA_WIDTH),
                                lambda b, j, *_: (b * n_halo + jnp.maximum(j * hb - 1, 0), col))
        return pl.BlockSpec((halo, NA_WIDTH),
                            lambda b, j, *_: (b * n_halo + jnp.minimum(j * hb + hb, n_halo - 1), col))

    kv_prev = pl.BlockSpec((SWA_BLOCK, 2 * SWA_KV_WIDTH),
                           lambda b, j, *_: (b * n_swa + jnp.maximum(j * sb - 1, 0), 0))
    kv_next = pl.BlockSpec((SWA_BLOCK, 2 * SWA_KV_WIDTH),
                           lambda b, j, *_: (b * n_swa + jnp.minimum(j * sb + sb, n_swa - 1), 0))

    def cspec(shape):
        nd = len(shape)
        return pl.BlockSpec(shape, lambda *_: (0,) * nd, pipeline_mode=pl.Buffered(1))

    grid_spec = pltpu.PrefetchScalarGridSpec(
        num_scalar_prefetch=1,
        grid=(batch, nj),
        in_specs=[
            pl.BlockSpec((tm, D_MODEL), tile),
            na_spec(0, 0),
            na_spec(1, -1), na_spec(1, 0), na_spec(1, 1),
            na_spec(2, -1), na_spec(2, 0), na_spec(2, 1),
            pl.BlockSpec((tm, SWA_Q_WIDTH), tile),
            kv_prev, pl.BlockSpec((tm, 2 * SWA_KV_WIDTH), tile), kv_next,
            pl.BlockSpec((tm, 2 * D_MODEL), tile),
            cspec(tbl.shape), cspec(swa_mask.shape),
            cspec(wua.shape), cspec(wub.shape), cspec(wo.shape),
            cspec(g2.shape), cspec(wr_t.shape), cspec(br.shape),
        ],
        out_specs=[
            pl.BlockSpec((tm, D_MODEL), tile),
            pl.BlockSpec((tm, HALF), tile),
            pl.BlockSpec((8, tm), lambda b, j, *_: (0, b * nj + j)),
            pl.BlockSpec((8, tm), lambda b, j, *_: (0, b * nj + j)),
            pl.BlockSpec((N_EXPERTS, LANES), tile),
        ],
        scratch_shapes=[
            pltpu.VMEM((tm + 2 * halo, NA_WIDTH), _BF16),
            pltpu.VMEM((tm + 2 * halo, NA_WIDTH), _BF16),
            pltpu.VMEM((tm + 2 * SWA_BLOCK, 2 * SWA_KV_WIDTH), _BF16),
            pltpu.VMEM((tm, NA_WIDTH), _BF16),
            pltpu.VMEM((tm, SWA_Q_WIDTH), _BF16),
            pltpu.VMEM((score_rows, NA_ROWS * GRID_W), _F32),
            pltpu.VMEM((score_rows, NA_ROWS * GRID_W), _BF16),
            pltpu.VMEM((score_rows, LANES), _F32),
        ],
    )
    return pl.pallas_call(
        functools.partial(_mixer_kernel, rows=rows, n_swa_blocks=n_swa),
        grid_spec=grid_spec,
        out_shape=[
            jax.ShapeDtypeStruct((t, D_MODEL), _F32),
            jax.ShapeDtypeStruct((t, HALF), jnp.uint32),
            jax.ShapeDtypeStruct((8, t), jnp.int32),
            jax.ShapeDtypeStruct((8, t), _F32),
            jax.ShapeDtypeStruct((t // tm * N_EXPERTS, LANES), _F32),
        ],
        compiler_params=pltpu.CompilerParams(
            dimension_semantics=("parallel", "parallel"), vmem_limit_bytes=VMEM_LIMIT),
        name="mixer",
    )(sinks_perm, x2, qkva, qkva, qkva, qkva, qkva, qkva, qkva, qb, kvb, kvb, kvb, gates, tbl, swa_mask,
      wua, wub, wo, g2, wr_t, br)


def _expert_kernel(bexp_ref, nact_ref, gidx_hbm, sidx_hbm, h2_hbm, w1_ref, b1_ref, w2_ref, b2_ref,
                   y_hbm, gs0, gs1, ss0, ss1, xbuf0, xbuf1, ybuf0, ybuf1, w1b, w2b, isem, gsem, ssem,
                   *, n_blocks, n_real_rows):
    i = pl.program_id(0)
    nact = nact_ref[0]
    eb = EXPERT_BLOCK
    gs, ss, xbuf, ybuf = (gs0, gs1), (ss0, ss1), (xbuf0, xbuf1), (ybuf0, ybuf1)
    rows = pl.ds(0, eb)

    def gidx_copy(blk, slot):
        return pltpu.make_async_copy(gidx_hbm.at[blk], gs[slot], isem.at[0, slot])

    def sidx_copy(entry, slot):
        return pltpu.make_async_copy(sidx_hbm.at[entry], ss[slot], isem.at[1, slot])

    def gather_row(slot, r, priority=0):
        pltpu.make_async_copy(h2_hbm.at[pl.ds(gs[slot][r], 1), :], xbuf[slot].at[pl.ds(r, 1), :],
                              gsem.at[slot]).start(priority=priority)

    def wait_gather(slot):
        pltpu.make_async_copy(h2_hbm.at[rows, :], xbuf[slot].at[rows, :], gsem.at[slot]).wait()

    def scatter_row(slot, r, priority=0):
        pltpu.make_async_copy(ybuf[slot].at[pl.ds(r, 1), :], y_hbm.at[pl.ds(ss[slot][r], 1), :],
                              ssem.at[slot]).start(priority=priority)

    def wait_scatter(slot):
        pltpu.make_async_copy(ybuf[slot].at[rows, :], y_hbm.at[rows, :], ssem.at[slot]).wait()

    def rolled(fn, slot):
        def body(r, c):
            fn(slot, r)
            return c
        lax.fori_loop(0, eb, body, 0)

    @pl.when(i == 0)
    def _():
        gidx_copy(0, 0).start()
        gidx_copy(0, 0).wait()
        rolled(gather_row, 0)
        gidx_copy(jnp.minimum(1, n_blocks - 1), 1).start()
        sidx_copy(0, 1).start()
        ybuf1[...] = jnp.zeros(ybuf1.shape, jnp.uint32)
        init = pltpu.make_async_copy(ybuf1.at[rows, :], y_hbm.at[pl.ds(n_real_rows, eb), :], ssem.at[0])
        init.start()
        init.wait()

    def step(cur):
        nxt = 1 - cur
        gidx_copy(jnp.minimum(i + 2, n_blocks - 1), cur).start()
        sidx_copy(i + 1, cur).start()
        gidx_copy(0, nxt).wait()
        sidx_copy(0, nxt).wait()
        wait_gather(cur)

        x_lo, x_hi = (half.astype(_BF16) for half in _unpack_rows(xbuf[cur][rows, :]))

        def x_dot(c0):
            return (jnp.dot(x_lo, w1b[0:HALF, c0:c0 + FF_CHUNK], preferred_element_type=_F32)
                    + jnp.dot(x_hi, w1b[HALF:, c0:c0 + FF_CHUNK], preferred_element_type=_F32))

        n_chunks = D_FF // FF_CHUNK
        rows_per_group = 2 * eb // n_chunks
        spare = pl.ds(pl.multiple_of(eb + jnp.minimum(i, 0) * SUBLANES, SUBLANES), SUBLANES)
        y = None
        for jc in range(n_chunks):
            issue = gather_row if jc < n_chunks // 2 else scatter_row
            first = (jc % (n_chunks // 2)) * rows_per_group
            for r in range(first, first + rows_per_group):
                issue(nxt, r, r % 2)
            xbuf[nxt][spare, 0:FF_CHUNK] = jnp.zeros((SUBLANES, FF_CHUNK), jnp.uint32)
            ybuf[nxt][spare, 0:FF_CHUNK] = jnp.zeros((SUBLANES, FF_CHUNK), jnp.uint32)
            anchor = lax.bitcast_convert_type(
                (xbuf[nxt][spare, 0:FF_CHUNK] | ybuf[nxt][spare, 0:FF_CHUNK])[0:1], _F32)
            c0 = jc * FF_CHUNK
            g = x_dot(c0) + (b1_ref[:, c0:c0 + FF_CHUNK] + anchor)
            u = x_dot(D_FF + c0) + b1_ref[:, D_FF + c0:D_FF + c0 + FF_CHUNK]
            gate = jnp.minimum(g, SWIGLU_LIMIT)
            up = jnp.clip(u, -SWIGLU_LIMIT, SWIGLU_LIMIT)
            act = ((up + 1.0) * (gate * jax.nn.sigmoid(gate * SWIGLU_ALPHA))).astype(_BF16)
            part = jnp.dot(act, w2b[c0:c0 + FF_CHUNK, :], preferred_element_type=_F32)
            y = part + b2_ref[...] if y is None else y + part

        @pl.when(i >= 1)
        def _():
            wait_scatter(cur)

        ybuf[cur][rows, :] = _pack_rows(y[:, :HALF].astype(_BF16), y[:, HALF:].astype(_BF16))

        @pl.when(i == nact - 1)
        def _():
            sidx_copy(0, cur).wait()
            rolled(scatter_row, cur)
            wait_scatter(nxt)
            wait_scatter(cur)
            wait_gather(nxt)
            gidx_copy(0, cur).wait()

    new_expert = jnp.logical_or(i == 0, bexp_ref[i] != bexp_ref[jnp.maximum(i - 1, 0)])

    @pl.when(jnp.logical_and(i < nact, new_expert))
    def _():
        w1b[...] = w1_ref[...].astype(_BF16)
        w2b[...] = w2_ref[...].astype(_BF16)

    for parity in range(2):
        @pl.when(jnp.logical_and(i < nact, i % 2 == parity))
        def _():
            step(parity)


def _experts(blk_exp, n_active, gidx, sidx, h2, w1, b1, w2, b2, n_tokens):
    n_blk = gidx.shape[0]
    eb = EXPERT_BLOCK
    n_real = TOP_K * n_tokens
    wmap = lambda i, be, na: (be[i], 0, 0)
    grid_spec = pltpu.PrefetchScalarGridSpec(
        num_scalar_prefetch=2,
        grid=(n_blk,),
        in_specs=[
            pl.BlockSpec(memory_space=pl.ANY),
            pl.BlockSpec(memory_space=pl.ANY),
            pl.BlockSpec(memory_space=pl.ANY),
            pl.BlockSpec((None, D_MODEL, 2 * D_FF), wmap),
            pl.BlockSpec((None, 1, 2 * D_FF), wmap),
            pl.BlockSpec((None, D_FF, D_MODEL), wmap),
            pl.BlockSpec((None, 1, D_MODEL), wmap),
        ],
        out_specs=pl.BlockSpec(memory_space=pl.ANY),
        scratch_shapes=[
            pltpu.SMEM((eb,), jnp.int32), pltpu.SMEM((eb,), jnp.int32),
            pltpu.SMEM((eb,), jnp.int32), pltpu.SMEM((eb,), jnp.int32),
            pltpu.VMEM((eb + SUBLANES, HALF), jnp.uint32),
            pltpu.VMEM((eb + SUBLANES, HALF), jnp.uint32),
            pltpu.VMEM((eb + SUBLANES, HALF), jnp.uint32),
            pltpu.VMEM((eb + SUBLANES, HALF), jnp.uint32),
            pltpu.VMEM((D_MODEL, 2 * D_FF), _BF16),
            pltpu.VMEM((D_FF, D_MODEL), _BF16),
            pltpu.SemaphoreType.DMA((2, 2)),
            pltpu.SemaphoreType.DMA((2,)),
            pltpu.SemaphoreType.DMA((2,)),
        ],
    )
    return pl.pallas_call(
        functools.partial(_expert_kernel, n_blocks=n_blk, n_real_rows=n_real),
        grid_spec=grid_spec,
        out_shape=jax.ShapeDtypeStruct((n_real + 2 * eb, HALF), jnp.uint32),
        compiler_params=pltpu.CompilerParams(
            dimension_semantics=("arbitrary",), vmem_limit_bytes=VMEM_LIMIT),
        name="experts",
    )(blk_exp, n_active, gidx, sidx, h2, w1, b1, w2, b2)


def _combine_kernel(x1_ref, y0_ref, y1_ref, y2_ref, y3_ref, wts_ref, g_ref, o_ref):
    wt = wts_ref[...].T
    y = x1_ref[...]
    for k, y_ref in enumerate((y0_ref, y1_ref, y2_ref, y3_ref)):
        y = y + wt[:, k:k + 1] * jnp.concatenate(_unpack_rows(y_ref[...]), axis=1)
    var = jnp.mean(y * y, axis=-1, keepdims=True)
    o_ref[...] = y * lax.rsqrt(var + RMS_EPS) * g_ref[...]


def _combine(x1, y, wts, gf):
    t = x1.shape[0]
    tm = TM_OUT
    nt = t // tm
    yspec = lambda k: pl.BlockSpec((tm, HALF), lambda i: (k * nt + i, 0))
    return pl.pallas_call(
        _combine_kernel,
        grid=(nt,),
        in_specs=[
            pl.BlockSpec((tm, D_MODEL), lambda i: (i, 0)),
            yspec(0), yspec(1), yspec(2), yspec(3),
            pl.BlockSpec((8, tm), lambda i: (0, i)),
            _const_spec((1, D_MODEL)),
        ],
        out_specs=pl.BlockSpec((tm, D_MODEL), lambda i: (i, 0)),
        out_shape=jax.ShapeDtypeStruct((t, D_MODEL), _F32),
        compiler_params=pltpu.CompilerParams(
            dimension_semantics=("parallel",), vmem_limit_bytes=VMEM_LIMIT),
        name="combine",
    )(x1, y, y, y, y, wts, gf)


def _split_bf16(w):
    hi = w.astype(_BF16)
    lo = (w - hi.astype(_F32)).astype(_BF16)
    return jnp.concatenate([hi, lo], axis=0)


def _rope_tables(seq):
    half = SWA_HEAD_DIM // 2
    inv_freq = np.float32(ROPE_THETA) ** (-np.arange(half, dtype=np.float32) / np.float32(half))
    ang = np.arange(seq, dtype=np.float32)[:, None] * inv_freq[None, :]
    cos, sin = np.cos(ang), np.sin(ang)
    cos_t = np.tile(np.concatenate([cos, cos], axis=1), (1, LANES // SWA_HEAD_DIM))
    sin_t = np.tile(np.concatenate([-sin, sin], axis=1), (1, LANES // SWA_HEAD_DIM))
    return jnp.asarray(cos_t, _F32), jnp.asarray(sin_t, _F32)


def _na_bias_table(rpb):
    col = np.arange(GRID_W)
    cstart = np.clip(col - NA_COLS // 2, 0, GRID_W - NA_COLS)
    kc = np.arange(GRID_W)
    valid = (kc[None, :] >= cstart[:, None]) & (kc[None, :] < cstart[:, None] + NA_COLS)
    off = kc[None, :] - col[:, None] + NA_COLS - 1
    pick = (off[:, :, None] == np.arange(2 * NA_COLS - 1)[None, None, :]) & valid[:, :, None]
    ext = jnp.einsum("hvo,cko->hvck", rpb.astype(_F32), jnp.asarray(pick, _F32),
                     precision=lax.Precision.HIGHEST)
    ext = jnp.where(valid[None, None], ext, NEG_BIG)
    tbl = jnp.stack([ext[:, NA_ROWS - 1 - d_:2 * NA_ROWS - 1 - d_] for d_ in range(NA_ROWS)], axis=1)
    tbl = tbl.transpose(1, 0, 3, 2, 4).reshape(NA_ROWS, NA_HEADS // 2, 2 * GRID_W, NA_ROWS * GRID_W)
    return (tbl * LOG2E).astype(_F32)


def kernel(x, norm1_g, w_in, b_in, na_rpb, swa_sinks, w_up_a, w_up_b, w_out, norm2_g, w_router,
           b_router, w1, b1, w2, b2, final_g):
    batch, seq, d = x.shape
    depth = w_in.shape[0]
    t = batch * seq
    assert depth == 1, "the final norm is fused into the single layer's combine step"
    assert d == D_MODEL and seq % TM_MIX == 0 and seq // GRID_W >= 2 * NA_ROWS and t % TM_OUT == 0

    group = SWA_Q_HEADS // SWA_KV_HEADS
    head_order = np.arange(SWA_Q_HEADS).reshape(SWA_KV_HEADS, group).T.reshape(-1)

    def reorder_heads(a, axis, start):
        take = lambda lo, hi: lax.slice_in_dim(a, lo, hi, axis=axis)
        heads = [take(start + h * SWA_HEAD_DIM, start + (h + 1) * SWA_HEAD_DIM) for h in head_order]
        return jnp.concatenate([take(0, start)] + heads + [take(start + SWA_Q_WIDTH, a.shape[axis])], axis=axis)

    cos_t, sin_t = _rope_tables(seq)

    n_assign = t * TOP_K
    assert n_assign <= PAD_FLAG
    n_rows = n_assign + N_EXPERTS * EXPERT_BLOCK
    n_blk = n_rows // EXPERT_BLOCK
    n_tiles = t // TM_MIX

    x2 = x.reshape(t, d)
    for l in range(depth):
        w_in_l = reorder_heads(w_in[l], 1, C_QB).astype(_BF16)
        b_in_l = reorder_heads(b_in[l], 0, C_QB).reshape(1, D_IN)
        qkva, qb, kvb, gates = _inproj(x2, norm1_g[l].reshape(1, d), w_in_l, b_in_l, cos_t, sin_t, seq)

        x1, h2, keys, wts, cnt = _mixer(
            swa_sinks[l][head_order].astype(_F32) * LOG2E, x2, qkva, qb, kvb, gates, _na_bias_table(na_rpb[l]),
            w_up_a[l].astype(_BF16), reorder_heads(w_up_b[l], 0, 0).astype(_BF16), w_out[l].astype(_BF16),
            norm2_g[l].reshape(1, d), _split_bf16(w_router[l].T), b_router[l].reshape(N_EXPERTS, 1),
            batch, seq)

        cnt = cnt.reshape(n_tiles, N_EXPERTS, LANES)[:, :, 0].astype(jnp.int32)
        counts = jnp.sum(cnt, axis=0)
        padded = (counts + EXPERT_BLOCK - 1) // EXPERT_BLOCK * EXPERT_BLOCK
        pend = jnp.cumsum(padded)
        pad_i = jnp.arange(EXPERT_BLOCK, dtype=jnp.int32)[None, :]
        pad_keys = jnp.where(pad_i < (padded - counts)[:, None],
                             (jnp.arange(N_EXPERTS, dtype=jnp.int32)[:, None] << KEY_SHIFT) | PAD_FLAG | pad_i,
                             jnp.iinfo(jnp.int32).max)
        sorted_keys = jnp.sort(jnp.concatenate([keys[:TOP_K].reshape(-1), pad_keys.reshape(-1)]))
        row_a = jnp.where((sorted_keys & PAD_FLAG) == 0, sorted_keys & (PAD_FLAG - 1), -1)
        row_a = row_a.reshape(n_blk, EXPERT_BLOCK)
        blk_start = jnp.arange(n_blk, dtype=jnp.int32) * EXPERT_BLOCK
        blk_exp = jnp.minimum(jnp.sum((pend[None, :] <= blk_start[:, None]).astype(jnp.int32), axis=1),
                              N_EXPERTS - 1)
        n_active = (pend[-1:] // EXPERT_BLOCK).astype(jnp.int32)
        r_in_blk = jnp.arange(EXPERT_BLOCK, dtype=jnp.int32)[None, :]
        parity = (jnp.arange(-1, n_blk, dtype=jnp.int32) % 2)[:, None]
        trash = n_assign + parity * EXPERT_BLOCK + r_in_blk
        gidx = jnp.where(row_a >= 0, row_a >> 2, 0)
        sidx = jnp.where(row_a >= 0, (row_a & (TOP_K - 1)) * t + (row_a >> 2), trash[1:])
        sidx = jnp.concatenate([trash[:1], sidx], axis=0)

        y = _experts(blk_exp, n_active, gidx, sidx, h2,
                     w1[l], b1[l].reshape(N_EXPERTS, 1, 2 * D_FF),
                     w2[l], b2[l].reshape(N_EXPERTS, 1, D_MODEL), t)
        x2 = _combine(x1, y, wts, final_g.reshape(1, d))
    return x2.reshape(batch, seq, d)
```

```python
import functools

import jax
import jax.numpy as jnp
import numpy as np
from jax import lax
from jax.experimental import pallas as pl
from jax.experimental.pallas import tpu as pltpu

D_MODEL = 1024
GRID_W = 64
NA_HEADS = 8
NA_HEAD_DIM = 64
NA_ROWS = 8
NA_COLS = 16
SWA_Q_HEADS = 8
SWA_KV_HEADS = 2
SWA_HEAD_DIM = 64
SWA_WINDOW = 128
SWA_BLOCK = 128
ROPE_THETA = 10000.0
N_EXPERTS = 32
TOP_K = 4
D_FF = 1024
SWIGLU_LIMIT = 7.0
SWIGLU_ALPHA = 1.702
EXPERT_BLOCK = 512
RMS_EPS = 1e-5

NA_WIDTH = NA_HEADS * NA_HEAD_DIM
SWA_Q_WIDTH = SWA_Q_HEADS * SWA_HEAD_DIM
SWA_KV_WIDTH = SWA_KV_HEADS * SWA_HEAD_DIM
LANES = 128
SUBLANES = 8
NEG_BIG = -1e30

C_QA, C_KA, C_VA = 0, NA_WIDTH, 2 * NA_WIDTH
C_QB = 3 * NA_WIDTH
C_KB = C_QB + SWA_Q_WIDTH
C_VB = C_KB + SWA_KV_WIDTH
C_GA = C_VB + SWA_KV_WIDTH
C_GB = C_GA + D_MODEL
D_IN = C_GB + D_MODEL

TM_PROJ = 1024
TM_MIX = 512
TM_OUT = 1024
GATE_CHUNK = 512
FF_CHUNK = 256
NA_BATCH_ROWS = 4
LOG2E = 1.4426950408889634
KEY_SHIFT = 18
PAD_FLAG = 1 << (KEY_SHIFT - 1)
VMEM_LIMIT = 56 * 1024 * 1024

_BF16 = jnp.bfloat16
_F32 = jnp.float32


def _const_spec(shape):
    nd = len(shape)
    return pl.BlockSpec(shape, lambda *_: (0,) * nd, pipeline_mode=pl.Buffered(1))


HALF = D_MODEL // 2
_HI_MASK = 0xFFFF0000


def _pack_rows(lo_bf16, hi_bf16):
    lo = lax.bitcast_convert_type(lo_bf16.astype(_F32), jnp.uint32) >> 16
    hi = lax.bitcast_convert_type(hi_bf16.astype(_F32), jnp.uint32) & jnp.uint32(_HI_MASK)
    return lo | hi


def _unpack_rows(u):
    lo = lax.bitcast_convert_type(u << 16, _F32)
    hi = lax.bitcast_convert_type(u & jnp.uint32(_HI_MASK), _F32)
    return lo, hi


def _rope_slab(y, cos, sin_signed):
    lane = lax.broadcasted_iota(jnp.int32, y.shape, 1)
    first_half = (lane & (SWA_HEAD_DIM - 1)) < (SWA_HEAD_DIM // 2)
    rot = jnp.where(first_half, pltpu.roll(y, LANES - SWA_HEAD_DIM // 2, axis=1),
                    pltpu.roll(y, SWA_HEAD_DIM // 2, axis=1))
    return y * cos + rot * sin_signed


def _inproj_kernel(x_ref, g_ref, w_ref, b_ref, cos_ref, sin_ref,
                   qkva_ref, qb_ref, kvb_ref, gate_ref):
    x = x_ref[...]
    var = jnp.mean(x * x, axis=-1, keepdims=True)
    h = (x * lax.rsqrt(var + RMS_EPS) * g_ref[...]).astype(_BF16)

    def proj(c0, c1):
        return jnp.dot(h, w_ref[:, c0:c1], preferred_element_type=_F32) + b_ref[:, c0:c1]

    scale = NA_HEAD_DIM ** -0.5 * LOG2E
    qkva_ref[:, C_QA:C_KA] = (proj(C_QA, C_KA) * scale).astype(_BF16)
    qkva_ref[:, C_KA:C_VA] = proj(C_KA, C_VA).astype(_BF16)
    qkva_ref[:, C_VA:C_QB] = proj(C_VA, C_QB).astype(_BF16)

    cos = cos_ref[...]
    sin = sin_ref[...]
    qb = proj(C_QB, C_KB)
    for s in range(SWA_Q_WIDTH // LANES):
        slab = _rope_slab(qb[:, s * LANES:(s + 1) * LANES], cos, sin)
        qb_ref[:, s * LANES:(s + 1) * LANES] = (slab * (SWA_HEAD_DIM ** -0.5 * LOG2E)).astype(_BF16)
    kvb = proj(C_KB, C_GA)
    kvb_ref[:, 0:LANES] = _rope_slab(kvb[:, 0:LANES], cos, sin).astype(_BF16)
    kvb_ref[:, LANES:2 * LANES] = kvb[:, LANES:2 * LANES].astype(_BF16)

    for c0 in range(C_GA, D_IN, GATE_CHUNK):
        gate_ref[:, c0 - C_GA:c0 - C_GA + GATE_CHUNK] = jax.nn.sigmoid(proj(c0, c0 + GATE_CHUNK)).astype(_BF16)


def _inproj(x2, g1, w_in, b_in, cos_t, sin_t, seq):
    t = x2.shape[0]
    tm = TM_PROJ
    nseq = seq // tm
    row = lambda i: (i, 0)
    return pl.pallas_call(
        _inproj_kernel,
        grid=(t // tm,),
        in_specs=[
            pl.BlockSpec((tm, D_MODEL), row),
            _const_spec((1, D_MODEL)),
            _const_spec((D_MODEL, D_IN)),
            _const_spec((1, D_IN)),
            pl.BlockSpec((tm, LANES), lambda i: (i % nseq, 0)),
            pl.BlockSpec((tm, LANES), lambda i: (i % nseq, 0)),
        ],
        out_specs=[
            pl.BlockSpec((tm, 3 * NA_WIDTH), row),
            pl.BlockSpec((tm, SWA_Q_WIDTH), row),
            pl.BlockSpec((tm, 2 * SWA_KV_WIDTH), row),
            pl.BlockSpec((tm, 2 * D_MODEL), row),
        ],
        out_shape=[
            jax.ShapeDtypeStruct((t, 3 * NA_WIDTH), _BF16),
            jax.ShapeDtypeStruct((t, SWA_Q_WIDTH), _BF16),
            jax.ShapeDtypeStruct((t, 2 * SWA_KV_WIDTH), _BF16),
            jax.ShapeDtypeStruct((t, 2 * D_MODEL), _BF16),
        ],
        compiler_params=pltpu.CompilerParams(
            dimension_semantics=("parallel",), vmem_limit_bytes=VMEM_LIMIT),
        name="inproj",
    )(x2, g1, w_in, b_in, cos_t, sin_t)


_NT_DIMS = (((1,), (1,)), ((), ()))


def _mixer_kernel(sink_ref, x_ref, q_ref, kp_ref, kc_ref, kn_ref, vp_ref, vc_ref, vn_ref,
                  qb_ref, kvp_ref, kvc_ref, kvn_ref, gate_ref, tbl_ref, mask_ref,
                  wua_ref, wub_ref, wo_ref, g2_ref, wr_ref, br_ref,
                  x1_ref, h2_ref, key_ref, wts_ref, cnt_ref,
                  kcat, vcat, kvcat, oa_s, ob_s, s_s, e_s, inv_s, *, rows, n_swa_blocks):
    j = pl.program_id(1)
    rows_per_tile = TM_MIX // GRID_W
    halo = (NA_ROWS // 2) * GRID_W
    band = NA_ROWS * GRID_W

    kcat[0:halo] = kp_ref[...]
    kcat[halo:halo + TM_MIX] = kc_ref[...]
    kcat[halo + TM_MIX:] = kn_ref[...]
    vcat[0:halo] = vp_ref[...]
    vcat[halo:halo + TM_MIX] = vc_ref[...]
    vcat[halo + TM_MIX:] = vn_ref[...]

    lane_q = lax.broadcasted_iota(jnp.int32, (GRID_W, LANES), 1)
    low_q = lane_q < NA_HEAD_DIM
    lane_o = lax.broadcasted_iota(jnp.int32, (GRID_W, LANES), 1) < NA_HEAD_DIM

    n_pairs = NA_HEADS // 2
    chain = 2 * GRID_W

    def band_start(i):
        r = j * rows_per_tile + i
        rs = jnp.clip(r - NA_ROWS // 2, 0, rows - NA_ROWS)
        start = pl.multiple_of((rs - (j * rows_per_tile - NA_ROWS // 2)) * GRID_W, GRID_W)
        return start, r - rs

    for half in range(rows_per_tile // NA_BATCH_ROWS):
        def na_scores(b, carry):
            i = half * NA_BATCH_ROWS + b
            start, d = band_start(i)
            q0 = pl.multiple_of(i * GRID_W, GRID_W)
            for p in range(n_pairs):
                cols = slice(p * LANES, (p + 1) * LANES)
                qpair = q_ref[pl.ds(q0, GRID_W), cols]
                zero = jnp.zeros_like(qpair)
                qs = jnp.concatenate([jnp.where(low_q, qpair, zero), jnp.where(low_q, zero, qpair)], axis=0)
                kb = kcat[pl.ds(start, band), cols]
                s = lax.dot_general(qs, kb, _NT_DIMS, preferred_element_type=_F32) + tbl_ref[d, p]
                s_s[pl.ds(pl.multiple_of((b * n_pairs + p) * chain, chain), chain), :] = s
            return carry

        def na_softmax(b, carry):
            rws = pl.ds(pl.multiple_of(b * n_pairs * chain, n_pairs * chain), n_pairs * chain)
            s = s_s[rws, :]
            e = jnp.exp2(s - jnp.max(s, axis=-1, keepdims=True))
            e_s[rws, :] = e.astype(_BF16)
            inv_s[rws, :] = jnp.broadcast_to(1.0 / jnp.sum(e, axis=-1, keepdims=True), (n_pairs * chain, LANES))
            return carry

        def na_values(b, carry):
            i = half * NA_BATCH_ROWS + b
            start, _ = band_start(i)
            q0 = pl.multiple_of(i * GRID_W, GRID_W)
            for p in range(n_pairs):
                cols = slice(p * LANES, (p + 1) * LANES)
                rws = pl.ds(pl.multiple_of((b * n_pairs + p) * chain, chain), chain)
                vb = vcat[pl.ds(start, band), cols]
                o = jnp.dot(e_s[rws, :], vb, preferred_element_type=_F32) * inv_s[rws, :]
                oa_s[pl.ds(q0, GRID_W), cols] = jnp.where(lane_o, o[:GRID_W], o[GRID_W:]).astype(_BF16)
            return carry

        lax.fori_loop(0, NA_BATCH_ROWS, na_scores, 0)
        lax.fori_loop(0, NA_BATCH_ROWS, na_softmax, 0)
        lax.fori_loop(0, NA_BATCH_ROWS, na_values, 0)

    kvcat[0:SWA_BLOCK] = kvp_ref[...]
    kvcat[SWA_BLOCK:SWA_BLOCK + TM_MIX] = kvc_ref[...]
    kvcat[SWA_BLOCK + TM_MIX:] = kvn_ref[...]

    n_slabs = SWA_Q_WIDTH // LANES
    stack = n_slabs * SWA_BLOCK
    lane_s = lax.broadcasted_iota(jnp.int32, (SWA_BLOCK, LANES), 1) < SWA_HEAD_DIM
    rowblk = lax.broadcasted_iota(jnp.int32, (stack, 1), 0) // SWA_BLOCK
    wband = 3 * SWA_BLOCK
    sinks = []
    for g in range(SWA_KV_HEADS):
        sink = jnp.zeros((stack, 1), _F32)
        for s_ in range(n_slabs):
            sink = jnp.where(rowblk == s_, sink_ref[SWA_KV_HEADS * s_ + g], sink)
        sinks.append(sink)

    def swa_block(n, carry):
        nb = j * (TM_MIX // SWA_BLOCK) + n
        t0 = pl.multiple_of(n * SWA_BLOCK, SWA_BLOCK)
        variant = jnp.where(nb == 0, 0, jnp.where(nb == n_swa_blocks - 1, 2, 1))
        mask = mask_ref[variant]
        mask = jnp.concatenate([mask] * n_slabs, axis=0)
        kband = kvcat[pl.ds(t0, wband), 0:LANES]
        vband = kvcat[pl.ds(t0, wband), LANES:2 * LANES]
        outs = []
        for g in range(SWA_KV_HEADS):
            parts = []
            for s_ in range(n_slabs):
                slab = qb_ref[pl.ds(t0, SWA_BLOCK), s_ * LANES:(s_ + 1) * LANES]
                zero = jnp.zeros_like(slab)
                parts.append(jnp.where(lane_s, slab, zero) if g == 0 else jnp.where(lane_s, zero, slab))
            qg = jnp.concatenate(parts, axis=0)
            s = lax.dot_general(qg, kband, _NT_DIMS, preferred_element_type=_F32) + mask
            m = jnp.maximum(jnp.max(s, axis=-1, keepdims=True), sinks[g])
            e = jnp.exp2(s - m)
            den = jnp.sum(e, axis=-1, keepdims=True) + jnp.exp2(sinks[g] - m)
            outs.append(jnp.dot(e.astype(_BF16), vband, preferred_element_type=_F32) * (1.0 / den))
        for s_ in range(n_slabs):
            rs_ = slice(s_ * SWA_BLOCK, (s_ + 1) * SWA_BLOCK)
            ob_s[pl.ds(t0, SWA_BLOCK), s_ * LANES:(s_ + 1) * LANES] = jnp.where(
                lane_s, outs[0][rs_], outs[1][rs_]).astype(_BF16)
        return carry

    lax.fori_loop(0, TM_MIX // SWA_BLOCK, swa_block, 0)

    ua = jnp.dot(oa_s[...], wua_ref[...], preferred_element_type=_F32)
    ub = jnp.dot(ob_s[...], wub_ref[...], preferred_element_type=_F32)
    merged = (gate_ref[:, 0:D_MODEL].astype(_F32) * ua
              + gate_ref[:, D_MODEL:].astype(_F32) * ub).astype(_BF16)
    x1 = x_ref[...] + jnp.dot(merged, wo_ref[...], preferred_element_type=_F32)
    x1_ref[...] = x1

    var = jnp.mean(x1 * x1, axis=-1, keepdims=True)
    h2 = x1 * lax.rsqrt(var + RMS_EPS) * g2_ref[...]
    h2_hi = h2.astype(_BF16)
    h2_ref[...] = _pack_rows(h2_hi[:, :HALF], h2_hi[:, HALF:])
    h2_lo = (h2 - h2_hi.astype(_F32)).astype(_BF16)
    both = lax.dot_general(wr_ref[...], h2_hi, _NT_DIMS, preferred_element_type=_F32)
    cross = lax.dot_general(wr_ref[0:N_EXPERTS], h2_lo, _NT_DIMS, preferred_element_type=_F32)
    logits = both[0:N_EXPERTS] + both[N_EXPERTS:] + cross + br_ref[...]
    eidx = lax.broadcasted_iota(jnp.int32, logits.shape, 0)
    vals, idxs, hots = [], [], []
    for _ in range(TOP_K):
        m = jnp.max(logits, axis=0, keepdims=True)
        idx = jnp.min(jnp.where(logits == m, eidx, N_EXPERTS), axis=0, keepdims=True)
        hot = eidx == idx
        logits = jnp.where(hot, -jnp.inf, logits)
        vals.append(m)
        idxs.append(idx)
        hots.append(hot)
    es = [jnp.exp(v - vals[0]) for v in vals]
    inv = 1.0 / (es[0] + es[1] + es[2] + es[3])
    sel = jnp.zeros(logits.shape, _F32)
    for hot in hots:
        sel = sel + jnp.where(hot, 1.0, 0.0)
    tile_idx = pl.program_id(0) * pl.num_programs(1) + j
    tok = tile_idx * TM_MIX + lax.broadcasted_iota(jnp.int32, (1, TM_MIX), 1)
    keys = [(idx << KEY_SHIFT) | (tok * TOP_K + k) for k, idx in enumerate(idxs)]
    key_ref[...] = jnp.concatenate(keys + [jnp.zeros((8 - TOP_K, TM_MIX), jnp.int32)], axis=0)
    wts_ref[...] = jnp.concatenate([e * inv for e in es] + [jnp.zeros((8 - TOP_K, TM_MIX), _F32)], axis=0)
    cnt_ref[...] = jnp.broadcast_to(jnp.sum(sel, axis=1, keepdims=True), (N_EXPERTS, LANES))


def _swa_mask_table():
    qpos = np.arange(SWA_BLOCK)[:, None]
    koff = np.arange(3 * SWA_BLOCK)[None, :] - SWA_BLOCK
    rel_ok = np.abs(koff - qpos) <= SWA_WINDOW
    has_prev = np.array([False, True, True])[:, None, None]
    has_next = np.array([True, True, False])[:, None, None]
    ok = rel_ok[None] & ((koff >= 0)[None] | has_prev) & ((koff < SWA_BLOCK)[None] | has_next)
    return jnp.asarray(np.where(ok, 0.0, NEG_BIG), _F32)


def _mixer(sinks_perm, x2, qkva, qb, kvb, gates, tbl, wua, wub, wo, g2, wr_t, br, batch, seq):
    t = x2.shape[0]
    tm = TM_MIX
    nj = seq // tm
    rows = seq // GRID_W
    hb = tm // ((NA_ROWS // 2) * GRID_W)
    sb = tm // SWA_BLOCK
    n_halo = seq // ((NA_ROWS // 2) * GRID_W)
    n_swa = seq // SWA_BLOCK
    halo = (NA_ROWS // 2) * GRID_W
    swa_mask = _swa_mask_table()
    score_rows = NA_BATCH_ROWS * NA_HEADS * GRID_W
    assert n_swa >= 2

    tile = lambda b, j, *_: (b * nj + j, 0)

    def na_spec(col, which):
        if which == 0:
            return pl.BlockSpec((tm, NA_WIDTH), lambda b, j, *_: (b * nj + j, col))
        if which < 0:
            return pl.BlockSpec((halo, NA_WIDTH),
                                lambda b, j, *_: (b * n_halo + jnp.maximum(j * hb - 1, 0), col))
        return pl.BlockSpec((halo, NA_WIDTH),
                            lambda b, j, *_: (b * n_halo + jnp.minimum(j * hb + hb, n_halo - 1), col))

    kv_prev = pl.BlockSpec((SWA_BLOCK, 2 * SWA_KV_WIDTH),
                           lambda b, j, *_: (b * n_swa + jnp.maximum(j * sb - 1, 0), 0))
    kv_next = pl.BlockSpec((SWA_BLOCK, 2 * SWA_KV_WIDTH),
                           lambda b, j, *_: (b * n_swa + jnp.minimum(j * sb + sb, n_swa - 1), 0))

    cspec = _const_spec

    grid_spec = pltpu.PrefetchScalarGridSpec(
        num_scalar_prefetch=1,
        grid=(batch, nj),
        in_specs=[
            pl.BlockSpec((tm, D_MODEL), tile),
            na_spec(0, 0),
            na_spec(1, -1), na_spec(1, 0), na_spec(1, 1),
            na_spec(2, -1), na_spec(2, 0), na_spec(2, 1),
            pl.BlockSpec((tm, SWA_Q_WIDTH), tile),
            kv_prev, pl.BlockSpec((tm, 2 * SWA_KV_WIDTH), tile), kv_next,
            pl.BlockSpec((tm, 2 * D_MODEL), tile),
            cspec(tbl.shape), cspec(swa_mask.shape),
            cspec(wua.shape), cspec(wub.shape), cspec(wo.shape),
            cspec(g2.shape), cspec(wr_t.shape), cspec(br.shape),
        ],
        out_specs=[
            pl.BlockSpec((tm, D_MODEL), tile),
            pl.BlockSpec((tm, HALF), tile),
            pl.BlockSpec((8, tm), lambda b, j, *_: (0, b * nj + j)),
            pl.BlockSpec((8, tm), lambda b, j, *_: (0, b * nj + j)),
            pl.BlockSpec((N_EXPERTS, LANES), tile),
        ],
        scratch_shapes=[
            pltpu.VMEM((tm + 2 * halo, NA_WIDTH), _BF16),
            pltpu.VMEM((tm + 2 * halo, NA_WIDTH), _BF16),
            pltpu.VMEM((tm + 2 * SWA_BLOCK, 2 * SWA_KV_WIDTH), _BF16),
            pltpu.VMEM((tm, NA_WIDTH), _BF16),
            pltpu.VMEM((tm, SWA_Q_WIDTH), _BF16),
            pltpu.VMEM((score_rows, NA_ROWS * GRID_W), _F32),
            pltpu.VMEM((score_rows, NA_ROWS * GRID_W), _BF16),
            pltpu.VMEM((score_rows, LANES), _F32),
        ],
    )
    return pl.pallas_call(
        functools.partial(_mixer_kernel, rows=rows, n_swa_blocks=n_swa),
        grid_spec=grid_spec,
        out_shape=[
            jax.ShapeDtypeStruct((t, D_MODEL), _F32),
            jax.ShapeDtypeStruct((t, HALF), jnp.uint32),
            jax.ShapeDtypeStruct((8, t), jnp.int32),
            jax.ShapeDtypeStruct((8, t), _F32),
            jax.ShapeDtypeStruct((t // tm * N_EXPERTS, LANES), _F32),
        ],
        compiler_params=pltpu.CompilerParams(
            dimension_semantics=("parallel", "parallel"), vmem_limit_bytes=VMEM_LIMIT),
        name="mixer",
    )(sinks_perm, x2, qkva, qkva, qkva, qkva, qkva, qkva, qkva, qb, kvb, kvb, kvb, gates, tbl, swa_mask,
      wua, wub, wo, g2, wr_t, br)


def _expert_kernel(bexp_ref, nact_ref, gidx_hbm, sidx_hbm, h2_hbm, w1_ref, b1_ref, w2_ref, b2_ref,
                   y_hbm, gs0, gs1, ss0, ss1, xbuf0, xbuf1, ybuf0, ybuf1, w1b, w2b, isem, gsem, ssem,
                   *, n_blocks, n_real_rows):
    i = pl.program_id(0)
    nact = nact_ref[0]
    eb = EXPERT_BLOCK
    gs, ss, xbuf, ybuf = (gs0, gs1), (ss0, ss1), (xbuf0, xbuf1), (ybuf0, ybuf1)
    rows = pl.ds(0, eb)

    def gidx_copy(blk, slot):
        return pltpu.make_async_copy(gidx_hbm.at[blk], gs[slot], isem.at[0, slot])

    def sidx_copy(entry, slot):
        return pltpu.make_async_copy(sidx_hbm.at[entry], ss[slot], isem.at[1, slot])

    def gather_row(slot, r, priority=0):
        pltpu.make_async_copy(h2_hbm.at[pl.ds(gs[slot][r], 1), :], xbuf[slot].at[pl.ds(r, 1), :],
                              gsem.at[slot]).start(priority=priority)

    def wait_gather(slot):
        pltpu.make_async_copy(h2_hbm.at[rows, :], xbuf[slot].at[rows, :], gsem.at[slot]).wait()

    def scatter_row(slot, r, priority=0):
        pltpu.make_async_copy(ybuf[slot].at[pl.ds(r, 1), :], y_hbm.at[pl.ds(ss[slot][r], 1), :],
                              ssem.at[slot]).start(priority=priority)

    def wait_scatter(slot):
        pltpu.make_async_copy(ybuf[slot].at[rows, :], y_hbm.at[rows, :], ssem.at[slot]).wait()

    def rolled(fn, slot):
        def body(r, c):
            fn(slot, r)
            return c
        lax.fori_loop(0, eb, body, 0)

    @pl.when(i == 0)
    def _():
        gidx_copy(0, 0).start()
        gidx_copy(0, 0).wait()
        rolled(gather_row, 0)
        gidx_copy(jnp.minimum(1, n_blocks - 1), 1).start()
        sidx_copy(0, 1).start()
        ybuf1[...] = jnp.zeros(ybuf1.shape, jnp.uint32)
        init = pltpu.make_async_copy(ybuf1.at[rows, :], y_hbm.at[pl.ds(n_real_rows, eb), :], ssem.at[0])
        init.start()
        init.wait()

    def step(cur):
        nxt = 1 - cur
        gidx_copy(jnp.minimum(i + 2, n_blocks - 1), cur).start()
        sidx_copy(i + 1, cur).start()
        gidx_copy(0, nxt).wait()
        sidx_copy(0, nxt).wait()
        wait_gather(cur)

        x_lo, x_hi = (half.astype(_BF16) for half in _unpack_rows(xbuf[cur][rows, :]))

        def x_dot(c0):
            return (jnp.dot(x_lo, w1b[0:HALF, c0:c0 + FF_CHUNK], preferred_element_type=_F32)
                    + jnp.dot(x_hi, w1b[HALF:, c0:c0 + FF_CHUNK], preferred_element_type=_F32))

        n_chunks = D_FF // FF_CHUNK
        rows_per_group = 2 * eb // n_chunks
        spare = pl.ds(pl.multiple_of(eb + jnp.minimum(i, 0) * SUBLANES, SUBLANES), SUBLANES)
        y = None
        for jc in range(n_chunks):
            issue = gather_row if jc < n_chunks // 2 else scatter_row
            first = (jc % (n_chunks // 2)) * rows_per_group
            for r in range(first, first + rows_per_group):
                issue(nxt, r, r % 2)
            xbuf[nxt][spare, 0:FF_CHUNK] = jnp.zeros((SUBLANES, FF_CHUNK), jnp.uint32)
            ybuf[nxt][spare, 0:FF_CHUNK] = jnp.zeros((SUBLANES, FF_CHUNK), jnp.uint32)
            anchor = lax.bitcast_convert_type(
                (xbuf[nxt][spare, 0:FF_CHUNK] | ybuf[nxt][spare, 0:FF_CHUNK])[0:1], _F32)
            c0 = jc * FF_CHUNK
            g = x_dot(c0) + (b1_ref[:, c0:c0 + FF_CHUNK] + anchor)
            u = x_dot(D_FF + c0) + b1_ref[:, D_FF + c0:D_FF + c0 + FF_CHUNK]
            gate = jnp.minimum(g, SWIGLU_LIMIT)
            up = jnp.clip(u, -SWIGLU_LIMIT, SWIGLU_LIMIT)
            act = ((up + 1.0) * (gate * jax.nn.sigmoid(gate * SWIGLU_ALPHA))).astype(_BF16)
            part = jnp.dot(act, w2b[c0:c0 + FF_CHUNK, :], preferred_element_type=_F32)
            y = part + b2_ref[...] if y is None else y + part

        @pl.when(i >= 1)
        def _():
            wait_scatter(cur)

        ybuf[cur][rows, :] = _pack_rows(y[:, :HALF].astype(_BF16), y[:, HALF:].astype(_BF16))

        @pl.when(i == nact - 1)
        def _():
            sidx_copy(0, cur).wait()
            rolled(scatter_row, cur)
            wait_scatter(nxt)
            wait_scatter(cur)
            wait_gather(nxt)
            gidx_copy(0, cur).wait()

    new_expert = jnp.logical_or(i == 0, bexp_ref[i] != bexp_ref[jnp.maximum(i - 1, 0)])

    @pl.when(jnp.logical_and(i < nact, new_expert))
    def _():
        w1b[...] = w1_ref[...].astype(_BF16)
        w2b[...] = w2_ref[...].astype(_BF16)

    for parity in range(2):
        @pl.when(jnp.logical_and(i < nact, i % 2 == parity))
        def _():
            step(parity)


def _experts(blk_exp, n_active, gidx, sidx, h2, w1, b1, w2, b2, n_tokens):
    n_blk = gidx.shape[0]
    eb = EXPERT_BLOCK
    n_real = TOP_K * n_tokens
    wmap = lambda i, be, na: (be[i], 0, 0)
    grid_spec = pltpu.PrefetchScalarGridSpec(
        num_scalar_prefetch=2,
        grid=(n_blk,),
        in_specs=[
            pl.BlockSpec(memory_space=pl.ANY),
            pl.BlockSpec(memory_space=pl.ANY),
            pl.BlockSpec(memory_space=pl.ANY),
            pl.BlockSpec((None, D_MODEL, 2 * D_FF), wmap),
            pl.BlockSpec((None, 1, 2 * D_FF), wmap),
            pl.BlockSpec((None, D_FF, D_MODEL), wmap),
            pl.BlockSpec((None, 1, D_MODEL), wmap),
        ],
        out_specs=pl.BlockSpec(memory_space=pl.ANY),
        scratch_shapes=[
            pltpu.SMEM((eb,), jnp.int32), pltpu.SMEM((eb,), jnp.int32),
            pltpu.SMEM((eb,), jnp.int32), pltpu.SMEM((eb,), jnp.int32),
            pltpu.VMEM((eb + SUBLANES, HALF), jnp.uint32),
            pltpu.VMEM((eb + SUBLANES, HALF), jnp.uint32),
            pltpu.VMEM((eb + SUBLANES, HALF), jnp.uint32),
            pltpu.VMEM((eb + SUBLANES, HALF), jnp.uint32),
            pltpu.VMEM((D_MODEL, 2 * D_FF), _BF16),
            pltpu.VMEM((D_FF, D_MODEL), _BF16),
            pltpu.SemaphoreType.DMA((2, 2)),
            pltpu.SemaphoreType.DMA((2,)),
            pltpu.SemaphoreType.DMA((2,)),
        ],
    )
    return pl.pallas_call(
        functools.partial(_expert_kernel, n_blocks=n_blk, n_real_rows=n_real),
        grid_spec=grid_spec,
        out_shape=jax.ShapeDtypeStruct((n_real + 2 * eb, HALF), jnp.uint32),
        compiler_params=pltpu.CompilerParams(
            dimension_semantics=("arbitrary",), vmem_limit_bytes=VMEM_LIMIT),
        name="experts",
    )(blk_exp, n_active, gidx, sidx, h2, w1, b1, w2, b2)


def _combine_kernel(x1_ref, y0_ref, y1_ref, y2_ref, y3_ref, wts_ref, g_ref, o_ref):
    wt = wts_ref[...].T
    y = x1_ref[...]
    for k, y_ref in enumerate((y0_ref, y1_ref, y2_ref, y3_ref)):
        y = y + wt[:, k:k + 1] * jnp.concatenate(_unpack_rows(y_ref[...]), axis=1)
    var = jnp.mean(y * y, axis=-1, keepdims=True)
    o_ref[...] = y * lax.rsqrt(var + RMS_EPS) * g_ref[...]


def _combine(x1, y, wts, gf):
    t = x1.shape[0]
    tm = TM_OUT
    nt = t // tm
    yspec = lambda k: pl.BlockSpec((tm, HALF), lambda i: (k * nt + i, 0))
    return pl.pallas_call(
        _combine_kernel,
        grid=(nt,),
        in_specs=[
            pl.BlockSpec((tm, D_MODEL), lambda i: (i, 0)),
            yspec(0), yspec(1), yspec(2), yspec(3),
            pl.BlockSpec((8, tm), lambda i: (0, i)),
            _const_spec((1, D_MODEL)),
        ],
        out_specs=pl.BlockSpec((tm, D_MODEL), lambda i: (i, 0)),
        out_shape=jax.ShapeDtypeStruct((t, D_MODEL), _F32),
        compiler_params=pltpu.CompilerParams(
            dimension_semantics=("parallel",), vmem_limit_bytes=VMEM_LIMIT),
        name="combine",
    )(x1, y, y, y, y, wts, gf)


def _split_bf16(w):
    hi = w.astype(_BF16)
    lo = (w - hi.astype(_F32)).astype(_BF16)
    return jnp.concatenate([hi, lo], axis=0)


def _rope_tables(seq):
    half = SWA_HEAD_DIM // 2
    inv_freq = np.float32(ROPE_THETA) ** (-np.arange(half, dtype=np.float32) / np.float32(half))
    ang = np.arange(seq, dtype=np.float32)[:, None] * inv_freq[None, :]
    cos, sin = np.cos(ang), np.sin(ang)
    cos_t = np.tile(np.concatenate([cos, cos], axis=1), (1, LANES // SWA_HEAD_DIM))
    sin_t = np.tile(np.concatenate([-sin, sin], axis=1), (1, LANES // SWA_HEAD_DIM))
    return jnp.asarray(cos_t, _F32), jnp.asarray(sin_t, _F32)


def _na_bias_table(rpb):
    col = np.arange(GRID_W)
    cstart = np.clip(col - NA_COLS // 2, 0, GRID_W - NA_COLS)
    kc = np.arange(GRID_W)
    valid = (kc[None, :] >= cstart[:, None]) & (kc[None, :] < cstart[:, None] + NA_COLS)
    off = kc[None, :] - col[:, None] + NA_COLS - 1
    pick = (off[:, :, None] == np.arange(2 * NA_COLS - 1)[None, None, :]) & valid[:, :, None]
    ext = jnp.einsum("hvo,cko->hvck", rpb.astype(_F32), jnp.asarray(pick, _F32),
                     precision=lax.Precision.HIGHEST)
    ext = jnp.where(valid[None, None], ext, NEG_BIG)
    tbl = jnp.stack([ext[:, NA_ROWS - 1 - d_:2 * NA_ROWS - 1 - d_] for d_ in range(NA_ROWS)], axis=1)
    tbl = tbl.transpose(1, 0, 3, 2, 4).reshape(NA_ROWS, NA_HEADS // 2, 2 * GRID_W, NA_ROWS * GRID_W)
    return (tbl * LOG2E).astype(_F32)


def kernel(x, norm1_g, w_in, b_in, na_rpb, swa_sinks, w_up_a, w_up_b, w_out, norm2_g, w_router,
           b_router, w1, b1, w2, b2, final_g):
    batch, seq, d = x.shape
    depth = w_in.shape[0]
    t = batch * seq
    assert depth == 1, "the final norm is fused into the single layer's combine step"
    assert d == D_MODEL and seq % TM_MIX == 0 and seq % TM_PROJ == 0 and t % TM_OUT == 0
    assert seq // GRID_W >= 2 * NA_ROWS

    group = SWA_Q_HEADS // SWA_KV_HEADS
    head_order = np.arange(SWA_Q_HEADS).reshape(SWA_KV_HEADS, group).T.reshape(-1)

    def reorder_heads(a, axis, start):
        take = lambda lo, hi: lax.slice_in_dim(a, lo, hi, axis=axis)
        heads = [take(start + h * SWA_HEAD_DIM, start + (h + 1) * SWA_HEAD_DIM) for h in head_order]
        return jnp.concatenate([take(0, start)] + heads + [take(start + SWA_Q_WIDTH, a.shape[axis])], axis=axis)

    cos_t, sin_t = _rope_tables(seq)

    n_assign = t * TOP_K
    assert n_assign <= PAD_FLAG
    n_rows = n_assign + N_EXPERTS * EXPERT_BLOCK
    n_blk = n_rows // EXPERT_BLOCK
    n_tiles = t // TM_MIX

    x2 = x.reshape(t, d)
    for l in range(depth):
        w_in_l = reorder_heads(w_in[l], 1, C_QB).astype(_BF16)
        b_in_l = reorder_heads(b_in[l], 0, C_QB).reshape(1, D_IN)
        qkva, qb, kvb, gates = _inproj(x2, norm1_g[l].reshape(1, d), w_in_l, b_in_l, cos_t, sin_t, seq)

        x1, h2, keys, wts, cnt = _mixer(
            swa_sinks[l][head_order].astype(_F32) * LOG2E, x2, qkva, qb, kvb, gates, _na_bias_table(na_rpb[l]),
            w_up_a[l].astype(_BF16), reorder_heads(w_up_b[l], 0, 0).astype(_BF16), w_out[l].astype(_BF16),
            norm2_g[l].reshape(1, d), _split_bf16(w_router[l].T), b_router[l].reshape(N_EXPERTS, 1),
            batch, seq)

        cnt = cnt.reshape(n_tiles, N_EXPERTS, LANES)[:, :, 0].astype(jnp.int32)
        counts = jnp.sum(cnt, axis=0)
        padded = (counts + EXPERT_BLOCK - 1) // EXPERT_BLOCK * EXPERT_BLOCK
        pend = jnp.cumsum(padded)
        pad_i = jnp.arange(EXPERT_BLOCK, dtype=jnp.int32)[None, :]
        pad_keys = jnp.where(pad_i < (padded - counts)[:, None],
                             (jnp.arange(N_EXPERTS, dtype=jnp.int32)[:, None] << KEY_SHIFT) | PAD_FLAG | pad_i,
                             jnp.iinfo(jnp.int32).max)
        sorted_keys = jnp.sort(jnp.concatenate([keys[:TOP_K].reshape(-1), pad_keys.reshape(-1)]))
        row_a = jnp.where((sorted_keys & PAD_FLAG) == 0, sorted_keys & (PAD_FLAG - 1), -1)
        row_a = row_a.reshape(n_blk, EXPERT_BLOCK)
        blk_start = jnp.arange(n_blk, dtype=jnp.int32) * EXPERT_BLOCK
        blk_exp = jnp.minimum(jnp.sum((pend[None, :] <= blk_start[:, None]).astype(jnp.int32), axis=1),
                              N_EXPERTS - 1)
        n_active = (pend[-1:] // EXPERT_BLOCK).astype(jnp.int32)
        r_in_blk = jnp.arange(EXPERT_BLOCK, dtype=jnp.int32)[None, :]
        parity = (jnp.arange(-1, n_blk, dtype=jnp.int32) % 2)[:, None]
        trash = n_assign + parity * EXPERT_BLOCK + r_in_blk
        gidx = jnp.where(row_a >= 0, row_a >> 2, 0)
        sidx = jnp.where(row_a >= 0, (row_a & (TOP_K - 1)) * t + (row_a >> 2), trash[1:])
        sidx = jnp.concatenate([trash[:1], sidx], axis=0)

        y = _experts(blk_exp, n_active, gidx, sidx, h2,
                     w1[l], b1[l].reshape(N_EXPERTS, 1, 2 * D_FF),
                     w2[l], b2[l].reshape(N_EXPERTS, 1, D_MODEL), t)
        x2 = _combine(x1, y, wts, final_g.reshape(1, d))
    return x2.reshape(batch, seq, d)
```

```python
import functools

import jax
import jax.numpy as jnp
import numpy as np
from jax import lax
from jax.experimental import pallas as pl
from jax.experimental.pallas import tpu as pltpu

D_MODEL = 1024
GRID_W = 64
NA_HEADS = 8
NA_HEAD_DIM = 64
NA_ROWS = 8
NA_COLS = 16
SWA_Q_HEADS = 8
SWA_KV_HEADS = 2
SWA_HEAD_DIM = 64
SWA_WINDOW = 128
SWA_BLOCK = 128
ROPE_THETA = 10000.0
N_EXPERTS = 32
TOP_K = 4
D_FF = 1024
SWIGLU_LIMIT = 7.0
SWIGLU_ALPHA = 1.702
EXPERT_BLOCK = 512
RMS_EPS = 1e-5

NA_WIDTH = NA_HEADS * NA_HEAD_DIM
SWA_Q_WIDTH = SWA_Q_HEADS * SWA_HEAD_DIM
SWA_KV_WIDTH = SWA_KV_HEADS * SWA_HEAD_DIM
LANES = 128
SUBLANES = 8
NEG_BIG = -1e30

C_QA, C_KA, C_VA = 0, NA_WIDTH, 2 * NA_WIDTH
C_QB = 3 * NA_WIDTH
C_KB = C_QB + SWA_Q_WIDTH
C_VB = C_KB + SWA_KV_WIDTH
C_GA = C_VB + SWA_KV_WIDTH
C_GB = C_GA + D_MODEL
D_IN = C_GB + D_MODEL

TM_PROJ = 1024
TM_MIX = 512
TM_OUT = 1024
GATE_CHUNK = 512
FF_CHUNK = 256
NA_BATCH_ROWS = 4
LOG2E = 1.4426950408889634
KEY_SHIFT = 18
PAD_FLAG = 1 << (KEY_SHIFT - 1)
VMEM_LIMIT = 56 * 1024 * 1024

_BF16 = jnp.bfloat16
_F32 = jnp.float32


def _const_spec(shape):
    nd = len(shape)
    return pl.BlockSpec(shape, lambda *_: (0,) * nd, pipeline_mode=pl.Buffered(1))


HALF = D_MODEL // 2
_HI_MASK = 0xFFFF0000


def _pack_rows(lo_bf16, hi_bf16):
    lo = lax.bitcast_convert_type(lo_bf16.astype(_F32), jnp.uint32) >> 16
    hi = lax.bitcast_convert_type(hi_bf16.astype(_F32), jnp.uint32) & jnp.uint32(_HI_MASK)
    return lo | hi


def _unpack_rows(u):
    lo = lax.bitcast_convert_type(u << 16, _F32)
    hi = lax.bitcast_convert_type(u & jnp.uint32(_HI_MASK), _F32)
    return lo, hi


def _rope_slab(y, cos, sin_signed):
    lane = lax.broadcasted_iota(jnp.int32, y.shape, 1)
    first_half = (lane & (SWA_HEAD_DIM - 1)) < (SWA_HEAD_DIM // 2)
    rot = jnp.where(first_half, pltpu.roll(y, LANES - SWA_HEAD_DIM // 2, axis=1),
                    pltpu.roll(y, SWA_HEAD_DIM // 2, axis=1))
    return y * cos + rot * sin_signed


def _inproj_kernel(x_ref, g_ref, w_ref, b_ref, cos_ref, sin_ref,
                   qkva_ref, qb_ref, kvb_ref, gate_ref):
    x = x_ref[...]
    var = jnp.mean(x * x, axis=-1, keepdims=True)
    h = (x * lax.rsqrt(var + RMS_EPS) * g_ref[...]).astype(_BF16)

    def proj(c0, c1):
        return jnp.dot(h, w_ref[:, c0:c1], preferred_element_type=_F32) + b_ref[:, c0:c1]

    scale = NA_HEAD_DIM ** -0.5 * LOG2E
    qkva_ref[:, C_QA:C_KA] = (proj(C_QA, C_KA) * scale).astype(_BF16)
    qkva_ref[:, C_KA:C_VA] = proj(C_KA, C_VA).astype(_BF16)
    qkva_ref[:, C_VA:C_QB] = proj(C_VA, C_QB).astype(_BF16)

    cos = cos_ref[...]
    sin = sin_ref[...]
    qb = proj(C_QB, C_KB)
    n_slabs = SWA_Q_WIDTH // LANES
    low = lax.broadcasted_iota(jnp.int32, (x.shape[0], LANES), 1) < SWA_HEAD_DIM
    for s in range(n_slabs):
        a = qb[:, (s // 2) * LANES:(s // 2 + 1) * LANES]
        b = qb[:, (s // 2 + n_slabs // 2) * LANES:(s // 2 + n_slabs // 2 + 1) * LANES]
        if s % 2 == 0:
            slab = jnp.where(low, a, pltpu.roll(b, SWA_HEAD_DIM, axis=1))
        else:
            slab = jnp.where(low, pltpu.roll(a, SWA_HEAD_DIM, axis=1), b)
        slab = _rope_slab(slab, cos, sin)
        qb_ref[:, s * LANES:(s + 1) * LANES] = (slab * (SWA_HEAD_DIM ** -0.5 * LOG2E)).astype(_BF16)
    kvb = proj(C_KB, C_GA)
    kvb_ref[:, 0:LANES] = _rope_slab(kvb[:, 0:LANES], cos, sin).astype(_BF16)
    kvb_ref[:, LANES:2 * LANES] = kvb[:, LANES:2 * LANES].astype(_BF16)

    for c0 in range(C_GA, D_IN, GATE_CHUNK):
        gate_ref[:, c0 - C_GA:c0 - C_GA + GATE_CHUNK] = jax.nn.sigmoid(proj(c0, c0 + GATE_CHUNK)).astype(_BF16)


def _inproj(x2, g1, w_in, b_in, cos_t, sin_t, seq):
    t = x2.shape[0]
    tm = TM_PROJ
    nseq = seq // tm
    row = lambda i: (i, 0)
    return pl.pallas_call(
        _inproj_kernel,
        grid=(t // tm,),
        in_specs=[
            pl.BlockSpec((tm, D_MODEL), row),
            _const_spec((1, D_MODEL)),
            _const_spec((D_MODEL, D_IN)),
            _const_spec((1, D_IN)),
            pl.BlockSpec((tm, LANES), lambda i: (i % nseq, 0)),
            pl.BlockSpec((tm, LANES), lambda i: (i % nseq, 0)),
        ],
        out_specs=[
            pl.BlockSpec((tm, 3 * NA_WIDTH), row),
            pl.BlockSpec((tm, SWA_Q_WIDTH), row),
            pl.BlockSpec((tm, 2 * SWA_KV_WIDTH), row),
            pl.BlockSpec((tm, 2 * D_MODEL), row),
        ],
        out_shape=[
            jax.ShapeDtypeStruct((t, 3 * NA_WIDTH), _BF16),
            jax.ShapeDtypeStruct((t, SWA_Q_WIDTH), _BF16),
            jax.ShapeDtypeStruct((t, 2 * SWA_KV_WIDTH), _BF16),
            jax.ShapeDtypeStruct((t, 2 * D_MODEL), _BF16),
        ],
        compiler_params=pltpu.CompilerParams(
            dimension_semantics=("parallel",), vmem_limit_bytes=VMEM_LIMIT),
        name="inproj",
    )(x2, g1, w_in, b_in, cos_t, sin_t)


_NT_DIMS = (((1,), (1,)), ((), ()))


def _mixer_kernel(sink_ref, x_ref, q_ref, kp_ref, kc_ref, kn_ref, vp_ref, vc_ref, vn_ref,
                  qb_ref, kvp_ref, kvc_ref, kvn_ref, gate_ref, tbl_ref, mask_ref,
                  wua_ref, wub_ref, wo_ref, g2_ref, wr_ref, br_ref,
                  x1_ref, h2_ref, key_ref, wts_ref, cnt_ref,
                  kcat, vcat, kvcat, oa_s, ob_s, s_s, e_s, inv_s, *, rows, n_swa_blocks):
    j = pl.program_id(1)
    rows_per_tile = TM_MIX // GRID_W
    halo = (NA_ROWS // 2) * GRID_W
    band = NA_ROWS * GRID_W

    kcat[0:halo] = kp_ref[...]
    kcat[halo:halo + TM_MIX] = kc_ref[...]
    kcat[halo + TM_MIX:] = kn_ref[...]
    vcat[0:halo] = vp_ref[...]
    vcat[halo:halo + TM_MIX] = vc_ref[...]
    vcat[halo + TM_MIX:] = vn_ref[...]

    lane_q = lax.broadcasted_iota(jnp.int32, (GRID_W, LANES), 1)
    low_q = lane_q < NA_HEAD_DIM
    lane_o = lax.broadcasted_iota(jnp.int32, (GRID_W, LANES), 1) < NA_HEAD_DIM

    n_pairs = NA_HEADS // 2
    chain = 2 * GRID_W

    def band_start(i):
        r = j * rows_per_tile + i
        rs = jnp.clip(r - NA_ROWS // 2, 0, rows - NA_ROWS)
        start = pl.multiple_of((rs - (j * rows_per_tile - NA_ROWS // 2)) * GRID_W, GRID_W)
        return start, r - rs

    for half in range(rows_per_tile // NA_BATCH_ROWS):
        def na_scores(b, carry):
            i = half * NA_BATCH_ROWS + b
            start, d = band_start(i)
            q0 = pl.multiple_of(i * GRID_W, GRID_W)
            for p in range(n_pairs):
                cols = slice(p * LANES, (p + 1) * LANES)
                qpair = q_ref[pl.ds(q0, GRID_W), cols]
                zero = jnp.zeros_like(qpair)
                qs = jnp.concatenate([jnp.where(low_q, qpair, zero), jnp.where(low_q, zero, qpair)], axis=0)
                kb = kcat[pl.ds(start, band), cols]
                s = lax.dot_general(qs, kb, _NT_DIMS, preferred_element_type=_F32) + tbl_ref[d, p]
                s_s[pl.ds(pl.multiple_of((b * n_pairs + p) * chain, chain), chain), :] = s
            return carry

        def na_softmax(b, carry):
            rws = pl.ds(pl.multiple_of(b * n_pairs * chain, n_pairs * chain), n_pairs * chain)
            s = s_s[rws, :]
            e = jnp.exp2(s - jnp.max(s, axis=-1, keepdims=True))
            e_s[rws, :] = e.astype(_BF16)
            inv_s[rws, :] = jnp.broadcast_to(1.0 / jnp.sum(e, axis=-1, keepdims=True), (n_pairs * chain, LANES))
            return carry

        def na_values(b, carry):
            i = half * NA_BATCH_ROWS + b
            start, _ = band_start(i)
            q0 = pl.multiple_of(i * GRID_W, GRID_W)
            for p in range(n_pairs):
                cols = slice(p * LANES, (p + 1) * LANES)
                rws = pl.ds(pl.multiple_of((b * n_pairs + p) * chain, chain), chain)
                vb = vcat[pl.ds(start, band), cols]
                o = jnp.dot(e_s[rws, :], vb, preferred_element_type=_F32) * inv_s[rws, :]
                oa_s[pl.ds(q0, GRID_W), cols] = jnp.where(lane_o, o[:GRID_W], o[GRID_W:]).astype(_BF16)
            return carry

        lax.fori_loop(0, NA_BATCH_ROWS, na_scores, 0)
        lax.fori_loop(0, NA_BATCH_ROWS, na_softmax, 0)
        lax.fori_loop(0, NA_BATCH_ROWS, na_values, 0)

    kvcat[0:SWA_BLOCK] = kvp_ref[...]
    kvcat[SWA_BLOCK:SWA_BLOCK + TM_MIX] = kvc_ref[...]
    kvcat[SWA_BLOCK + TM_MIX:] = kvn_ref[...]

    n_slabs = SWA_Q_WIDTH // LANES
    stack = n_slabs * SWA_BLOCK
    lane_s = lax.broadcasted_iota(jnp.int32, (SWA_BLOCK, LANES), 1) < SWA_HEAD_DIM
    rowblk = lax.broadcasted_iota(jnp.int32, (stack, 1), 0) // SWA_BLOCK
    wband = 3 * SWA_BLOCK
    sinks = []
    for g in range(SWA_KV_HEADS):
        sink = jnp.zeros((stack, 1), _F32)
        for s_ in range(n_slabs):
            sink = jnp.where(rowblk == s_, sink_ref[SWA_KV_HEADS * s_ + g], sink)
        sinks.append(sink)

    def swa_block(n, carry):
        nb = j * (TM_MIX // SWA_BLOCK) + n
        t0 = pl.multiple_of(n * SWA_BLOCK, SWA_BLOCK)
        variant = jnp.where(nb == 0, 0, jnp.where(nb == n_swa_blocks - 1, 2, 1))
        mask = mask_ref[variant]
        mask = jnp.concatenate([mask] * n_slabs, axis=0)
        kband = kvcat[pl.ds(t0, wband), 0:LANES]
        vband = kvcat[pl.ds(t0, wband), LANES:2 * LANES]
        outs = []
        for g in range(SWA_KV_HEADS):
            parts = []
            for s_ in range(n_slabs):
                slab = qb_ref[pl.ds(t0, SWA_BLOCK), s_ * LANES:(s_ + 1) * LANES]
                zero = jnp.zeros_like(slab)
                parts.append(jnp.where(lane_s, slab, zero) if g == 0 else jnp.where(lane_s, zero, slab))
            qg = jnp.concatenate(parts, axis=0)
            s = lax.dot_general(qg, kband, _NT_DIMS, preferred_element_type=_F32) + mask
            m = jnp.maximum(jnp.max(s, axis=-1, keepdims=True), sinks[g])
            e = jnp.exp2(s - m)
            den = jnp.sum(e, axis=-1, keepdims=True) + jnp.exp2(sinks[g] - m)
            outs.append(jnp.dot(e.astype(_BF16), vband, preferred_element_type=_F32) * (1.0 / den))
        for s_ in range(n_slabs):
            rs_ = slice(s_ * SWA_BLOCK, (s_ + 1) * SWA_BLOCK)
            ob_s[pl.ds(t0, SWA_BLOCK), s_ * LANES:(s_ + 1) * LANES] = jnp.where(
                lane_s, outs[0][rs_], outs[1][rs_]).astype(_BF16)
        return carry

    lax.fori_loop(0, TM_MIX // SWA_BLOCK, swa_block, 0)

    ua = jnp.dot(oa_s[...], wua_ref[...], preferred_element_type=_F32)
    ub = jnp.dot(ob_s[...], wub_ref[...], preferred_element_type=_F32)
    merged = (gate_ref[:, 0:D_MODEL].astype(_F32) * ua
              + gate_ref[:, D_MODEL:].astype(_F32) * ub).astype(_BF16)
    x1 = x_ref[...] + jnp.dot(merged, wo_ref[...], preferred_element_type=_F32)
    x1_ref[...] = x1

    var = jnp.mean(x1 * x1, axis=-1, keepdims=True)
    h2 = x1 * lax.rsqrt(var + RMS_EPS) * g2_ref[...]
    h2_hi = h2.astype(_BF16)
    h2_ref[...] = _pack_rows(h2_hi[:, :HALF], h2_hi[:, HALF:])
    h2_lo = (h2 - h2_hi.astype(_F32)).astype(_BF16)
    both = lax.dot_general(wr_ref[...], h2_hi, _NT_DIMS, preferred_element_type=_F32)
    cross = lax.dot_general(wr_ref[0:N_EXPERTS], h2_lo, _NT_DIMS, preferred_element_type=_F32)
    logits = both[0:N_EXPERTS] + both[N_EXPERTS:] + cross + br_ref[...]
    eidx = lax.broadcasted_iota(jnp.int32, logits.shape, 0)
    vals, idxs, hots = [], [], []
    for _ in range(TOP_K):
        m = jnp.max(logits, axis=0, keepdims=True)
        idx = jnp.min(jnp.where(logits == m, eidx, N_EXPERTS), axis=0, keepdims=True)
        hot = eidx == idx
        logits = jnp.where(hot, -jnp.inf, logits)
        vals.append(m)
        idxs.append(idx)
        hots.append(hot)
    es = [jnp.exp(v - vals[0]) for v in vals]
    inv = 1.0 / (es[0] + es[1] + es[2] + es[3])
    sel = jnp.zeros(logits.shape, _F32)
    for hot in hots:
        sel = sel + jnp.where(hot, 1.0, 0.0)
    tile_idx = pl.program_id(0) * pl.num_programs(1) + j
    tok = tile_idx * TM_MIX + lax.broadcasted_iota(jnp.int32, (1, TM_MIX), 1)
    keys = [(idx << KEY_SHIFT) | (tok * TOP_K + k) for k, idx in enumerate(idxs)]
    key_ref[...] = jnp.concatenate(
        [key[:, c * LANES:(c + 1) * LANES] for key in keys for c in range(TM_MIX // LANES)], axis=0)
    wts_ref[...] = jnp.concatenate([e * inv for e in es] + [jnp.zeros((8 - TOP_K, TM_MIX), _F32)], axis=0)
    cnt_ref[...] = jnp.broadcast_to(jnp.sum(sel, axis=1, keepdims=True), (N_EXPERTS, LANES))


def _swa_mask_table():
    qpos = np.arange(SWA_BLOCK)[:, None]
    koff = np.arange(3 * SWA_BLOCK)[None, :] - SWA_BLOCK
    rel_ok = np.abs(koff - qpos) <= SWA_WINDOW
    has_prev = np.array([False, True, True])[:, None, None]
    has_next = np.array([True, True, False])[:, None, None]
    ok = rel_ok[None] & ((koff >= 0)[None] | has_prev) & ((koff < SWA_BLOCK)[None] | has_next)
    return jnp.asarray(np.where(ok, 0.0, NEG_BIG), _F32)


def _mixer(sinks_perm, x2, qkva, qb, kvb, gates, tbl, wua, wub, wo, g2, wr_t, br, batch, seq):
    t = x2.shape[0]
    tm = TM_MIX
    nj = seq // tm
    rows = seq // GRID_W
    hb = tm // ((NA_ROWS // 2) * GRID_W)
    sb = tm // SWA_BLOCK
    n_halo = seq // ((NA_ROWS // 2) * GRID_W)
    n_swa = seq // SWA_BLOCK
    halo = (NA_ROWS // 2) * GRID_W
    swa_mask = _swa_mask_table()
    score_rows = NA_BATCH_ROWS * NA_HEADS * GRID_W
    assert n_swa >= 2

    tile = lambda b, j, *_: (b * nj + j, 0)

    def na_spec(col, which):
        if which == 0:
            return pl.BlockSpec((tm, NA_WIDTH), lambda b, j, *_: (b * nj + j, col))
        if which < 0:
            return pl.BlockSpec((halo, NA_WIDTH),
                                lambda b, j, *_: (b * n_halo + jnp.maximum(j * hb - 1, 0), col))
        return pl.BlockSpec((halo, NA_WIDTH),
                            lambda b, j, *_: (b * n_halo + jnp.minimum(j * hb + hb, n_halo - 1), col))

    kv_prev = pl.BlockSpec((SWA_BLOCK, 2 * SWA_KV_WIDTH),
                           lambda b, j, *_: (b * n_swa + jnp.maximum(j * sb - 1, 0), 0))
    kv_next = pl.BlockSpec((SWA_BLOCK, 2 * SWA_KV_WIDTH),
                           lambda b, j, *_: (b * n_swa + jnp.minimum(j * sb + sb, n_swa - 1), 0))

    cspec = _const_spec

    grid_spec = pltpu.PrefetchScalarGridSpec(
        num_scalar_prefetch=1,
        grid=(batch, nj),
        in_specs=[
            pl.BlockSpec((tm, D_MODEL), tile),
            na_spec(0, 0),
            na_spec(1, -1), na_spec(1, 0), na_spec(1, 1),
            na_spec(2, -1), na_spec(2, 0), na_spec(2, 1),
            pl.BlockSpec((tm, SWA_Q_WIDTH), tile),
            kv_prev, pl.BlockSpec((tm, 2 * SWA_KV_WIDTH), tile), kv_next,
            pl.BlockSpec((tm, 2 * D_MODEL), tile),
            cspec(tbl.shape), cspec(swa_mask.shape),
            cspec(wua.shape), cspec(wub.shape), cspec(wo.shape),
            cspec(g2.shape), cspec(wr_t.shape), cspec(br.shape),
        ],
        out_specs=[
            pl.BlockSpec((tm, D_MODEL), tile),
            pl.BlockSpec((tm, HALF), tile),
            pl.BlockSpec((TOP_K * tm // LANES, LANES), tile),
            pl.BlockSpec((8, tm), lambda b, j, *_: (0, b * nj + j)),
            pl.BlockSpec((N_EXPERTS, LANES), tile),
        ],
        scratch_shapes=[
            pltpu.VMEM((tm + 2 * halo, NA_WIDTH), _BF16),
            pltpu.VMEM((tm + 2 * halo, NA_WIDTH), _BF16),
            pltpu.VMEM((tm + 2 * SWA_BLOCK, 2 * SWA_KV_WIDTH), _BF16),
            pltpu.VMEM((tm, NA_WIDTH), _BF16),
            pltpu.VMEM((tm, SWA_Q_WIDTH), _BF16),
            pltpu.VMEM((score_rows, NA_ROWS * GRID_W), _F32),
            pltpu.VMEM((score_rows, NA_ROWS * GRID_W), _BF16),
            pltpu.VMEM((score_rows, LANES), _F32),
        ],
    )
    return pl.pallas_call(
        functools.partial(_mixer_kernel, rows=rows, n_swa_blocks=n_swa),
        grid_spec=grid_spec,
        out_shape=[
            jax.ShapeDtypeStruct((t, D_MODEL), _F32),
            jax.ShapeDtypeStruct((t, HALF), jnp.uint32),
            jax.ShapeDtypeStruct((TOP_K * t // LANES, LANES), jnp.int32),
            jax.ShapeDtypeStruct((8, t), _F32),
            jax.ShapeDtypeStruct((t // tm * N_EXPERTS, LANES), _F32),
        ],
        compiler_params=pltpu.CompilerParams(
            dimension_semantics=("parallel", "parallel"), vmem_limit_bytes=VMEM_LIMIT),
        name="mixer",
    )(sinks_perm, x2, qkva, qkva, qkva, qkva, qkva, qkva, qkva, qb, kvb, kvb, kvb, gates, tbl, swa_mask,
      wua, wub, wo, g2, wr_t, br)


def _expert_kernel(bexp_ref, nact_ref, gidx_hbm, sidx_hbm, h2_hbm, w1_ref, b1_ref, w2_ref, b2_ref,
                   y_hbm, gs0, gs1, ss0, ss1, xbuf0, xbuf1, ybuf0, ybuf1, w1b, w2b, isem, gsem, ssem,
                   *, n_blocks, n_real_rows):
    i = pl.program_id(0)
    nact = nact_ref[0]
    eb = EXPERT_BLOCK
    gs, ss, xbuf, ybuf = (gs0, gs1), (ss0, ss1), (xbuf0, xbuf1), (ybuf0, ybuf1)
    rows = pl.ds(0, eb)

    def gidx_copy(blk, slot):
        return pltpu.make_async_copy(gidx_hbm.at[blk], gs[slot], isem.at[0, slot])

    def sidx_copy(entry, slot):
        return pltpu.make_async_copy(sidx_hbm.at[entry], ss[slot], isem.at[1, slot])

    def gather_row(slot, r, priority=0):
        pltpu.make_async_copy(h2_hbm.at[pl.ds(gs[slot][r], 1), :], xbuf[slot].at[pl.ds(r, 1), :],
                              gsem.at[slot]).start(priority=priority)

    def wait_gather(slot):
        pltpu.make_async_copy(h2_hbm.at[rows, :], xbuf[slot].at[rows, :], gsem.at[slot]).wait()

    def scatter_row(slot, r, priority=0):
        pltpu.make_async_copy(ybuf[slot].at[pl.ds(r, 1), :], y_hbm.at[pl.ds(ss[slot][r], 1), :],
                              ssem.at[slot]).start(priority=priority)

    def wait_scatter(slot):
        pltpu.make_async_copy(ybuf[slot].at[rows, :], y_hbm.at[rows, :], ssem.at[slot]).wait()

    def rolled(fn, slot):
        def body(r, c):
            fn(slot, r)
            return c
        lax.fori_loop(0, eb, body, 0)

    @pl.when(i == 0)
    def _():
        gidx_copy(0, 0).start()
        gidx_copy(0, 0).wait()
        rolled(gather_row, 0)
        gidx_copy(jnp.minimum(1, n_blocks - 1), 1).start()
        sidx_copy(0, 1).start()
        ybuf1[...] = jnp.zeros(ybuf1.shape, jnp.uint32)
        init = pltpu.make_async_copy(ybuf1.at[rows, :], y_hbm.at[pl.ds(n_real_rows, eb), :], ssem.at[0])
        init.start()
        init.wait()

    def step(cur):
        nxt = 1 - cur
        gidx_copy(jnp.minimum(i + 2, n_blocks - 1), cur).start()
        sidx_copy(i + 1, cur).start()
        gidx_copy(0, nxt).wait()
        sidx_copy(0, nxt).wait()
        wait_gather(cur)

        x_lo, x_hi = (half.astype(_BF16) for half in _unpack_rows(xbuf[cur][rows, :]))

        def x_dot(c0):
            return (jnp.dot(x_lo, w1b[0:HALF, c0:c0 + FF_CHUNK], preferred_element_type=_F32)
                    + jnp.dot(x_hi, w1b[HALF:, c0:c0 + FF_CHUNK], preferred_element_type=_F32))

        n_chunks = D_FF // FF_CHUNK
        rows_per_group = 2 * eb // n_chunks
        spare = pl.ds(pl.multiple_of(eb + jnp.minimum(i, 0) * SUBLANES, SUBLANES), SUBLANES)
        y = None
        for jc in range(n_chunks):
            issue = gather_row if jc < n_chunks // 2 else scatter_row
            first = (jc % (n_chunks // 2)) * rows_per_group
            for r in range(first, first + rows_per_group):
                issue(nxt, r, r % 2)
            xbuf[nxt][spare, 0:FF_CHUNK] = jnp.zeros((SUBLANES, FF_CHUNK), jnp.uint32)
            ybuf[nxt][spare, 0:FF_CHUNK] = jnp.zeros((SUBLANES, FF_CHUNK), jnp.uint32)
            anchor = lax.bitcast_convert_type(
                (xbuf[nxt][spare, 0:FF_CHUNK] | ybuf[nxt][spare, 0:FF_CHUNK])[0:1], _F32)
            c0 = jc * FF_CHUNK
            g = x_dot(c0) + (b1_ref[:, c0:c0 + FF_CHUNK] + anchor)
            u = x_dot(D_FF + c0) + b1_ref[:, D_FF + c0:D_FF + c0 + FF_CHUNK]
            gate = jnp.minimum(g, SWIGLU_LIMIT)
            up = jnp.clip(u, -SWIGLU_LIMIT, SWIGLU_LIMIT)
            act = ((up + 1.0) * (gate * jax.nn.sigmoid(gate * SWIGLU_ALPHA))).astype(_BF16)
            part = jnp.dot(act, w2b[c0:c0 + FF_CHUNK, :], preferred_element_type=_F32)
            y = part + b2_ref[...] if y is None else y + part

        @pl.when(i >= 1)
        def _():
            wait_scatter(cur)

        ybuf[cur][rows, :] = _pack_rows(y[:, :HALF].astype(_BF16), y[:, HALF:].astype(_BF16))

        @pl.when(i == nact - 1)
        def _():
            sidx_copy(0, cur).wait()
            rolled(scatter_row, cur)
            wait_scatter(nxt)
            wait_scatter(cur)
            wait_gather(nxt)
            gidx_copy(0, cur).wait()

    new_expert = jnp.logical_or(i == 0, bexp_ref[i] != bexp_ref[jnp.maximum(i - 1, 0)])

    @pl.when(jnp.logical_and(i < nact, new_expert))
    def _():
        w1b[...] = w1_ref[...].astype(_BF16)
        w2b[...] = w2_ref[...].astype(_BF16)

    for parity in range(2):
        @pl.when(jnp.logical_and(i < nact, i % 2 == parity))
        def _():
            step(parity)


def _experts(blk_exp, n_active, gidx, sidx, h2, w1, b1, w2, b2, n_tokens):
    n_blk = gidx.shape[0]
    eb = EXPERT_BLOCK
    n_real = TOP_K * n_tokens
    wmap = lambda i, be, na: (be[i], 0, 0)
    grid_spec = pltpu.PrefetchScalarGridSpec(
        num_scalar_prefetch=2,
        grid=(n_blk,),
        in_specs=[
            pl.BlockSpec(memory_space=pl.ANY),
            pl.BlockSpec(memory_space=pl.ANY),
            pl.BlockSpec(memory_space=pl.ANY),
            pl.BlockSpec((None, D_MODEL, 2 * D_FF), wmap),
            pl.BlockSpec((None, 1, 2 * D_FF), wmap),
            pl.BlockSpec((None, D_FF, D_MODEL), wmap),
            pl.BlockSpec((None, 1, D_MODEL), wmap),
        ],
        out_specs=pl.BlockSpec(memory_space=pl.ANY),
        scratch_shapes=[
            pltpu.SMEM((eb,), jnp.int32), pltpu.SMEM((eb,), jnp.int32),
            pltpu.SMEM((eb,), jnp.int32), pltpu.SMEM((eb,), jnp.int32),
            pltpu.VMEM((eb + SUBLANES, HALF), jnp.uint32),
            pltpu.VMEM((eb + SUBLANES, HALF), jnp.uint32),
            pltpu.VMEM((eb + SUBLANES, HALF), jnp.uint32),
            pltpu.VMEM((eb + SUBLANES, HALF), jnp.uint32),
            pltpu.VMEM((D_MODEL, 2 * D_FF), _BF16),
            pltpu.VMEM((D_FF, D_MODEL), _BF16),
            pltpu.SemaphoreType.DMA((2, 2)),
            pltpu.SemaphoreType.DMA((2,)),
            pltpu.SemaphoreType.DMA((2,)),
        ],
    )
    return pl.pallas_call(
        functools.partial(_expert_kernel, n_blocks=n_blk, n_real_rows=n_real),
        grid_spec=grid_spec,
        out_shape=jax.ShapeDtypeStruct((n_real + 2 * eb, HALF), jnp.uint32),
        compiler_params=pltpu.CompilerParams(
            dimension_semantics=("arbitrary",), vmem_limit_bytes=VMEM_LIMIT),
        name="experts",
    )(blk_exp, n_active, gidx, sidx, h2, w1, b1, w2, b2)


def _combine_kernel(x1_ref, y0_ref, y1_ref, y2_ref, y3_ref, wts_ref, g_ref, o_ref):
    wt = wts_ref[...].T
    y = x1_ref[...]
    for k, y_ref in enumerate((y0_ref, y1_ref, y2_ref, y3_ref)):
        y = y + wt[:, k:k + 1] * jnp.concatenate(_unpack_rows(y_ref[...]), axis=1)
    var = jnp.mean(y * y, axis=-1, keepdims=True)
    o_ref[...] = y * lax.rsqrt(var + RMS_EPS) * g_ref[...]


def _combine(x1, y, wts, gf):
    t = x1.shape[0]
    tm = TM_OUT
    nt = t // tm
    yspec = lambda k: pl.BlockSpec((tm, HALF), lambda i: (k * nt + i, 0))
    return pl.pallas_call(
        _combine_kernel,
        grid=(nt,),
        in_specs=[
            pl.BlockSpec((tm, D_MODEL), lambda i: (i, 0)),
            yspec(0), yspec(1), yspec(2), yspec(3),
            pl.BlockSpec((8, tm), lambda i: (0, i)),
            _const_spec((1, D_MODEL)),
        ],
        out_specs=pl.BlockSpec((tm, D_MODEL), lambda i: (i, 0)),
        out_shape=jax.ShapeDtypeStruct((t, D_MODEL), _F32),
        compiler_params=pltpu.CompilerParams(
            dimension_semantics=("parallel",), vmem_limit_bytes=VMEM_LIMIT),
        name="combine",
    )(x1, y, y, y, y, wts, gf)


def _split_bf16(w):
    hi = w.astype(_BF16)
    lo = (w - hi.astype(_F32)).astype(_BF16)
    return jnp.concatenate([hi, lo], axis=0)


def _rope_tables(seq):
    half = SWA_HEAD_DIM // 2
    inv_freq = np.float32(ROPE_THETA) ** (-np.arange(half, dtype=np.float32) / np.float32(half))
    ang = np.arange(seq, dtype=np.float32)[:, None] * inv_freq[None, :]
    cos, sin = np.cos(ang), np.sin(ang)
    cos_t = np.tile(np.concatenate([cos, cos], axis=1), (1, LANES // SWA_HEAD_DIM))
    sin_t = np.tile(np.concatenate([-sin, sin], axis=1), (1, LANES // SWA_HEAD_DIM))
    return jnp.asarray(cos_t, _F32), jnp.asarray(sin_t, _F32)


def _na_bias_table(rpb):
    col = np.arange(GRID_W)
    cstart = np.clip(col - NA_COLS // 2, 0, GRID_W - NA_COLS)
    kc = np.arange(GRID_W)
    valid = (kc[None, :] >= cstart[:, None]) & (kc[None, :] < cstart[:, None] + NA_COLS)
    off = kc[None, :] - col[:, None] + NA_COLS - 1
    pick = (off[:, :, None] == np.arange(2 * NA_COLS - 1)[None, None, :]) & valid[:, :, None]
    ext = jnp.einsum("hvo,cko->hvck", rpb.astype(_F32), jnp.asarray(pick, _F32),
                     precision=lax.Precision.HIGHEST)
    ext = jnp.where(valid[None, None], ext, NEG_BIG)
    tbl = jnp.stack([ext[:, NA_ROWS - 1 - d_:2 * NA_ROWS - 1 - d_] for d_ in range(NA_ROWS)], axis=1)
    tbl = tbl.transpose(1, 0, 3, 2, 4).reshape(NA_ROWS, NA_HEADS // 2, 2 * GRID_W, NA_ROWS * GRID_W)
    return (tbl * LOG2E).astype(_F32)


def kernel(x, norm1_g, w_in, b_in, na_rpb, swa_sinks, w_up_a, w_up_b, w_out, norm2_g, w_router,
           b_router, w1, b1, w2, b2, final_g):
    batch, seq, d = x.shape
    depth = w_in.shape[0]
    t = batch * seq
    assert depth == 1, "the final norm is fused into the single layer's combine step"
    assert d == D_MODEL and seq % TM_MIX == 0 and seq % TM_PROJ == 0 and t % TM_OUT == 0
    assert seq // GRID_W >= 2 * NA_ROWS

    group = SWA_Q_HEADS // SWA_KV_HEADS
    head_order = np.arange(SWA_Q_HEADS).reshape(SWA_KV_HEADS, group).T.reshape(-1)

    def reorder_heads(a, axis, start):
        take = lambda lo, hi: lax.slice_in_dim(a, lo, hi, axis=axis)
        heads = [take(start + h * SWA_HEAD_DIM, start + (h + 1) * SWA_HEAD_DIM) for h in head_order]
        return jnp.concatenate([take(0, start)] + heads + [take(start + SWA_Q_WIDTH, a.shape[axis])], axis=axis)

    cos_t, sin_t = _rope_tables(seq)

    n_assign = t * TOP_K
    assert n_assign <= PAD_FLAG
    n_rows = n_assign + N_EXPERTS * EXPERT_BLOCK
    n_blk = n_rows // EXPERT_BLOCK
    n_tiles = t // TM_MIX

    x2 = x.reshape(t, d)
    for l in range(depth):
        w_in_l = w_in[l].astype(_BF16)
        b_in_l = b_in[l].reshape(1, D_IN)
        qkva, qb, kvb, gates = _inproj(x2, norm1_g[l].reshape(1, d), w_in_l, b_in_l, cos_t, sin_t, seq)

        x1, h2, keys, wts, cnt = _mixer(
            swa_sinks[l][head_order].astype(_F32) * LOG2E, x2, qkva, qb, kvb, gates, _na_bias_table(na_rpb[l]),
            w_up_a[l].astype(_BF16), reorder_heads(w_up_b[l], 0, 0).astype(_BF16), w_out[l].astype(_BF16),
            norm2_g[l].reshape(1, d), _split_bf16(w_router[l].T), b_router[l].reshape(N_EXPERTS, 1),
            batch, seq)

        cnt = cnt.reshape(n_tiles, N_EXPERTS, LANES)[:, :, 0].astype(jnp.int32)
        counts = jnp.sum(cnt, axis=0)
        padded = (counts + EXPERT_BLOCK - 1) // EXPERT_BLOCK * EXPERT_BLOCK
        pend = jnp.cumsum(padded)
        pad_i = jnp.arange(EXPERT_BLOCK, dtype=jnp.int32)[None, :]
        pad_keys = jnp.where(pad_i < (padded - counts)[:, None],
                             (jnp.arange(N_EXPERTS, dtype=jnp.int32)[:, None] << KEY_SHIFT) | PAD_FLAG | pad_i,
                             jnp.iinfo(jnp.int32).max)
        sorted_keys = jnp.sort(jnp.concatenate([keys.reshape(-1), pad_keys.reshape(-1)]))
        row_a = jnp.where((sorted_keys & PAD_FLAG) == 0, sorted_keys & (PAD_FLAG - 1), -1)
        row_a = row_a.reshape(n_blk, EXPERT_BLOCK)
        blk_start = jnp.arange(n_blk, dtype=jnp.int32) * EXPERT_BLOCK
        blk_exp = jnp.minimum(jnp.sum((pend[None, :] <= blk_start[:, None]).astype(jnp.int32), axis=1),
                              N_EXPERTS - 1)
        n_active = (pend[-1:] // EXPERT_BLOCK).astype(jnp.int32)
        r_in_blk = jnp.arange(EXPERT_BLOCK, dtype=jnp.int32)[None, :]
        parity = (jnp.arange(-1, n_blk, dtype=jnp.int32) % 2)[:, None]
        trash = n_assign + parity * EXPERT_BLOCK + r_in_blk
        gidx = jnp.where(row_a >= 0, row_a >> 2, 0)
        sidx = jnp.where(row_a >= 0, (row_a & (TOP_K - 1)) * t + (row_a >> 2), trash[1:])
        sidx = jnp.concatenate([trash[:1], sidx], axis=0)

        y = _experts(blk_exp, n_active, gidx, sidx, h2,
                     w1[l], b1[l].reshape(N_EXPERTS, 1, 2 * D_FF),
                     w2[l], b2[l].reshape(N_EXPERTS, 1, D_MODEL), t)
        x2 = _combine(x1, y, wts, final_g.reshape(1, d))
    return x2.reshape(batch, seq, d)
```

```python
import functools

import jax
import jax.numpy as jnp
import numpy as np
from jax import lax
from jax.experimental import pallas as pl
from jax.experimental.pallas import tpu as pltpu

D_MODEL = 1024
GRID_W = 64
NA_HEADS = 8
NA_HEAD_DIM = 64
NA_ROWS = 8
NA_COLS = 16
SWA_Q_HEADS = 8
SWA_KV_HEADS = 2
SWA_HEAD_DIM = 64
SWA_WINDOW = 128
SWA_BLOCK = 128
ROPE_THETA = 10000.0
N_EXPERTS = 32
TOP_K = 4
D_FF = 1024
SWIGLU_LIMIT = 7.0
SWIGLU_ALPHA = 1.702
EXPERT_BLOCK = 512
RMS_EPS = 1e-5

NA_WIDTH = NA_HEADS * NA_HEAD_DIM
SWA_Q_WIDTH = SWA_Q_HEADS * SWA_HEAD_DIM
SWA_KV_WIDTH = SWA_KV_HEADS * SWA_HEAD_DIM
LANES = 128
SUBLANES = 8
NEG_BIG = -1e30

C_QA, C_KA, C_VA = 0, NA_WIDTH, 2 * NA_WIDTH
C_QB = 3 * NA_WIDTH
C_KB = C_QB + SWA_Q_WIDTH
C_VB = C_KB + SWA_KV_WIDTH
C_GA = C_VB + SWA_KV_WIDTH
C_GB = C_GA + D_MODEL
D_IN = C_GB + D_MODEL

TM_PROJ = 1024
TM_MIX = 512
TM_OUT = 1024
GATE_CHUNK = 512
FF_CHUNK = 256
NA_BATCH_ROWS = 4
LOG2E = 1.4426950408889634
KEY_SHIFT = 18
PAD_FLAG = 1 << (KEY_SHIFT - 1)
VMEM_LIMIT = 56 * 1024 * 1024

_BF16 = jnp.bfloat16
_F32 = jnp.float32


def _const_spec(shape):
    nd = len(shape)
    return pl.BlockSpec(shape, lambda *_: (0,) * nd, pipeline_mode=pl.Buffered(1))


HALF = D_MODEL // 2
_HI_MASK = 0xFFFF0000


def _pack_rows(lo_bf16, hi_bf16):
    lo = lax.bitcast_convert_type(lo_bf16.astype(_F32), jnp.uint32) >> 16
    hi = lax.bitcast_convert_type(hi_bf16.astype(_F32), jnp.uint32) & jnp.uint32(_HI_MASK)
    return lo | hi


def _unpack_rows(u):
    lo = lax.bitcast_convert_type(u << 16, _F32)
    hi = lax.bitcast_convert_type(u & jnp.uint32(_HI_MASK), _F32)
    return lo, hi


def _rope_slab(y, cos, sin_signed):
    lane = lax.broadcasted_iota(jnp.int32, y.shape, 1)
    first_half = (lane & (SWA_HEAD_DIM - 1)) < (SWA_HEAD_DIM // 2)
    rot = jnp.where(first_half, pltpu.roll(y, LANES - SWA_HEAD_DIM // 2, axis=1),
                    pltpu.roll(y, SWA_HEAD_DIM // 2, axis=1))
    return y * cos + rot * sin_signed


def _inproj_kernel(x_ref, g_ref, w_ref, b_ref, cos_ref, sin_ref,
                   qkva_ref, qb_ref, kvb_ref, gate_ref):
    x = x_ref[...]
    var = jnp.mean(x * x, axis=-1, keepdims=True)
    h = (x * lax.rsqrt(var + RMS_EPS) * g_ref[...]).astype(_BF16)

    def proj(c0, c1):
        return jnp.dot(h, w_ref[:, c0:c1], preferred_element_type=_F32) + b_ref[:, c0:c1]

    scale = NA_HEAD_DIM ** -0.5 * LOG2E
    qkva_ref[:, C_QA:C_KA] = (proj(C_QA, C_KA) * scale).astype(_BF16)
    qkva_ref[:, C_KA:C_VA] = proj(C_KA, C_VA).astype(_BF16)
    qkva_ref[:, C_VA:C_QB] = proj(C_VA, C_QB).astype(_BF16)

    cos = cos_ref[...]
    sin = sin_ref[...]
    qb = proj(C_QB, C_KB)
    n_slabs = SWA_Q_WIDTH // LANES
    low = lax.broadcasted_iota(jnp.int32, (x.shape[0], LANES), 1) < SWA_HEAD_DIM
    for s in range(n_slabs):
        a = qb[:, (s // 2) * LANES:(s // 2 + 1) * LANES]
        b = qb[:, (s // 2 + n_slabs // 2) * LANES:(s // 2 + n_slabs // 2 + 1) * LANES]
        if s % 2 == 0:
            slab = jnp.where(low, a, pltpu.roll(b, SWA_HEAD_DIM, axis=1))
        else:
            slab = jnp.where(low, pltpu.roll(a, SWA_HEAD_DIM, axis=1), b)
        slab = _rope_slab(slab, cos, sin)
        qb_ref[:, s * LANES:(s + 1) * LANES] = (slab * (SWA_HEAD_DIM ** -0.5 * LOG2E)).astype(_BF16)
    kvb = proj(C_KB, C_GA)
    kvb_ref[:, 0:LANES] = _rope_slab(kvb[:, 0:LANES], cos, sin).astype(_BF16)
    kvb_ref[:, LANES:2 * LANES] = kvb[:, LANES:2 * LANES].astype(_BF16)

    for c0 in range(C_GA, D_IN, GATE_CHUNK):
        gate_ref[:, c0 - C_GA:c0 - C_GA + GATE_CHUNK] = jax.nn.sigmoid(proj(c0, c0 + GATE_CHUNK)).astype(_BF16)


def _inproj(x2, g1, w_in, b_in, cos_t, sin_t, seq):
    t = x2.shape[0]
    tm = TM_PROJ
    nseq = seq // tm
    row = lambda i: (i, 0)
    return pl.pallas_call(
        _inproj_kernel,
        grid=(t // tm,),
        in_specs=[
            pl.BlockSpec((tm, D_MODEL), row),
            _const_spec((1, D_MODEL)),
            _const_spec((D_MODEL, D_IN)),
            _const_spec((1, D_IN)),
            pl.BlockSpec((tm, LANES), lambda i: (i % nseq, 0)),
            pl.BlockSpec((tm, LANES), lambda i: (i % nseq, 0)),
        ],
        out_specs=[
            pl.BlockSpec((tm, 3 * NA_WIDTH), row),
            pl.BlockSpec((tm, SWA_Q_WIDTH), row),
            pl.BlockSpec((tm, 2 * SWA_KV_WIDTH), row),
            pl.BlockSpec((tm, 2 * D_MODEL), row),
        ],
        out_shape=[
            jax.ShapeDtypeStruct((t, 3 * NA_WIDTH), _BF16),
            jax.ShapeDtypeStruct((t, SWA_Q_WIDTH), _BF16),
            jax.ShapeDtypeStruct((t, 2 * SWA_KV_WIDTH), _BF16),
            jax.ShapeDtypeStruct((t, 2 * D_MODEL), _BF16),
        ],
        compiler_params=pltpu.CompilerParams(
            dimension_semantics=("parallel",), vmem_limit_bytes=VMEM_LIMIT),
        name="inproj",
    )(x2, g1, w_in, b_in, cos_t, sin_t)


_NT_DIMS = (((1,), (1,)), ((), ()))


def _mixer_kernel(sink_ref, x_ref, q_ref, kp_ref, kc_ref, kn_ref, vp_ref, vc_ref, vn_ref,
                  qb_ref, kvp_ref, kvc_ref, kvn_ref, gate_ref, tbl_ref, mask_ref,
                  wua_ref, wub_ref, wo_ref, g2_ref, wr_ref, br_ref,
                  x1_ref, h2_ref, key_ref, wts_ref, cnt_ref,
                  kcat, vcat, kvcat, oa_s, ob_s, s_s, e_s, inv_s, *, rows, n_swa_blocks):
    j = pl.program_id(1)
    rows_per_tile = TM_MIX // GRID_W
    halo = (NA_ROWS // 2) * GRID_W
    band = NA_ROWS * GRID_W

    kcat[0:halo] = kp_ref[...]
    kcat[halo:halo + TM_MIX] = kc_ref[...]
    kcat[halo + TM_MIX:] = kn_ref[...]
    vcat[0:halo] = vp_ref[...]
    vcat[halo:halo + TM_MIX] = vc_ref[...]
    vcat[halo + TM_MIX:] = vn_ref[...]

    lane_q = lax.broadcasted_iota(jnp.int32, (GRID_W, LANES), 1)
    low_q = lane_q < NA_HEAD_DIM
    lane_o = lax.broadcasted_iota(jnp.int32, (GRID_W, LANES), 1) < NA_HEAD_DIM

    n_pairs = NA_HEADS // 2
    chain = 2 * GRID_W

    def band_start(i):
        r = j * rows_per_tile + i
        rs = jnp.clip(r - NA_ROWS // 2, 0, rows - NA_ROWS)
        start = pl.multiple_of((rs - (j * rows_per_tile - NA_ROWS // 2)) * GRID_W, GRID_W)
        return start, r - rs

    for half in range(rows_per_tile // NA_BATCH_ROWS):
        def na_scores(b, carry):
            i = half * NA_BATCH_ROWS + b
            start, d = band_start(i)
            q0 = pl.multiple_of(i * GRID_W, GRID_W)
            for p in range(n_pairs):
                cols = slice(p * LANES, (p + 1) * LANES)
                qpair = q_ref[pl.ds(q0, GRID_W), cols]
                zero = jnp.zeros_like(qpair)
                qs = jnp.concatenate([jnp.where(low_q, qpair, zero), jnp.where(low_q, zero, qpair)], axis=0)
                kb = kcat[pl.ds(start, band), cols]
                s = lax.dot_general(qs, kb, _NT_DIMS, preferred_element_type=_F32) + tbl_ref[d, p]
                s_s[pl.ds(pl.multiple_of((b * n_pairs + p) * chain, chain), chain), :] = s
            return carry

        def na_softmax(b, carry):
            rws = pl.ds(pl.multiple_of(b * n_pairs * chain, n_pairs * chain), n_pairs * chain)
            s = s_s[rws, :]
            e = jnp.exp2(s - jnp.max(s, axis=-1, keepdims=True))
            e_s[rws, :] = e.astype(_BF16)
            inv_s[rws, :] = jnp.broadcast_to(1.0 / jnp.sum(e, axis=-1, keepdims=True), (n_pairs * chain, LANES))
            return carry

        def na_values(b, carry):
            i = half * NA_BATCH_ROWS + b
            start, _ = band_start(i)
            q0 = pl.multiple_of(i * GRID_W, GRID_W)
            for p in range(n_pairs):
                cols = slice(p * LANES, (p + 1) * LANES)
                rws = pl.ds(pl.multiple_of((b * n_pairs + p) * chain, chain), chain)
                vb = vcat[pl.ds(start, band), cols]
                o = jnp.dot(e_s[rws, :], vb, preferred_element_type=_F32) * inv_s[rws, :]
                oa_s[pl.ds(q0, GRID_W), cols] = jnp.where(lane_o, o[:GRID_W], o[GRID_W:]).astype(_BF16)
            return carry

        lax.fori_loop(0, NA_BATCH_ROWS, na_scores, 0)
        lax.fori_loop(0, NA_BATCH_ROWS, na_softmax, 0)
        lax.fori_loop(0, NA_BATCH_ROWS, na_values, 0)

    kvcat[0:SWA_BLOCK] = kvp_ref[...]
    kvcat[SWA_BLOCK:SWA_BLOCK + TM_MIX] = kvc_ref[...]
    kvcat[SWA_BLOCK + TM_MIX:] = kvn_ref[...]

    n_slabs = SWA_Q_WIDTH // LANES
    stack = n_slabs * SWA_BLOCK
    lane_s = lax.broadcasted_iota(jnp.int32, (SWA_BLOCK, LANES), 1) < SWA_HEAD_DIM
    rowblk = lax.broadcasted_iota(jnp.int32, (stack, 1), 0) // SWA_BLOCK
    wband = 3 * SWA_BLOCK
    sinks = []
    for g in range(SWA_KV_HEADS):
        sink = jnp.zeros((stack, 1), _F32)
        for s_ in range(n_slabs):
            sink = jnp.where(rowblk == s_, sink_ref[SWA_KV_HEADS * s_ + g], sink)
        sinks.append(sink)

    def swa_block(n, carry):
        nb = j * (TM_MIX // SWA_BLOCK) + n
        t0 = pl.multiple_of(n * SWA_BLOCK, SWA_BLOCK)
        variant = jnp.where(nb == 0, 0, jnp.where(nb == n_swa_blocks - 1, 2, 1))
        mask = mask_ref[variant]
        mask = jnp.concatenate([mask] * n_slabs, axis=0)
        kband = kvcat[pl.ds(t0, wband), 0:LANES]
        vband = kvcat[pl.ds(t0, wband), LANES:2 * LANES]
        outs = []
        for g in range(SWA_KV_HEADS):
            parts = []
            for s_ in range(n_slabs):
                slab = qb_ref[pl.ds(t0, SWA_BLOCK), s_ * LANES:(s_ + 1) * LANES]
                zero = jnp.zeros_like(slab)
                parts.append(jnp.where(lane_s, slab, zero) if g == 0 else jnp.where(lane_s, zero, slab))
            qg = jnp.concatenate(parts, axis=0)
            s = lax.dot_general(qg, kband, _NT_DIMS, preferred_element_type=_F32) + mask
            m = jnp.maximum(jnp.max(s, axis=-1, keepdims=True), sinks[g])
            e = jnp.exp2(s - m)
            den = jnp.sum(e, axis=-1, keepdims=True) + jnp.exp2(sinks[g] - m)
            outs.append(jnp.dot(e.astype(_BF16), vband, preferred_element_type=_F32) * (1.0 / den))
        for s_ in range(n_slabs):
            rs_ = slice(s_ * SWA_BLOCK, (s_ + 1) * SWA_BLOCK)
            ob_s[pl.ds(t0, SWA_BLOCK), s_ * LANES:(s_ + 1) * LANES] = jnp.where(
                lane_s, outs[0][rs_], outs[1][rs_]).astype(_BF16)
        return carry

    lax.fori_loop(0, TM_MIX // SWA_BLOCK, swa_block, 0)

    ua = jnp.dot(oa_s[...], wua_ref[...], preferred_element_type=_F32)
    ub = jnp.dot(ob_s[...], wub_ref[...], preferred_element_type=_F32)
    merged = (gate_ref[:, 0:D_MODEL].astype(_F32) * ua
              + gate_ref[:, D_MODEL:].astype(_F32) * ub).astype(_BF16)
    x1 = x_ref[...] + jnp.dot(merged, wo_ref[...], preferred_element_type=_F32)
    x1_ref[...] = x1

    var = jnp.mean(x1 * x1, axis=-1, keepdims=True)
    h2 = x1 * lax.rsqrt(var + RMS_EPS) * g2_ref[...]
    h2_hi = h2.astype(_BF16)
    h2_ref[...] = _pack_rows(h2_hi[:, :HALF], h2_hi[:, HALF:])
    h2_lo = (h2 - h2_hi.astype(_F32)).astype(_BF16)
    both = lax.dot_general(wr_ref[...], h2_hi, _NT_DIMS, preferred_element_type=_F32)
    cross = lax.dot_general(wr_ref[0:N_EXPERTS], h2_lo, _NT_DIMS, preferred_element_type=_F32)
    logits = both[0:N_EXPERTS] + both[N_EXPERTS:] + cross + br_ref[...]
    eidx = lax.broadcasted_iota(jnp.int32, logits.shape, 0)
    vals, idxs, hots = [], [], []
    for _ in range(TOP_K):
        m = jnp.max(logits, axis=0, keepdims=True)
        idx = jnp.min(jnp.where(logits == m, eidx, N_EXPERTS), axis=0, keepdims=True)
        hot = eidx == idx
        logits = jnp.where(hot, -jnp.inf, logits)
        vals.append(m)
        idxs.append(idx)
        hots.append(hot)
    es = [jnp.exp(v - vals[0]) for v in vals]
    inv = 1.0 / (es[0] + es[1] + es[2] + es[3])
    sel = jnp.zeros(logits.shape, _F32)
    for hot in hots:
        sel = sel + jnp.where(hot, 1.0, 0.0)
    tile_idx = pl.program_id(0) * pl.num_programs(1) + j
    tok = tile_idx * TM_MIX + lax.broadcasted_iota(jnp.int32, (1, TM_MIX), 1)
    keys = [(idx << KEY_SHIFT) | (tok * TOP_K + k) for k, idx in enumerate(idxs)]
    key_ref[...] = jnp.concatenate(
        [key[:, c * LANES:(c + 1) * LANES] for key in keys for c in range(TM_MIX // LANES)], axis=0)
    wts_ref[...] = jnp.concatenate([e * inv for e in es] + [jnp.zeros((8 - TOP_K, TM_MIX), _F32)], axis=0)
    cnt_ref[...] = jnp.broadcast_to(jnp.sum(sel, axis=1, keepdims=True), (N_EXPERTS, LANES))


def _swa_mask_table():
    qpos = np.arange(SWA_BLOCK)[:, None]
    koff = np.arange(3 * SWA_BLOCK)[None, :] - SWA_BLOCK
    rel_ok = np.abs(koff - qpos) <= SWA_WINDOW
    has_prev = np.array([False, True, True])[:, None, None]
    has_next = np.array([True, True, False])[:, None, None]
    ok = rel_ok[None] & ((koff >= 0)[None] | has_prev) & ((koff < SWA_BLOCK)[None] | has_next)
    return jnp.asarray(np.where(ok, 0.0, NEG_BIG), _F32)


def _mixer(sinks_perm, x2, qkva, qb, kvb, gates, tbl, wua, wub, wo, g2, wr_t, br, batch, seq):
    t = x2.shape[0]
    tm = TM_MIX
    nj = seq // tm
    rows = seq // GRID_W
    hb = tm // ((NA_ROWS // 2) * GRID_W)
    sb = tm // SWA_BLOCK
    n_halo = seq // ((NA_ROWS // 2) * GRID_W)
    n_swa = seq // SWA_BLOCK
    halo = (NA_ROWS // 2) * GRID_W
    swa_mask = _swa_mask_table()
    score_rows = NA_BATCH_ROWS * NA_HEADS * GRID_W
    assert n_swa >= 2

    tile = lambda b, j, *_: (b * nj + j, 0)

    def na_spec(col, which):
        if which == 0:
            return pl.BlockSpec((tm, NA_WIDTH), lambda b, j, *_: (b * nj + j, col))
        if which < 0:
            return pl.BlockSpec((halo, NA_WIDTH),
                                lambda b, j, *_: (b * n_halo + jnp.maximum(j * hb - 1, 0), col))
        return pl.BlockSpec((halo, NA_WIDTH),
                            lambda b, j, *_: (b * n_halo + jnp.minimum(j * hb + hb, n_halo - 1), col))

    kv_prev = pl.BlockSpec((SWA_BLOCK, 2 * SWA_KV_WIDTH),
                           lambda b, j, *_: (b * n_swa + jnp.maximum(j * sb - 1, 0), 0))
    kv_next = pl.BlockSpec((SWA_BLOCK, 2 * SWA_KV_WIDTH),
                           lambda b, j, *_: (b * n_swa + jnp.minimum(j * sb + sb, n_swa - 1), 0))

    cspec = _const_spec

    grid_spec = pltpu.PrefetchScalarGridSpec(
        num_scalar_prefetch=1,
        grid=(batch, nj),
        in_specs=[
            pl.BlockSpec((tm, D_MODEL), tile),
            na_spec(0, 0),
            na_spec(1, -1), na_spec(1, 0), na_spec(1, 1),
            na_spec(2, -1), na_spec(2, 0), na_spec(2, 1),
            pl.BlockSpec((tm, SWA_Q_WIDTH), tile),
            kv_prev, pl.BlockSpec((tm, 2 * SWA_KV_WIDTH), tile), kv_next,
            pl.BlockSpec((tm, 2 * D_MODEL), tile),
            cspec(tbl.shape), cspec(swa_mask.shape),
            cspec(wua.shape), cspec(wub.shape), cspec(wo.shape),
            cspec(g2.shape), cspec(wr_t.shape), cspec(br.shape),
        ],
        out_specs=[
            pl.BlockSpec((tm, D_MODEL), tile),
            pl.BlockSpec((tm, HALF), tile),
            pl.BlockSpec((TOP_K * tm // LANES, LANES), tile),
            pl.BlockSpec((8, tm), lambda b, j, *_: (0, b * nj + j)),
            pl.BlockSpec((N_EXPERTS, LANES), tile),
        ],
        scratch_shapes=[
            pltpu.VMEM((tm + 2 * halo, NA_WIDTH), _BF16),
            pltpu.VMEM((tm + 2 * halo, NA_WIDTH), _BF16),
            pltpu.VMEM((tm + 2 * SWA_BLOCK, 2 * SWA_KV_WIDTH), _BF16),
            pltpu.VMEM((tm, NA_WIDTH), _BF16),
            pltpu.VMEM((tm, SWA_Q_WIDTH), _BF16),
            pltpu.VMEM((score_rows, NA_ROWS * GRID_W), _F32),
            pltpu.VMEM((score_rows, NA_ROWS * GRID_W), _BF16),
            pltpu.VMEM((score_rows, LANES), _F32),
        ],
    )
    return pl.pallas_call(
        functools.partial(_mixer_kernel, rows=rows, n_swa_blocks=n_swa),
        grid_spec=grid_spec,
        out_shape=[
            jax.ShapeDtypeStruct((t, D_MODEL), _F32),
            jax.ShapeDtypeStruct((t, HALF), jnp.uint32),
            jax.ShapeDtypeStruct((TOP_K * t // LANES, LANES), jnp.int32),
            jax.ShapeDtypeStruct((8, t), _F32),
            jax.ShapeDtypeStruct((t // tm * N_EXPERTS, LANES), _F32),
        ],
        compiler_params=pltpu.CompilerParams(
            dimension_semantics=("parallel", "parallel"), vmem_limit_bytes=VMEM_LIMIT),
        name="mixer",
    )(sinks_perm, x2, qkva, qkva, qkva, qkva, qkva, qkva, qkva, qb, kvb, kvb, kvb, gates, tbl, swa_mask,
      wua, wub, wo, g2, wr_t, br)


def _expert_kernel(bexp_ref, nact_ref, gidx_hbm, sidx_hbm, h2_hbm, w1_ref, b1_ref, w2_ref, b2_ref,
                   y_hbm, gs0, gs1, ss0, ss1, xbuf0, xbuf1, ybuf0, ybuf1, w1b, w2b, isem, gsem, ssem,
                   *, n_blocks, n_real_rows):
    i = pl.program_id(0)
    nact = nact_ref[0]
    eb = EXPERT_BLOCK
    gs, ss, xbuf, ybuf = (gs0, gs1), (ss0, ss1), (xbuf0, xbuf1), (ybuf0, ybuf1)
    rows = pl.ds(0, eb)

    def gidx_copy(blk, slot):
        return pltpu.make_async_copy(gidx_hbm.at[blk], gs[slot], isem.at[0, slot])

    def sidx_copy(entry, slot):
        return pltpu.make_async_copy(sidx_hbm.at[entry], ss[slot], isem.at[1, slot])

    def gather_row(slot, r, priority=0):
        pltpu.make_async_copy(h2_hbm.at[pl.ds(gs[slot][r], 1), :], xbuf[slot].at[pl.ds(r, 1), :],
                              gsem.at[slot]).start(priority=priority)

    def wait_gather(slot):
        pltpu.make_async_copy(h2_hbm.at[rows, :], xbuf[slot].at[rows, :], gsem.at[slot]).wait()

    def scatter_row(slot, r, priority=0):
        pltpu.make_async_copy(ybuf[slot].at[pl.ds(r, 1), :], y_hbm.at[pl.ds(ss[slot][r], 1), :],
                              ssem.at[slot]).start(priority=priority)

    def wait_scatter(slot):
        pltpu.make_async_copy(ybuf[slot].at[rows, :], y_hbm.at[rows, :], ssem.at[slot]).wait()

    def rolled(fn, slot):
        def body(r, c):
            fn(slot, r)
            return c
        lax.fori_loop(0, eb, body, 0)

    @pl.when(i == 0)
    def _():
        gidx_copy(0, 0).start()
        gidx_copy(0, 0).wait()
        rolled(gather_row, 0)
        gidx_copy(jnp.minimum(1, n_blocks - 1), 1).start()
        sidx_copy(0, 1).start()
        ybuf1[...] = jnp.zeros(ybuf1.shape, jnp.uint32)
        init = pltpu.make_async_copy(ybuf1.at[rows, :], y_hbm.at[pl.ds(n_real_rows, eb), :], ssem.at[0])
        init.start()
        init.wait()

    def step(cur):
        nxt = 1 - cur
        gidx_copy(jnp.minimum(i + 2, n_blocks - 1), cur).start()
        sidx_copy(i + 1, cur).start()
        gidx_copy(0, nxt).wait()
        sidx_copy(0, nxt).wait()
        wait_gather(cur)

        x_lo, x_hi = (half.astype(_BF16) for half in _unpack_rows(xbuf[cur][rows, :]))

        def x_dot(c0):
            return (jnp.dot(x_lo, w1b[0:HALF, c0:c0 + FF_CHUNK], preferred_element_type=_F32)
                    + jnp.dot(x_hi, w1b[HALF:, c0:c0 + FF_CHUNK], preferred_element_type=_F32))

        n_chunks = D_FF // FF_CHUNK
        spare = pl.ds(pl.multiple_of(eb + jnp.minimum(i, 0) * SUBLANES, SUBLANES), SUBLANES)
        y = None
        for jc in range(n_chunks):
            lo, hi = (jc * eb) // (n_chunks - 1), ((jc + 1) * eb) // (n_chunks - 1)
            if jc < n_chunks - 1:
                for r in range(lo, hi):
                    gather_row(nxt, r, 1)
            if jc > 0:
                for r in range(lo - eb // (n_chunks - 1), hi - eb // (n_chunks - 1)):
                    scatter_row(nxt, r, 0)
            xbuf[nxt][spare, 0:FF_CHUNK] = jnp.zeros((SUBLANES, FF_CHUNK), jnp.uint32)
            ybuf[nxt][spare, 0:FF_CHUNK] = jnp.zeros((SUBLANES, FF_CHUNK), jnp.uint32)
            anchor = lax.bitcast_convert_type(
                (xbuf[nxt][spare, 0:FF_CHUNK] | ybuf[nxt][spare, 0:FF_CHUNK])[0:1], _F32)
            c0 = jc * FF_CHUNK
            g = x_dot(c0) + (b1_ref[:, c0:c0 + FF_CHUNK] + anchor)
            u = x_dot(D_FF + c0) + b1_ref[:, D_FF + c0:D_FF + c0 + FF_CHUNK]
            gate = jnp.minimum(g, SWIGLU_LIMIT)
            up = jnp.clip(u, -SWIGLU_LIMIT, SWIGLU_LIMIT)
            act = ((up + 1.0) * (gate * jax.nn.sigmoid(gate * SWIGLU_ALPHA))).astype(_BF16)
            part = jnp.dot(act, w2b[c0:c0 + FF_CHUNK, :], preferred_element_type=_F32)
            y = part + b2_ref[...] if y is None else y + part

        @pl.when(i >= 1)
        def _():
            wait_scatter(cur)

        ybuf[cur][rows, :] = _pack_rows(y[:, :HALF].astype(_BF16), y[:, HALF:].astype(_BF16))

        @pl.when(i == nact - 1)
        def _():
            sidx_copy(0, cur).wait()
            rolled(scatter_row, cur)
            wait_scatter(nxt)
            wait_scatter(cur)
            wait_gather(nxt)
            gidx_copy(0, cur).wait()

    new_expert = jnp.logical_or(i == 0, bexp_ref[i] != bexp_ref[jnp.maximum(i - 1, 0)])

    @pl.when(jnp.logical_and(i < nact, new_expert))
    def _():
        w1b[...] = w1_ref[...].astype(_BF16)
        w2b[...] = w2_ref[...].astype(_BF16)

    for parity in range(2):
        @pl.when(jnp.logical_and(i < nact, i % 2 == parity))
        def _():
            step(parity)


def _experts(blk_exp, n_active, gidx, sidx, h2, w1, b1, w2, b2, n_tokens):
    n_blk = gidx.shape[0]
    eb = EXPERT_BLOCK
    n_real = TOP_K * n_tokens
    wmap = lambda i, be, na: (be[i], 0, 0)
    grid_spec = pltpu.PrefetchScalarGridSpec(
        num_scalar_prefetch=2,
        grid=(n_blk,),
        in_specs=[
            pl.BlockSpec(memory_space=pl.ANY),
            pl.BlockSpec(memory_space=pl.ANY),
            pl.BlockSpec(memory_space=pl.ANY),
            pl.BlockSpec((None, D_MODEL, 2 * D_FF), wmap),
            pl.BlockSpec((None, 1, 2 * D_FF), wmap),
            pl.BlockSpec((None, D_FF, D_MODEL), wmap),
            pl.BlockSpec((None, 1, D_MODEL), wmap),
        ],
        out_specs=pl.BlockSpec(memory_space=pl.ANY),
        scratch_shapes=[
            pltpu.SMEM((eb,), jnp.int32), pltpu.SMEM((eb,), jnp.int32),
            pltpu.SMEM((eb,), jnp.int32), pltpu.SMEM((eb,), jnp.int32),
            pltpu.VMEM((eb + SUBLANES, HALF), jnp.uint32),
            pltpu.VMEM((eb + SUBLANES, HALF), jnp.uint32),
            pltpu.VMEM((eb + SUBLANES, HALF), jnp.uint32),
            pltpu.VMEM((eb + SUBLANES, HALF), jnp.uint32),
            pltpu.VMEM((D_MODEL, 2 * D_FF), _BF16),
            pltpu.VMEM((D_FF, D_MODEL), _BF16),
            pltpu.SemaphoreType.DMA((2, 2)),
            pltpu.SemaphoreType.DMA((2,)),
            pltpu.SemaphoreType.DMA((2,)),
        ],
    )
    return pl.pallas_call(
        functools.partial(_expert_kernel, n_blocks=n_blk, n_real_rows=n_real),
        grid_spec=grid_spec,
        out_shape=jax.ShapeDtypeStruct((n_real + 2 * eb, HALF), jnp.uint32),
        compiler_params=pltpu.CompilerParams(
            dimension_semantics=("arbitrary",), vmem_limit_bytes=VMEM_LIMIT),
        name="experts",
    )(blk_exp, n_active, gidx, sidx, h2, w1, b1, w2, b2)


def _combine_kernel(x1_ref, y0_ref, y1_ref, y2_ref, y3_ref, wts_ref, g_ref, o_ref):
    wt = wts_ref[...].T
    y = x1_ref[...]
    for k, y_ref in enumerate((y0_ref, y1_ref, y2_ref, y3_ref)):
        y = y + wt[:, k:k + 1] * jnp.concatenate(_unpack_rows(y_ref[...]), axis=1)
    var = jnp.mean(y * y, axis=-1, keepdims=True)
    o_ref[...] = y * lax.rsqrt(var + RMS_EPS) * g_ref[...]


def _combine(x1, y, wts, gf):
    t = x1.shape[0]
    tm = TM_OUT
    nt = t // tm
    yspec = lambda k: pl.BlockSpec((tm, HALF), lambda i: (k * nt + i, 0))
    return pl.pallas_call(
        _combine_kernel,
        grid=(nt,),
        in_specs=[
            pl.BlockSpec((tm, D_MODEL), lambda i: (i, 0)),
            yspec(0), yspec(1), yspec(2), yspec(3),
            pl.BlockSpec((8, tm), lambda i: (0, i)),
            _const_spec((1, D_MODEL)),
        ],
        out_specs=pl.BlockSpec((tm, D_MODEL), lambda i: (i, 0)),
        out_shape=jax.ShapeDtypeStruct((t, D_MODEL), _F32),
        compiler_params=pltpu.CompilerParams(
            dimension_semantics=("parallel",), vmem_limit_bytes=VMEM_LIMIT),
        name="combine",
    )(x1, y, y, y, y, wts, gf)


def _split_bf16(w):
    hi = w.astype(_BF16)
    lo = (w - hi.astype(_F32)).astype(_BF16)
    return jnp.concatenate([hi, lo], axis=0)


def _rope_tables(seq):
    half = SWA_HEAD_DIM // 2
    inv_freq = np.float32(ROPE_THETA) ** (-np.arange(half, dtype=np.float32) / np.float32(half))
    ang = np.arange(seq, dtype=np.float32)[:, None] * inv_freq[None, :]
    cos, sin = np.cos(ang), np.sin(ang)
    cos_t = np.tile(np.concatenate([cos, cos], axis=1), (1, LANES // SWA_HEAD_DIM))
    sin_t = np.tile(np.concatenate([-sin, sin], axis=1), (1, LANES // SWA_HEAD_DIM))
    return jnp.asarray(cos_t, _F32), jnp.asarray(sin_t, _F32)


def _na_bias_table(rpb):
    col = np.arange(GRID_W)
    cstart = np.clip(col - NA_COLS // 2, 0, GRID_W - NA_COLS)
    kc = np.arange(GRID_W)
    valid = (kc[None, :] >= cstart[:, None]) & (kc[None, :] < cstart[:, None] + NA_COLS)
    off = kc[None, :] - col[:, None] + NA_COLS - 1
    pick = (off[:, :, None] == np.arange(2 * NA_COLS - 1)[None, None, :]) & valid[:, :, None]
    ext = jnp.einsum("hvo,cko->hvck", rpb.astype(_F32), jnp.asarray(pick, _F32),
                     precision=lax.Precision.HIGHEST)
    ext = jnp.where(valid[None, None], ext, NEG_BIG)
    tbl = jnp.stack([ext[:, NA_ROWS - 1 - d_:2 * NA_ROWS - 1 - d_] for d_ in range(NA_ROWS)], axis=1)
    tbl = tbl.transpose(1, 0, 3, 2, 4).reshape(NA_ROWS, NA_HEADS // 2, 2 * GRID_W, NA_ROWS * GRID_W)
    return (tbl * LOG2E).astype(_F32)


def kernel(x, norm1_g, w_in, b_in, na_rpb, swa_sinks, w_up_a, w_up_b, w_out, norm2_g, w_router,
           b_router, w1, b1, w2, b2, final_g):
    batch, seq, d = x.shape
    depth = w_in.shape[0]
    t = batch * seq
    assert depth == 1, "the final norm is fused into the single layer's combine step"
    assert d == D_MODEL and seq % TM_MIX == 0 and seq % TM_PROJ == 0 and t % TM_OUT == 0
    assert seq // GRID_W >= 2 * NA_ROWS

    group = SWA_Q_HEADS // SWA_KV_HEADS
    head_order = np.arange(SWA_Q_HEADS).reshape(SWA_KV_HEADS, group).T.reshape(-1)

    def reorder_heads(a, axis, start):
        take = lambda lo, hi: lax.slice_in_dim(a, lo, hi, axis=axis)
        heads = [take(start + h * SWA_HEAD_DIM, start + (h + 1) * SWA_HEAD_DIM) for h in head_order]
        return jnp.concatenate([take(0, start)] + heads + [take(start + SWA_Q_WIDTH, a.shape[axis])], axis=axis)

    cos_t, sin_t = _rope_tables(seq)

    n_assign = t * TOP_K
    assert n_assign <= PAD_FLAG
    n_rows = n_assign + N_EXPERTS * EXPERT_BLOCK
    n_blk = n_rows // EXPERT_BLOCK
    n_tiles = t // TM_MIX

    x2 = x.reshape(t, d)
    for l in range(depth):
        w_in_l = w_in[l].astype(_BF16)
        b_in_l = b_in[l].reshape(1, D_IN)
        qkva, qb, kvb, gates = _inproj(x2, norm1_g[l].reshape(1, d), w_in_l, b_in_l, cos_t, sin_t, seq)

        x1, h2, keys, wts, cnt = _mixer(
            swa_sinks[l][head_order].astype(_F32) * LOG2E, x2, qkva, qb, kvb, gates, _na_bias_table(na_rpb[l]),
            w_up_a[l].astype(_BF16), reorder_heads(w_up_b[l], 0, 0).astype(_BF16), w_out[l].astype(_BF16),
            norm2_g[l].reshape(1, d), _split_bf16(w_router[l].T), b_router[l].reshape(N_EXPERTS, 1),
            batch, seq)

        cnt = cnt.reshape(n_tiles, N_EXPERTS, LANES)[:, :, 0].astype(jnp.int32)
        counts = jnp.sum(cnt, axis=0)
        padded = (counts + EXPERT_BLOCK - 1) // EXPERT_BLOCK * EXPERT_BLOCK
        pend = jnp.cumsum(padded)
        pad_i = jnp.arange(EXPERT_BLOCK, dtype=jnp.int32)[None, :]
        pad_keys = jnp.where(pad_i < (padded - counts)[:, None],
                             (jnp.arange(N_EXPERTS, dtype=jnp.int32)[:, None] << KEY_SHIFT) | PAD_FLAG | pad_i,
                             jnp.iinfo(jnp.int32).max)
        sorted_keys = jnp.sort(jnp.concatenate([keys.reshape(-1), pad_keys.reshape(-1)]))
        row_a = jnp.where((sorted_keys & PAD_FLAG) == 0, sorted_keys & (PAD_FLAG - 1), -1)
        row_a = row_a.reshape(n_blk, EXPERT_BLOCK)
        blk_start = jnp.arange(n_blk, dtype=jnp.int32) * EXPERT_BLOCK
        blk_exp = jnp.minimum(jnp.sum((pend[None, :] <= blk_start[:, None]).astype(jnp.int32), axis=1),
                              N_EXPERTS - 1)
        n_active = (pend[-1:] // EXPERT_BLOCK).astype(jnp.int32)
        r_in_blk = jnp.arange(EXPERT_BLOCK, dtype=jnp.int32)[None, :]
        parity = (jnp.arange(-1, n_blk, dtype=jnp.int32) % 2)[:, None]
        trash = n_assign + parity * EXPERT_BLOCK + r_in_blk
        gidx = jnp.where(row_a >= 0, row_a >> 2, 0)
        sidx = jnp.where(row_a >= 0, (row_a & (TOP_K - 1)) * t + (row_a >> 2), trash[1:])
        sidx = jnp.concatenate([trash[:1], sidx], axis=0)

        y = _experts(blk_exp, n_active, gidx, sidx, h2,
                     w1[l], b1[l].reshape(N_EXPERTS, 1, 2 * D_FF),
                     w2[l], b2[l].reshape(N_EXPERTS, 1, D_MODEL), t)
        x2 = _combine(x1, y, wts, final_g.reshape(1, d))
    return x2.reshape(batch, seq, d)
```

```python
import functools

import jax
import jax.numpy as jnp
import numpy as np
from jax import lax
from jax.experimental import pallas as pl
from jax.experimental.pallas import tpu as pltpu

D_MODEL = 1024
GRID_W = 64
NA_HEADS = 8
NA_HEAD_DIM = 64
NA_ROWS = 8
NA_COLS = 16
SWA_Q_HEADS = 8
SWA_KV_HEADS = 2
SWA_HEAD_DIM = 64
SWA_WINDOW = 128
SWA_BLOCK = 128
ROPE_THETA = 10000.0
N_EXPERTS = 32
TOP_K = 4
D_FF = 1024
SWIGLU_LIMIT = 7.0
SWIGLU_ALPHA = 1.702
EXPERT_BLOCK = 512
RMS_EPS = 1e-5

NA_WIDTH = NA_HEADS * NA_HEAD_DIM
SWA_Q_WIDTH = SWA_Q_HEADS * SWA_HEAD_DIM
SWA_KV_WIDTH = SWA_KV_HEADS * SWA_HEAD_DIM
LANES = 128
SUBLANES = 8
NEG_BIG = -1e30

C_QA, C_KA, C_VA = 0, NA_WIDTH, 2 * NA_WIDTH
C_QB = 3 * NA_WIDTH
C_KB = C_QB + SWA_Q_WIDTH
C_VB = C_KB + SWA_KV_WIDTH
C_GA = C_VB + SWA_KV_WIDTH
C_GB = C_GA + D_MODEL
D_IN = C_GB + D_MODEL

TM_PROJ = 1024
TM_MIX = 512
TM_OUT = 1024
GATE_CHUNK = 512
FF_CHUNK = 256
NA_BATCH_ROWS = 4
LOG2E = 1.4426950408889634
KEY_SHIFT = 18
PAD_FLAG = 1 << (KEY_SHIFT - 1)
VMEM_LIMIT = 56 * 1024 * 1024

_BF16 = jnp.bfloat16
_F32 = jnp.float32


def _const_spec(shape):
    nd = len(shape)
    return pl.BlockSpec(shape, lambda *_: (0,) * nd, pipeline_mode=pl.Buffered(1))


HALF = D_MODEL // 2
_HI_MASK = 0xFFFF0000


def _pack_rows(lo_bf16, hi_bf16):
    lo = lax.bitcast_convert_type(lo_bf16.astype(_F32), jnp.uint32) >> 16
    hi = lax.bitcast_convert_type(hi_bf16.astype(_F32), jnp.uint32) & jnp.uint32(_HI_MASK)
    return lo | hi


def _unpack_rows(u):
    lo = lax.bitcast_convert_type(u << 16, _F32)
    hi = lax.bitcast_convert_type(u & jnp.uint32(_HI_MASK), _F32)
    return lo, hi


def _rope_slab(y, cos, sin_signed):
    lane = lax.broadcasted_iota(jnp.int32, y.shape, 1)
    first_half = (lane & (SWA_HEAD_DIM - 1)) < (SWA_HEAD_DIM // 2)
    rot = jnp.where(first_half, pltpu.roll(y, LANES - SWA_HEAD_DIM // 2, axis=1),
                    pltpu.roll(y, SWA_HEAD_DIM // 2, axis=1))
    return y * cos + rot * sin_signed


def _inproj_kernel(x_ref, g_ref, w_ref, b_ref, cos_ref, sin_ref,
                   qkva_ref, qb_ref, kvb_ref, gate_ref):
    x = x_ref[...]
    var = jnp.mean(x * x, axis=-1, keepdims=True)
    h = (x * lax.rsqrt(var + RMS_EPS) * g_ref[...]).astype(_BF16)

    def proj(c0, c1):
        return jnp.dot(h, w_ref[:, c0:c1], preferred_element_type=_F32) + b_ref[:, c0:c1]

    scale = NA_HEAD_DIM ** -0.5 * LOG2E
    qkva_ref[:, C_QA:C_KA] = (proj(C_QA, C_KA) * scale).astype(_BF16)
    qkva_ref[:, C_KA:C_VA] = proj(C_KA, C_VA).astype(_BF16)
    qkva_ref[:, C_VA:C_QB] = proj(C_VA, C_QB).astype(_BF16)

    cos = cos_ref[...]
    sin = sin_ref[...]
    qb = proj(C_QB, C_KB)
    n_slabs = SWA_Q_WIDTH // LANES
    low = lax.broadcasted_iota(jnp.int32, (x.shape[0], LANES), 1) < SWA_HEAD_DIM
    for s in range(n_slabs):
        a = qb[:, (s // 2) * LANES:(s // 2 + 1) * LANES]
        b = qb[:, (s // 2 + n_slabs // 2) * LANES:(s // 2 + n_slabs // 2 + 1) * LANES]
        if s % 2 == 0:
            slab = jnp.where(low, a, pltpu.roll(b, SWA_HEAD_DIM, axis=1))
        else:
            slab = jnp.where(low, pltpu.roll(a, SWA_HEAD_DIM, axis=1), b)
        slab = _rope_slab(slab, cos, sin)
        qb_ref[:, s * LANES:(s + 1) * LANES] = (slab * (SWA_HEAD_DIM ** -0.5 * LOG2E)).astype(_BF16)
    kvb = proj(C_KB, C_GA)
    kvb_ref[:, 0:LANES] = _rope_slab(kvb[:, 0:LANES], cos, sin).astype(_BF16)
    kvb_ref[:, LANES:2 * LANES] = kvb[:, LANES:2 * LANES].astype(_BF16)

    for c0 in range(C_GA, D_IN, GATE_CHUNK):
        gate_ref[:, c0 - C_GA:c0 - C_GA + GATE_CHUNK] = jax.nn.sigmoid(proj(c0, c0 + GATE_CHUNK)).astype(_BF16)


def _inproj(x2, g1, w_in, b_in, cos_t, sin_t, seq):
    t = x2.shape[0]
    tm = TM_PROJ
    nseq = seq // tm
    row = lambda i: (i, 0)
    return pl.pallas_call(
        _inproj_kernel,
        grid=(t // tm,),
        in_specs=[
            pl.BlockSpec((tm, D_MODEL), row),
            _const_spec((1, D_MODEL)),
            _const_spec((D_MODEL, D_IN)),
            _const_spec((1, D_IN)),
            pl.BlockSpec((tm, LANES), lambda i: (i % nseq, 0)),
            pl.BlockSpec((tm, LANES), lambda i: (i % nseq, 0)),
        ],
        out_specs=[
            pl.BlockSpec((tm, 3 * NA_WIDTH), row),
            pl.BlockSpec((tm, SWA_Q_WIDTH), row),
            pl.BlockSpec((tm, 2 * SWA_KV_WIDTH), row),
            pl.BlockSpec((tm, 2 * D_MODEL), row),
        ],
        out_shape=[
            jax.ShapeDtypeStruct((t, 3 * NA_WIDTH), _BF16),
            jax.ShapeDtypeStruct((t, SWA_Q_WIDTH), _BF16),
            jax.ShapeDtypeStruct((t, 2 * SWA_KV_WIDTH), _BF16),
            jax.ShapeDtypeStruct((t, 2 * D_MODEL), _BF16),
        ],
        compiler_params=pltpu.CompilerParams(
            dimension_semantics=("parallel",), vmem_limit_bytes=VMEM_LIMIT),
        name="inproj",
    )(x2, g1, w_in, b_in, cos_t, sin_t)


_NT_DIMS = (((1,), (1,)), ((), ()))


def _mixer_kernel(sink_ref, x_ref, q_ref, kp_ref, kc_ref, kn_ref, vp_ref, vc_ref, vn_ref,
                  qb_ref, kvp_ref, kvc_ref, kvn_ref, gate_ref, tbl_ref, mask_ref,
                  wua_ref, wub_ref, wo_ref, g2_ref, wr_ref, br_ref,
                  x1_ref, h2_ref, key_ref, wts_ref, cnt_ref,
                  kcat, vcat, kvcat, oa_s, ob_s, s_s, e_s, inv_s, *, rows, n_swa_blocks):
    j = pl.program_id(1)
    rows_per_tile = TM_MIX // GRID_W
    halo = (NA_ROWS // 2) * GRID_W
    band = NA_ROWS * GRID_W

    kcat[0:halo] = kp_ref[...]
    kcat[halo:halo + TM_MIX] = kc_ref[...]
    kcat[halo + TM_MIX:] = kn_ref[...]
    vcat[0:halo] = vp_ref[...]
    vcat[halo:halo + TM_MIX] = vc_ref[...]
    vcat[halo + TM_MIX:] = vn_ref[...]

    lane_q = lax.broadcasted_iota(jnp.int32, (GRID_W, LANES), 1)
    low_q = lane_q < NA_HEAD_DIM
    lane_o = lax.broadcasted_iota(jnp.int32, (GRID_W, LANES), 1) < NA_HEAD_DIM

    n_pairs = NA_HEADS // 2
    chain = 2 * GRID_W

    def band_start(i):
        r = j * rows_per_tile + i
        rs = jnp.clip(r - NA_ROWS // 2, 0, rows - NA_ROWS)
        start = pl.multiple_of((rs - (j * rows_per_tile - NA_ROWS // 2)) * GRID_W, GRID_W)
        return start, r - rs

    for half in range(rows_per_tile // NA_BATCH_ROWS):
        def na_scores(b, carry):
            i = half * NA_BATCH_ROWS + b
            start, d = band_start(i)
            q0 = pl.multiple_of(i * GRID_W, GRID_W)
            for p in range(n_pairs):
                cols = slice(p * LANES, (p + 1) * LANES)
                qpair = q_ref[pl.ds(q0, GRID_W), cols]
                zero = jnp.zeros_like(qpair)
                qs = jnp.concatenate([jnp.where(low_q, qpair, zero), jnp.where(low_q, zero, qpair)], axis=0)
                kb = kcat[pl.ds(start, band), cols]
                s = lax.dot_general(qs, kb, _NT_DIMS, preferred_element_type=_F32) + tbl_ref[d, p]
                s_s[pl.ds(pl.multiple_of((b * n_pairs + p) * chain, chain), chain), :] = s
            return carry

        def na_softmax(b, carry):
            rws = pl.ds(pl.multiple_of(b * n_pairs * chain, n_pairs * chain), n_pairs * chain)
            s = s_s[rws, :]
            e = jnp.exp2(s - jnp.max(s, axis=-1, keepdims=True))
            e_s[rws, :] = e.astype(_BF16)
            inv_s[rws, :] = jnp.broadcast_to(1.0 / jnp.sum(e, axis=-1, keepdims=True), (n_pairs * chain, LANES))
            return carry

        def na_values(b, carry):
            i = half * NA_BATCH_ROWS + b
            start, _ = band_start(i)
            q0 = pl.multiple_of(i * GRID_W, GRID_W)
            for p in range(n_pairs):
                cols = slice(p * LANES, (p + 1) * LANES)
                rws = pl.ds(pl.multiple_of((b * n_pairs + p) * chain, chain), chain)
                vb = vcat[pl.ds(start, band), cols]
                o = jnp.dot(e_s[rws, :], vb, preferred_element_type=_F32) * inv_s[rws, :]
                oa_s[pl.ds(q0, GRID_W), cols] = jnp.where(lane_o, o[:GRID_W], o[GRID_W:]).astype(_BF16)
            return carry

        lax.fori_loop(0, NA_BATCH_ROWS, na_scores, 0)
        lax.fori_loop(0, NA_BATCH_ROWS, na_softmax, 0)
        lax.fori_loop(0, NA_BATCH_ROWS, na_values, 0)

    kvcat[0:SWA_BLOCK] = kvp_ref[...]
    kvcat[SWA_BLOCK:SWA_BLOCK + TM_MIX] = kvc_ref[...]
    kvcat[SWA_BLOCK + TM_MIX:] = kvn_ref[...]

    n_slabs = SWA_Q_WIDTH // LANES
    stack = n_slabs * SWA_BLOCK
    lane_s = lax.broadcasted_iota(jnp.int32, (SWA_BLOCK, LANES), 1) < SWA_HEAD_DIM
    rowblk = lax.broadcasted_iota(jnp.int32, (stack, 1), 0) // SWA_BLOCK
    wband = 3 * SWA_BLOCK
    sinks = []
    for g in range(SWA_KV_HEADS):
        sink = jnp.zeros((stack, 1), _F32)
        for s_ in range(n_slabs):
            sink = jnp.where(rowblk == s_, sink_ref[SWA_KV_HEADS * s_ + g], sink)
        sinks.append(sink)

    def swa_block(n, carry):
        nb = j * (TM_MIX // SWA_BLOCK) + n
        t0 = pl.multiple_of(n * SWA_BLOCK, SWA_BLOCK)
        variant = jnp.where(nb == 0, 0, jnp.where(nb == n_swa_blocks - 1, 2, 1))
        mask = mask_ref[variant]
        mask = jnp.concatenate([mask] * n_slabs, axis=0)
        kband = kvcat[pl.ds(t0, wband), 0:LANES]
        vband = kvcat[pl.ds(t0, wband), LANES:2 * LANES]
        outs = []
        for g in range(SWA_KV_HEADS):
            parts = []
            for s_ in range(n_slabs):
                slab = qb_ref[pl.ds(t0, SWA_BLOCK), s_ * LANES:(s_ + 1) * LANES]
                zero = jnp.zeros_like(slab)
                parts.append(jnp.where(lane_s, slab, zero) if g == 0 else jnp.where(lane_s, zero, slab))
            qg = jnp.concatenate(parts, axis=0)
            s = lax.dot_general(qg, kband, _NT_DIMS, preferred_element_type=_F32) + mask
            m = jnp.maximum(jnp.max(s, axis=-1, keepdims=True), sinks[g])
            e = jnp.exp2(s - m)
            den = jnp.sum(e, axis=-1, keepdims=True) + jnp.exp2(sinks[g] - m)
            outs.append(jnp.dot(e.astype(_BF16), vband, preferred_element_type=_F32) * (1.0 / den))
        for s_ in range(n_slabs):
            rs_ = slice(s_ * SWA_BLOCK, (s_ + 1) * SWA_BLOCK)
            ob_s[pl.ds(t0, SWA_BLOCK), s_ * LANES:(s_ + 1) * LANES] = jnp.where(
                lane_s, outs[0][rs_], outs[1][rs_]).astype(_BF16)
        return carry

    lax.fori_loop(0, TM_MIX // SWA_BLOCK, swa_block, 0)

    ua = jnp.dot(oa_s[...], wua_ref[...], preferred_element_type=_F32)
    ub = jnp.dot(ob_s[...], wub_ref[...], preferred_element_type=_F32)
    merged = (gate_ref[:, 0:D_MODEL].astype(_F32) * ua
              + gate_ref[:, D_MODEL:].astype(_F32) * ub).astype(_BF16)
    x1 = x_ref[...] + jnp.dot(merged, wo_ref[...], preferred_element_type=_F32)
    x1_ref[...] = x1

    var = jnp.mean(x1 * x1, axis=-1, keepdims=True)
    h2 = x1 * lax.rsqrt(var + RMS_EPS) * g2_ref[...]
    h2_hi = h2.astype(_BF16)
    h2_ref[...] = _pack_rows(h2_hi[:, :HALF], h2_hi[:, HALF:])
    h2_lo = (h2 - h2_hi.astype(_F32)).astype(_BF16)
    both = lax.dot_general(wr_ref[...], h2_hi, _NT_DIMS, preferred_element_type=_F32)
    cross = lax.dot_general(wr_ref[0:N_EXPERTS], h2_lo, _NT_DIMS, preferred_element_type=_F32)
    logits = both[0:N_EXPERTS] + both[N_EXPERTS:] + cross + br_ref[...]
    eidx = lax.broadcasted_iota(jnp.int32, logits.shape, 0)
    vals, idxs, hots = [], [], []
    for _ in range(TOP_K):
        m = jnp.max(logits, axis=0, keepdims=True)
        idx = jnp.min(jnp.where(logits == m, eidx, N_EXPERTS), axis=0, keepdims=True)
        hot = eidx == idx
        logits = jnp.where(hot, -jnp.inf, logits)
        vals.append(m)
        idxs.append(idx)
        hots.append(hot)
    es = [jnp.exp(v - vals[0]) for v in vals]
    inv = 1.0 / (es[0] + es[1] + es[2] + es[3])
    sel = jnp.zeros(logits.shape, _F32)
    for hot in hots:
        sel = sel + jnp.where(hot, 1.0, 0.0)
    tile_idx = pl.program_id(0) * pl.num_programs(1) + j
    tok = tile_idx * TM_MIX + lax.broadcasted_iota(jnp.int32, (1, TM_MIX), 1)
    keys = [(idx << KEY_SHIFT) | (tok * TOP_K + k) for k, idx in enumerate(idxs)]
    key_ref[...] = jnp.concatenate(
        [key[:, c * LANES:(c + 1) * LANES] for key in keys for c in range(TM_MIX // LANES)], axis=0)
    wts_ref[...] = jnp.concatenate([e * inv for e in es] + [jnp.zeros((8 - TOP_K, TM_MIX), _F32)], axis=0)
    cnt_ref[...] = jnp.broadcast_to(jnp.sum(sel, axis=1, keepdims=True), (N_EXPERTS, LANES))


def _swa_mask_table():
    qpos = np.arange(SWA_BLOCK)[:, None]
    koff = np.arange(3 * SWA_BLOCK)[None, :] - SWA_BLOCK
    rel_ok = np.abs(koff - qpos) <= SWA_WINDOW
    has_prev = np.array([False, True, True])[:, None, None]
    has_next = np.array([True, True, False])[:, None, None]
    ok = rel_ok[None] & ((koff >= 0)[None] | has_prev) & ((koff < SWA_BLOCK)[None] | has_next)
    return jnp.asarray(np.where(ok, 0.0, NEG_BIG), _F32)


def _mixer(sinks_perm, x2, qkva, qb, kvb, gates, tbl, wua, wub, wo, g2, wr_t, br, batch, seq):
    t = x2.shape[0]
    tm = TM_MIX
    nj = seq // tm
    rows = seq // GRID_W
    hb = tm // ((NA_ROWS // 2) * GRID_W)
    sb = tm // SWA_BLOCK
    n_halo = seq // ((NA_ROWS // 2) * GRID_W)
    n_swa = seq // SWA_BLOCK
    halo = (NA_ROWS // 2) * GRID_W
    swa_mask = _swa_mask_table()
    score_rows = NA_BATCH_ROWS * NA_HEADS * GRID_W
    assert n_swa >= 2

    tile = lambda b, j, *_: (b * nj + j, 0)

    def na_spec(col, which):
        if which == 0:
            return pl.BlockSpec((tm, NA_WIDTH), lambda b, j, *_: (b * nj + j, col))
        if which < 0:
            return pl.BlockSpec((halo, NA_WIDTH),
                                lambda b, j, *_: (b * n_halo + jnp.maximum(j * hb - 1, 0), col))
        return pl.BlockSpec((halo, NA_WIDTH),
                            lambda b, j, *_: (b * n_halo + jnp.minimum(j * hb + hb, n_halo - 1), col))

    kv_prev = pl.BlockSpec((SWA_BLOCK, 2 * SWA_KV_WIDTH),
                           lambda b, j, *_: (b * n_swa + jnp.maximum(j * sb - 1, 0), 0))
    kv_next = pl.BlockSpec((SWA_BLOCK, 2 * SWA_KV_WIDTH),
                           lambda b, j, *_: (b * n_swa + jnp.minimum(j * sb + sb, n_swa - 1), 0))

    cspec = _const_spec

    grid_spec = pltpu.PrefetchScalarGridSpec(
        num_scalar_prefetch=1,
        grid=(batch, nj),
        in_specs=[
            pl.BlockSpec((tm, D_MODEL), tile),
            na_spec(0, 0),
            na_spec(1, -1), na_spec(1, 0), na_spec(1, 1),
            na_spec(2, -1), na_spec(2, 0), na_spec(2, 1),
            pl.BlockSpec((tm, SWA_Q_WIDTH), tile),
            kv_prev, pl.BlockSpec((tm, 2 * SWA_KV_WIDTH), tile), kv_next,
            pl.BlockSpec((tm, 2 * D_MODEL), tile),
            cspec(tbl.shape), cspec(swa_mask.shape),
            cspec(wua.shape), cspec(wub.shape), cspec(wo.shape),
            cspec(g2.shape), cspec(wr_t.shape), cspec(br.shape),
        ],
        out_specs=[
            pl.BlockSpec((tm, D_MODEL), tile),
            pl.BlockSpec((tm, HALF), tile),
            pl.BlockSpec((TOP_K * tm // LANES, LANES), tile),
            pl.BlockSpec((8, tm), lambda b, j, *_: (0, b * nj + j)),
            pl.BlockSpec((N_EXPERTS, LANES), tile),
        ],
        scratch_shapes=[
            pltpu.VMEM((tm + 2 * halo, NA_WIDTH), _BF16),
            pltpu.VMEM((tm + 2 * halo, NA_WIDTH), _BF16),
            pltpu.VMEM((tm + 2 * SWA_BLOCK, 2 * SWA_KV_WIDTH), _BF16),
            pltpu.VMEM((tm, NA_WIDTH), _BF16),
            pltpu.VMEM((tm, SWA_Q_WIDTH), _BF16),
            pltpu.VMEM((score_rows, NA_ROWS * GRID_W), _F32),
            pltpu.VMEM((score_rows, NA_ROWS * GRID_W), _BF16),
            pltpu.VMEM((score_rows, LANES), _F32),
        ],
    )
    return pl.pallas_call(
        functools.partial(_mixer_kernel, rows=rows, n_swa_blocks=n_swa),
        grid_spec=grid_spec,
        out_shape=[
            jax.ShapeDtypeStruct((t, D_MODEL), _F32),
            jax.ShapeDtypeStruct((t, HALF), jnp.uint32),
            jax.ShapeDtypeStruct((TOP_K * t // LANES, LANES), jnp.int32),
            jax.ShapeDtypeStruct((8, t), _F32),
            jax.ShapeDtypeStruct((t // tm * N_EXPERTS, LANES), _F32),
        ],
        compiler_params=pltpu.CompilerParams(
            dimension_semantics=("parallel", "parallel"), vmem_limit_bytes=VMEM_LIMIT),
        name="mixer",
    )(sinks_perm, x2, qkva, qkva, qkva, qkva, qkva, qkva, qkva, qb, kvb, kvb, kvb, gates, tbl, swa_mask,
      wua, wub, wo, g2, wr_t, br)


def _expert_kernel(bexp_ref, nact_ref, gidx_hbm, sidx_hbm, h2_hbm, w1_ref, b1_ref, w2_ref, b2_ref,
                   y_hbm, gs0, gs1, ss0, ss1, xbuf0, xbuf1, ybuf0, ybuf1, w1b, w2b, isem, gsem, ssem,
                   *, n_blocks, n_real_rows):
    i = pl.program_id(0)
    nact = nact_ref[0]
    eb = EXPERT_BLOCK
    gs, ss, xbuf, ybuf = (gs0, gs1), (ss0, ss1), (xbuf0, xbuf1), (ybuf0, ybuf1)
    rows = pl.ds(0, eb)

    def gidx_copy(blk, slot):
        return pltpu.make_async_copy(gidx_hbm.at[blk], gs[slot], isem.at[0, slot])

    def sidx_copy(entry, slot):
        return pltpu.make_async_copy(sidx_hbm.at[entry], ss[slot], isem.at[1, slot])

    def gather_row(slot, r, priority=0):
        pltpu.make_async_copy(h2_hbm.at[pl.ds(gs[slot][r], 1), :], xbuf[slot].at[pl.ds(r, 1), :],
                              gsem.at[slot]).start(priority=priority)

    def wait_gather(slot):
        pltpu.make_async_copy(h2_hbm.at[rows, :], xbuf[slot].at[rows, :], gsem.at[slot]).wait()

    def scatter_row(slot, r, priority=0):
        pltpu.make_async_copy(ybuf[slot].at[pl.ds(r, 1), :], y_hbm.at[pl.ds(ss[slot][r], 1), :],
                              ssem.at[slot]).start(priority=priority)

    def wait_scatter(slot):
        pltpu.make_async_copy(ybuf[slot].at[rows, :], y_hbm.at[rows, :], ssem.at[slot]).wait()

    def rolled(fn, slot):
        def body(r, c):
            fn(slot, r)
            return c
        lax.fori_loop(0, eb, body, 0)

    @pl.when(i == 0)
    def _():
        gidx_copy(0, 0).start()
        gidx_copy(0, 0).wait()
        rolled(gather_row, 0)
        gidx_copy(jnp.minimum(1, n_blocks - 1), 1).start()
        sidx_copy(0, 1).start()
        ybuf1[...] = jnp.zeros(ybuf1.shape, jnp.uint32)
        init = pltpu.make_async_copy(ybuf1.at[rows, :], y_hbm.at[pl.ds(n_real_rows, eb), :], ssem.at[0])
        init.start()
        init.wait()

    def step(cur):
        nxt = 1 - cur
        gidx_copy(jnp.minimum(i + 2, n_blocks - 1), cur).start()
        sidx_copy(i + 1, cur).start()
        gidx_copy(0, nxt).wait()
        sidx_copy(0, nxt).wait()
        wait_gather(cur)

        x_lo, x_hi = (half.astype(_BF16) for half in _unpack_rows(xbuf[cur][rows, :]))

        def x_dot(c0):
            return (jnp.dot(x_lo, w1b[0:HALF, c0:c0 + FF_CHUNK], preferred_element_type=_F32)
                    + jnp.dot(x_hi, w1b[HALF:, c0:c0 + FF_CHUNK], preferred_element_type=_F32))

        n_chunks = D_FF // FF_CHUNK
        rows_per_group = 2 * eb // n_chunks
        spare = pl.ds(pl.multiple_of(eb + jnp.minimum(i, 0) * SUBLANES, SUBLANES), SUBLANES)
        y = None
        for jc in range(n_chunks):
            issue = gather_row if jc < n_chunks // 2 else scatter_row
            first = (jc % (n_chunks // 2)) * rows_per_group
            for r in range(first, first + rows_per_group):
                issue(nxt, r, r % 2)
            xbuf[nxt][spare, 0:FF_CHUNK] = jnp.zeros((SUBLANES, FF_CHUNK), jnp.uint32)
            ybuf[nxt][spare, 0:FF_CHUNK] = jnp.zeros((SUBLANES, FF_CHUNK), jnp.uint32)
            anchor = lax.bitcast_convert_type(
                (xbuf[nxt][spare, 0:FF_CHUNK] | ybuf[nxt][spare, 0:FF_CHUNK])[0:1], _F32)
            c0 = jc * FF_CHUNK
            g = x_dot(c0) + (b1_ref[:, c0:c0 + FF_CHUNK] + anchor)
            u = x_dot(D_FF + c0) + b1_ref[:, D_FF + c0:D_FF + c0 + FF_CHUNK]
            gate = jnp.minimum(g, SWIGLU_LIMIT)
            up = jnp.clip(u, -SWIGLU_LIMIT, SWIGLU_LIMIT)
            act = ((up + 1.0) * (gate * jax.nn.sigmoid(gate * SWIGLU_ALPHA))).astype(_BF16)
            part = jnp.dot(act, w2b[c0:c0 + FF_CHUNK, :], preferred_element_type=_F32)
            y = part + b2_ref[...] if y is None else y + part

        @pl.when(i >= 1)
        def _():
            wait_scatter(cur)

        ybuf[cur][rows, :] = _pack_rows(y[:, :HALF].astype(_BF16), y[:, HALF:].astype(_BF16))

        @pl.when(i == nact - 1)
        def _():
            sidx_copy(0, cur).wait()
            rolled(scatter_row, cur)
            wait_scatter(nxt)
            wait_scatter(cur)
            wait_gather(nxt)
            gidx_copy(0, cur).wait()

    new_expert = jnp.logical_or(i == 0, bexp_ref[i] != bexp_ref[jnp.maximum(i - 1, 0)])

    @pl.when(jnp.logical_and(i < nact, new_expert))
    def _():
        w1b[...] = w1_ref[...].astype(_BF16)
        w2b[...] = w2_ref[...].astype(_BF16)

    for parity in range(2):
        @pl.when(jnp.logical_and(i < nact, i % 2 == parity))
        def _():
            step(parity)


def _experts(blk_exp, n_active, gidx, sidx, h2, w1, b1, w2, b2, n_tokens):
    n_blk = gidx.shape[0]
    eb = EXPERT_BLOCK
    n_real = TOP_K * n_tokens
    wmap = lambda i, be, na: (be[i], 0, 0)
    grid_spec = pltpu.PrefetchScalarGridSpec(
        num_scalar_prefetch=2,
        grid=(n_blk,),
        in_specs=[
            pl.BlockSpec(memory_space=pl.ANY),
            pl.BlockSpec(memory_space=pl.ANY),
            pl.BlockSpec(memory_space=pl.ANY),
            pl.BlockSpec((None, D_MODEL, 2 * D_FF), wmap),
            pl.BlockSpec((None, 1, 2 * D_FF), wmap),
            pl.BlockSpec((None, D_FF, D_MODEL), wmap),
            pl.BlockSpec((None, 1, D_MODEL), wmap),
        ],
        out_specs=pl.BlockSpec(memory_space=pl.ANY),
        scratch_shapes=[
            pltpu.SMEM((eb,), jnp.int32), pltpu.SMEM((eb,), jnp.int32),
            pltpu.SMEM((eb,), jnp.int32), pltpu.SMEM((eb,), jnp.int32),
            pltpu.VMEM((eb + SUBLANES, HALF), jnp.uint32),
            pltpu.VMEM((eb + SUBLANES, HALF), jnp.uint32),
            pltpu.VMEM((eb + SUBLANES, HALF), jnp.uint32),
            pltpu.VMEM((eb + SUBLANES, HALF), jnp.uint32),
            pltpu.VMEM((D_MODEL, 2 * D_FF), _BF16),
            pltpu.VMEM((D_FF, D_MODEL), _BF16),
            pltpu.SemaphoreType.DMA((2, 2)),
            pltpu.SemaphoreType.DMA((2,)),
            pltpu.SemaphoreType.DMA((2,)),
        ],
    )
    return pl.pallas_call(
        functools.partial(_expert_kernel, n_blocks=n_blk, n_real_rows=n_real),
        grid_spec=grid_spec,
        out_shape=jax.ShapeDtypeStruct((n_real + 2 * eb, HALF), jnp.uint32),
        compiler_params=pltpu.CompilerParams(
            dimension_semantics=("arbitrary",), vmem_limit_bytes=VMEM_LIMIT),
        name="experts",
    )(blk_exp, n_active, gidx, sidx, h2, w1, b1, w2, b2)


def _combine_kernel(x1_ref, y0_ref, y1_ref, y2_ref, y3_ref, wts_ref, g_ref, o_ref):
    wt = wts_ref[...].T
    y = x1_ref[...]
    for k, y_ref in enumerate((y0_ref, y1_ref, y2_ref, y3_ref)):
        y = y + wt[:, k:k + 1] * jnp.concatenate(_unpack_rows(y_ref[...]), axis=1)
    var = jnp.mean(y * y, axis=-1, keepdims=True)
    o_ref[...] = y * lax.rsqrt(var + RMS_EPS) * g_ref[...]


def _combine(x1, y, wts, gf):
    t = x1.shape[0]
    tm = TM_OUT
    nt = t // tm
    yspec = lambda k: pl.BlockSpec((tm, HALF), lambda i: (k * nt + i, 0))
    return pl.pallas_call(
        _combine_kernel,
        grid=(nt,),
        in_specs=[
            pl.BlockSpec((tm, D_MODEL), lambda i: (i, 0)),
            yspec(0), yspec(1), yspec(2), yspec(3),
            pl.BlockSpec((8, tm), lambda i: (0, i)),
            _const_spec((1, D_MODEL)),
        ],
        out_specs=pl.BlockSpec((tm, D_MODEL), lambda i: (i, 0)),
        out_shape=jax.ShapeDtypeStruct((t, D_MODEL), _F32),
        compiler_params=pltpu.CompilerParams(
            dimension_semantics=("parallel",), vmem_limit_bytes=VMEM_LIMIT),
        name="combine",
    )(x1, y, y, y, y, wts, gf)


def _split_bf16(w):
    hi = w.astype(_BF16)
    lo = (w - hi.astype(_F32)).astype(_BF16)
    return jnp.concatenate([hi, lo], axis=0)


def _rope_tables(seq):
    half = SWA_HEAD_DIM // 2
    inv_freq = np.float32(ROPE_THETA) ** (-np.arange(half, dtype=np.float32) / np.float32(half))
    ang = np.arange(seq, dtype=np.float32)[:, None] * inv_freq[None, :]
    cos, sin = np.cos(ang), np.sin(ang)
    cos_t = np.tile(np.concatenate([cos, cos], axis=1), (1, LANES // SWA_HEAD_DIM))
    sin_t = np.tile(np.concatenate([-sin, sin], axis=1), (1, LANES // SWA_HEAD_DIM))
    return jnp.asarray(cos_t, _F32), jnp.asarray(sin_t, _F32)


def _na_bias_table(rpb):
    col = np.arange(GRID_W)
    cstart = np.clip(col - NA_COLS // 2, 0, GRID_W - NA_COLS)
    kc = np.arange(GRID_W)
    valid = (kc[None, :] >= cstart[:, None]) & (kc[None, :] < cstart[:, None] + NA_COLS)
    off = kc[None, :] - col[:, None] + NA_COLS - 1
    pick = (off[:, :, None] == np.arange(2 * NA_COLS - 1)[None, None, :]) & valid[:, :, None]
    ext = jnp.einsum("hvo,cko->hvck", rpb.astype(_F32), jnp.asarray(pick, _F32),
                     precision=lax.Precision.HIGHEST)
    ext = jnp.where(valid[None, None], ext, NEG_BIG)
    tbl = jnp.stack([ext[:, NA_ROWS - 1 - d_:2 * NA_ROWS - 1 - d_] for d_ in range(NA_ROWS)], axis=1)
    tbl = tbl.transpose(1, 0, 3, 2, 4).reshape(NA_ROWS, NA_HEADS // 2, 2 * GRID_W, NA_ROWS * GRID_W)
    return (tbl * LOG2E).astype(_F32)


def kernel(x, norm1_g, w_in, b_in, na_rpb, swa_sinks, w_up_a, w_up_b, w_out, norm2_g, w_router,
           b_router, w1, b1, w2, b2, final_g):
    batch, seq, d = x.shape
    depth = w_in.shape[0]
    t = batch * seq
    assert depth == 1, "the final norm is fused into the single layer's combine step"
    assert d == D_MODEL and seq % TM_MIX == 0 and seq % TM_PROJ == 0 and t % TM_OUT == 0
    assert seq // GRID_W >= 2 * NA_ROWS

    group = SWA_Q_HEADS // SWA_KV_HEADS
    head_order = np.arange(SWA_Q_HEADS).reshape(SWA_KV_HEADS, group).T.reshape(-1)

    def reorder_heads(a, axis, start):
        take = lambda lo, hi: lax.slice_in_dim(a, lo, hi, axis=axis)
        heads = [take(start + h * SWA_HEAD_DIM, start + (h + 1) * SWA_HEAD_DIM) for h in head_order]
        return jnp.concatenate([take(0, start)] + heads + [take(start + SWA_Q_WIDTH, a.shape[axis])], axis=axis)

    cos_t, sin_t = _rope_tables(seq)

    n_assign = t * TOP_K
    assert n_assign <= PAD_FLAG
    n_rows = n_assign + N_EXPERTS * EXPERT_BLOCK
    n_blk = n_rows // EXPERT_BLOCK
    n_tiles = t // TM_MIX

    x2 = x.reshape(t, d)
    for l in range(depth):
        w_in_l = w_in[l].astype(_BF16)
        b_in_l = b_in[l].reshape(1, D_IN)
        qkva, qb, kvb, gates = _inproj(x2, norm1_g[l].reshape(1, d), w_in_l, b_in_l, cos_t, sin_t, seq)

        x1, h2, keys, wts, cnt = _mixer(
            swa_sinks[l][head_order].astype(_F32) * LOG2E, x2, qkva, qb, kvb, gates, _na_bias_table(na_rpb[l]),
            w_up_a[l].astype(_BF16), reorder_heads(w_up_b[l], 0, 0).astype(_BF16), w_out[l].astype(_BF16),
            norm2_g[l].reshape(1, d), _split_bf16(w_router[l].T), b_router[l].reshape(N_EXPERTS, 1),
            batch, seq)

        cnt = cnt.reshape(n_tiles, N_EXPERTS, LANES)[:, :, 0].astype(jnp.int32)
        counts = jnp.sum(cnt, axis=0)
        padded = (counts + EXPERT_BLOCK - 1) // EXPERT_BLOCK * EXPERT_BLOCK
        pend = jnp.cumsum(padded)
        pad_i = jnp.arange(EXPERT_BLOCK, dtype=jnp.int32)[None, :]
        pad_keys = jnp.where(pad_i < (padded - counts)[:, None],
                             (jnp.arange(N_EXPERTS, dtype=jnp.int32)[:, None] << KEY_SHIFT) | PAD_FLAG | pad_i,
                             jnp.iinfo(jnp.int32).max)
        sorted_keys = jnp.sort(jnp.concatenate([keys.reshape(-1), pad_keys.reshape(-1)]), stable=False)
        row_a = jnp.where((sorted_keys & PAD_FLAG) == 0, sorted_keys & (PAD_FLAG - 1), -1)
        row_a = row_a.reshape(n_blk, EXPERT_BLOCK)
        blk_start = jnp.arange(n_blk, dtype=jnp.int32) * EXPERT_BLOCK
        blk_exp = jnp.minimum(jnp.sum((pend[None, :] <= blk_start[:, None]).astype(jnp.int32), axis=1),
                              N_EXPERTS - 1)
        n_active = (pend[-1:] // EXPERT_BLOCK).astype(jnp.int32)
        r_in_blk = jnp.arange(EXPERT_BLOCK, dtype=jnp.int32)[None, :]
        parity = (jnp.arange(-1, n_blk, dtype=jnp.int32) % 2)[:, None]
        trash = n_assign + parity * EXPERT_BLOCK + r_in_blk
        gidx = jnp.where(row_a >= 0, row_a >> 2, 0)
        sidx = jnp.where(row_a >= 0, (row_a & (TOP_K - 1)) * t + (row_a >> 2), trash[1:])
        sidx = jnp.concatenate([trash[:1], sidx], axis=0)

        y = _experts(blk_exp, n_active, gidx, sidx, h2,
                     w1[l], b1[l].reshape(N_EXPERTS, 1, 2 * D_FF),
                     w2[l], b2[l].reshape(N_EXPERTS, 1, D_MODEL), t)
        x2 = _combine(x1, y, wts, final_g.reshape(1, d))
    return x2.reshape(batch, seq, d)
```
